```python
import jax, jax.numpy as jnp
from jax import lax
import numpy as np

D_MODEL = 1024
BATCH = 2
SEQ = 8192
DEPTH = 4
DEC_BATCH = 128
DEC_SEQ = 8
PAST_LEN = 2048
PAGE_SIZE = 128

N_A = DEPTH // 2
N_B = DEPTH - N_A
POOL_WINDOWS = (2, 4, 8, 16)
N_POOL_GROUPS = len(POOL_WINDOWS)
POOL_GROUP = D_MODEL // N_POOL_GROUPS
POOL_STATE = max(POOL_WINDOWS) - 1
N_HEADS = 16
HEAD_DIM = D_MODEL // N_HEADS
N_KV = 4
GROUP = N_HEADS // N_KV
CMP_LEN = 32
CMP_STRIDE = 16
CMP_HIDDEN = 256
SLC_LEN = 64
TOP_N = 16
WINDOW = 512
Q_BLOCK = 128
D_FF = 2816
CONV_W = 3
KV_SLOTS = 6
PAGED_SLOTS = 4
DN_ALPHA = (2 * DEPTH) ** 0.25
DN_BETA = (8 * DEPTH) ** -0.25
LN_EPS = 1e-5
NEG = -1e30
FORCE_BONUS = 1e4

kernel_name = 'pool_yoco_nsa_convffn_step'


def layer_norm(x, g, b):
    xf = x.astype(jnp.float32)
    mu = jnp.mean(xf, axis=-1, keepdims=True)
    var = jnp.mean(jnp.square(xf - mu), axis=-1, keepdims=True)
    return (((xf - mu) * lax.rsqrt(var + LN_EPS)) * g + b).astype(x.dtype)


def ada_params(c, w, b):
    m = jax.nn.silu(c) @ w + b
    return jnp.split(m[:, None, :], 6, axis=-1)


def modulate(x, shift, scale):
    return x * (1 + scale) + shift


def post_norm(x, delta, gate, g, b):
    return layer_norm(DN_ALPHA * x + (1 + gate) * delta, g, b)


def pool_mixer(u, prev, start_pos, w_pool, ls):
    B, T, D = u.shape
    P = POOL_STATE
    ext = jnp.concatenate([prev, u], axis=1)
    extf = ext.astype(jnp.float32)
    csum = jnp.concatenate([jnp.zeros_like(extf[:, :1]), jnp.cumsum(extf, axis=1)], axis=1)
    pos = start_pos + jnp.arange(T)
    groups = []
    for gi, w in enumerate(POOL_WINDOWS):
        sl = slice(gi * POOL_GROUP, (gi + 1) * POOL_GROUP)
        win_sum = csum[:, P + 1:P + T + 1, sl] - csum[:, P + 1 - w:P + T + 1 - w, sl]
        count = jnp.minimum(pos + 1, w).astype(jnp.float32)[None, :, None]
        groups.append(win_sum / count - extf[:, P:, sl])
    pooled = jnp.stack(groups, axis=2).astype(u.dtype)
    mixed = jnp.einsum('btgc,gce->btge', pooled, w_pool).reshape(B, T, D)
    return mixed * ls, ext[:, -P:]


def conv_ffn(u, prev, w_up, b_up, w_conv, b_conv, w_down):
    T = u.shape[1]
    h = u @ w_up + b_up
    ext = jnp.concatenate([prev, h], axis=1)
    hc = b_conv + ext[:, 0:T] * w_conv[0]
    for k in range(1, CONV_W):
        hc = hc + ext[:, k:k + T] * w_conv[k]
    a, v = jnp.split(hc, 2, axis=-1)
    return (jax.nn.silu(a) * v) @ w_down, ext[:, -(CONV_W - 1):]


def compress(rows, pe, w1, b1, w2, b2):
    B, L = rows.shape[:2]
    nc = (L - CMP_LEN) // CMP_STRIDE + 1
    idx = jnp.arange(nc)[:, None] * CMP_STRIDE + jnp.arange(CMP_LEN)[None, :]
    blk = rows[:, idx] + pe[None, None, :, None, :]
    blk = jnp.moveaxis(blk, 2, 3).reshape(B, nc, N_KV, CMP_LEN * HEAD_DIM)
    return jax.nn.gelu(blk @ w1 + b1) @ w2 + b2


def kv_context(paged_rows, win_rows, cmp_pe, cmp_w1, cmp_b1, cmp_w2, cmp_b2):
    B, L = paged_rows.shape[:2]
    kc = compress(paged_rows[:, :, 0], cmp_pe[0], cmp_w1[0], cmp_b1[0], cmp_w2[0], cmp_b2[0])
    vc = compress(paged_rows[:, :, 1], cmp_pe[1], cmp_w1[1], cmp_b1[1], cmp_w2[1], cmp_b2[1])
    cend = jnp.arange(kc.shape[1]) * CMP_STRIDE + CMP_LEN - 1
    ns = -(-L // SLC_LEN)
    slc = jnp.pad(paged_rows[:, :, 2:4], ((0, 0), (0, ns * SLC_LEN - L), (0, 0), (0, 0), (0, 0)))
    slc = slc.reshape(B, ns, SLC_LEN, 2, N_KV, HEAD_DIM).transpose(3, 0, 4, 1, 2, 5)
    return (kc, vc, cend, slc[0], slc[1], win_rows[:, :, 0], win_rows[:, :, 1])


def nsa_core(q, gates, qpos, kc, vc, cend, ks_blk, vs_blk, kw, vw, kwpos):
    B, Tq = q.shape[:2]
    scale = HEAD_DIM ** -0.5
    qg = q.reshape(B, Tq, N_KV, GROUP, HEAD_DIM)
    s_c = jnp.einsum('btgrd,bngd->bgrtn', qg, kc).astype(jnp.float32) * scale
    m_c = cend[None, :] <= qpos[:, None]
    p_c = jax.nn.softmax(jnp.where(m_c, s_c, NEG), axis=-1) * m_c
    o_c = jnp.einsum('bgrtn,bngd->btgrd', p_c.astype(q.dtype), vc)
    n_c = kc.shape[1]
    ns = ks_blk.shape[2]
    ci = jnp.arange(n_c)[:, None] * CMP_STRIDE
    sj = jnp.arange(ns)[None, :] * SLC_LEN
    overlap = ((ci <= sj + SLC_LEN - 1) & (ci + CMP_LEN - 1 >= sj)).astype(jnp.float32)
    imp = jnp.einsum('bgrtn,nj->bgtj', p_c, overlap)
    jj = jnp.arange(ns)[None, :]
    cur = (qpos // SLC_LEN)[:, None]
    forced = (jj == 0) | (jj == cur) | (jj == cur - 1)
    valid = jj * SLC_LEN <= qpos[:, None]
    imp = jnp.where(valid, imp + FORCE_BONUS * forced, NEG)
    n_sel = min(TOP_N, ns)
    _, sel = lax.top_k(imp, n_sel)
    gather = jax.vmap(jax.vmap(lambda blk, ix: blk[ix]))
    ksel = gather(ks_blk, sel)
    vsel = gather(vs_blk, sel)
    s_s = jnp.einsum('btgrd,bgtnkd->bgrtnk', qg, ksel).astype(jnp.float32) * scale
    kpos = sel[..., None] * SLC_LEN + jnp.arange(SLC_LEN)
    m_s = (kpos <= qpos[None, None, :, None, None])[:, :, None]
    s_s = jnp.where(m_s, s_s, NEG).reshape(B, N_KV, GROUP, Tq, n_sel * SLC_LEN)
    p_s = jax.nn.softmax(s_s, axis=-1).reshape(B, N_KV, GROUP, Tq, n_sel, SLC_LEN)
    o_s = jnp.einsum('bgrtnk,bgtnkd->btgrd', p_s.astype(q.dtype), vsel)
    s_w = jnp.einsum('btgrd,bkgd->bgrtk', qg, kw).astype(jnp.float32) * scale
    dist = qpos[:, None] - kwpos[None, :]
    m_w = (dist >= 0) & (dist < WINDOW) & (kwpos[None, :] >= 0)
    p_w = jax.nn.softmax(jnp.where(m_w, s_w, NEG), axis=-1)
    o_w = jnp.einsum('bgrtk,bkgd->btgrd', p_w.astype(q.dtype), vw)
    o = jnp.stack([o_c, o_s, o_w], axis=-1).reshape(B, Tq, N_HEADS, HEAD_DIM, 3)
    return jnp.einsum('bthdi,bthi->bthd', o, gates)


def nsa_prompt(q, gates, ctx):
    kc, vc, cend, ks_blk, vs_blk, kw, vw = ctx
    B, T = q.shape[:2]
    nqb = T // Q_BLOCK
    pad = ((0, 0), (WINDOW, 0), (0, 0), (0, 0))
    kw_pad = jnp.pad(kw, pad)
    vw_pad = jnp.pad(vw, pad)

    def one_block(i):
        s = i * Q_BLOCK
        qb = lax.dynamic_slice_in_dim(q, s, Q_BLOCK, axis=1)
        gb = lax.dynamic_slice_in_dim(gates, s, Q_BLOCK, axis=1)
        kwb = lax.dynamic_slice_in_dim(kw_pad, s, WINDOW + Q_BLOCK, axis=1)
        vwb = lax.dynamic_slice_in_dim(vw_pad, s, WINDOW + Q_BLOCK, axis=1)
        qpos = s + jnp.arange(Q_BLOCK)
        kwpos = s - WINDOW + jnp.arange(WINDOW + Q_BLOCK)
        return nsa_core(qb, gb, qpos, kc, vc, cend, ks_blk, vs_blk, kwb, vwb, kwpos)

    out = lax.map(one_block, jnp.arange(nqb))
    return jnp.moveaxis(out, 0, 1).reshape(B, T, N_HEADS, HEAD_DIM)


def nsa_sample(q, gates, ctx):
    kc, vc, cend, ks_blk, vs_blk, kw, vw = ctx
    T = q.shape[1]
    Lw = kw.shape[1]
    qpos = PAST_LEN + jnp.arange(T)
    kwpos = PAST_LEN + T - Lw + jnp.arange(Lw)
    return nsa_core(q, gates, qpos, kc, vc, cend, ks_blk, vs_blk, kw, vw, kwpos)


def run_trunk(x, c, pool_prev, conv_prev, start_pos, make_ctx, attend,
              ada_w, ada_b, ln_g, ln_b, pool_w, pool_ls,
              ffn_w_up, ffn_b_up, ffn_w_conv, ffn_b_conv, ffn_w_down,
              w_kv, nsa_w_qg, nsa_w_o):
    B, T, _ = x.shape
    hq = N_HEADS * HEAD_DIM
    pool_new, conv_new = [], []
    kv, ctx = None, None
    for l in range(DEPTH):
        sh1, sc1, g1, sh2, sc2, g2 = ada_params(c, ada_w[l], ada_b[l])
        u = modulate(x, sh1, sc1)
        if l < N_A:
            mix, pst = pool_mixer(u, pool_prev[l], start_pos, pool_w[l], pool_ls[l])
            pool_new.append(pst)
        else:
            j = l - N_A
            qgt = u @ nsa_w_qg[j]
            q = qgt[..., :hq].reshape(B, T, N_HEADS, HEAD_DIM)
            gates = jax.nn.sigmoid(qgt[..., hq:]).reshape(B, T, N_HEADS, 3)
            mix = attend(q, gates, ctx).reshape(B, T, hq) @ nsa_w_o[j]
        x = post_norm(x, mix, g1, ln_g[l, 0], ln_b[l, 0])
        f, cst = conv_ffn(modulate(x, sh2, sc2), conv_prev[l], ffn_w_up[l], ffn_b_up[l],
                          ffn_w_conv[l], ffn_b_conv[l], ffn_w_down[l])
        conv_new.append(cst)
        x = post_norm(x, f, g2, ln_g[l, 1], ln_b[l, 1])
        if l == N_A - 1:
            kv = (x @ w_kv).reshape(B, T, KV_SLOTS, N_KV, HEAD_DIM)
            ctx = make_ctx(kv)
    return x, kv, jnp.stack(pool_new), jnp.stack(conv_new)


def setup_inputs(seed: int = 0) -> dict:
    key = jax.random.key(seed)
    keys = iter(jax.random.split(key, 40))

    def nrm(shape, s):
        return jax.random.normal(next(keys), shape, jnp.float32) * s

    n_pages = PAST_LEN // PAGE_SIZE
    n_used = DEC_BATCH * n_pages
    n_phys = n_used + max(1, n_used // 4)
    wbuf = min(WINDOW, PAST_LEN)
    page_table = jax.random.permutation(next(keys), n_phys)[:n_used].reshape(DEC_BATCH, n_pages).astype(jnp.int32)
    d, f2, hq = D_MODEL, 2 * D_FF, N_HEADS * HEAD_DIM
    return {
        'x_prompt': nrm((BATCH, SEQ, d), 1.0),
        'x_sample': nrm((DEC_BATCH, DEC_SEQ, d), 1.0),
        'cache_kv': nrm((n_phys, PAGE_SIZE, PAGED_SLOTS, N_KV, HEAD_DIM), 1.0),
        'state_kv_win': nrm((DEC_BATCH, wbuf, 2, N_KV, HEAD_DIM), 1.0),
        'state_pool': nrm((N_A, DEC_BATCH, POOL_STATE, d), 1.0),
        'state_conv': nrm((DEPTH, DEC_BATCH, CONV_W - 1, f2), 1.0),
        'page_table': page_table,
        'c_prompt': nrm((BATCH, d), 1.0),
        'c_sample': nrm((DEC_BATCH, d), 1.0),
        'ada_w': nrm((DEPTH, d, 6 * d), 0.5 * d ** -0.5),
        'ada_b': nrm((DEPTH, 6 * d), 0.01),
        'ln_g': 1.0 + nrm((DEPTH, 2, d), 0.05),
        'ln_b': nrm((DEPTH, 2, d), 0.02),
        'pool_w': nrm((N_A, N_POOL_GROUPS, POOL_GROUP, POOL_GROUP), DN_BETA * POOL_GROUP ** -0.5),
        'pool_ls': 1.0 + nrm((N_A, d), 0.05),
        'ffn_w_up': nrm((DEPTH, d, f2), d ** -0.5),
        'ffn_b_up': nrm((DEPTH, f2), 0.01),
        'ffn_w_conv': nrm((DEPTH, CONV_W, f2), CONV_W ** -0.5),
        'ffn_b_conv': nrm((DEPTH, f2), 0.01),
        'ffn_w_down': nrm((DEPTH, D_FF, d), DN_BETA * D_FF ** -0.5),
        'w_kv': nrm((d, KV_SLOTS * N_KV * HEAD_DIM), d ** -0.5),
        'cmp_pe': nrm((2, CMP_LEN, HEAD_DIM), 0.1),
        'cmp_w1': nrm((2, CMP_LEN * HEAD_DIM, CMP_HIDDEN), (CMP_LEN * HEAD_DIM) ** -0.5),
        'cmp_b1': nrm((2, CMP_HIDDEN), 0.01),
        'cmp_w2': nrm((2, CMP_HIDDEN, HEAD_DIM), 2.0 * CMP_HIDDEN ** -0.5),
        'cmp_b2': nrm((2, HEAD_DIM), 0.01),
        'nsa_w_qg': nrm((N_B, d, hq + 3 * N_HEADS), d ** -0.5),
        'nsa_w_o': nrm((N_B, hq, d), DN_BETA * hq ** -0.5),
    }


def reference(x_prompt, x_sample, cache_kv, state_kv_win, state_pool, state_conv, page_table,
              c_prompt, c_sample, ada_w, ada_b, ln_g, ln_b, pool_w, pool_ls,
              ffn_w_up, ffn_b_up, ffn_w_conv, ffn_b_conv, ffn_w_down,
              w_kv, cmp_pe, cmp_w1, cmp_b1, cmp_w2, cmp_b2, nsa_w_qg, nsa_w_o):
    weights = (ada_w, ada_b, ln_g, ln_b, pool_w, pool_ls, ffn_w_up, ffn_b_up, ffn_w_conv,
               ffn_b_conv, ffn_w_down, w_kv, nsa_w_qg, nsa_w_o)
    wbuf = min(WINDOW, PAST_LEN)
    n_pages = PAST_LEN // PAGE_SIZE

    def prompt_ctx(kv):
        return kv_context(kv[:, :, :PAGED_SLOTS], kv[:, :, PAGED_SLOTS:],
                          cmp_pe, cmp_w1, cmp_b1, cmp_w2, cmp_b2)

    pool0 = jnp.zeros((N_A, BATCH, POOL_STATE, D_MODEL), x_prompt.dtype)
    conv0 = jnp.zeros((DEPTH, BATCH, CONV_W - 1, 2 * D_FF), x_prompt.dtype)
    y_prompt, kv_p, pool_p, conv_p = run_trunk(x_prompt, c_prompt, pool0, conv0, 0,
                                               prompt_ctx, nsa_prompt, *weights)

    def sample_ctx(kv):
        past = cache_kv[page_table].reshape(DEC_BATCH, n_pages * PAGE_SIZE, PAGED_SLOTS, N_KV, HEAD_DIM)
        rows = jnp.concatenate([past, kv[:, :, :PAGED_SLOTS]], axis=1)
        win = jnp.concatenate([state_kv_win, kv[:, :, PAGED_SLOTS:]], axis=1)
        return kv_context(rows, win, cmp_pe, cmp_w1, cmp_b1, cmp_w2, cmp_b2)

    y_sample, kv_s, pool_s, conv_s = run_trunk(x_sample, c_sample, state_pool, state_conv, PAST_LEN,
                                               sample_ctx, nsa_sample, *weights)
    win_s = jnp.concatenate([state_kv_win, kv_s[:, :, PAGED_SLOTS:]], axis=1)[:, -wbuf:]
    win_p = kv_p[:, -min(WINDOW, SEQ):, PAGED_SLOTS:]
    return (y_prompt, y_sample, kv_p[:, :, :PAGED_SLOTS], kv_s[:, :, :PAGED_SLOTS], win_p, win_s,
            pool_p, pool_s, conv_p, conv_s)
```

```python
import functools

import numpy as np
import jax
import jax.numpy as jnp
from jax import lax
from jax.experimental import pallas as pl
from jax.experimental.pallas import tpu as pltpu

F32 = jnp.float32
BF16 = jnp.bfloat16

POOL_WINDOWS = (2, 4, 8, 16)
POOL_HALO = 16
N_HEADS = 16
HEAD_DIM = 64
N_KV = 4
GROUP = N_HEADS // N_KV
CMP_LEN = 32
CMP_STRIDE = 16
SLC_LEN = 64
SLC_SHIFT = 6
TOP_N = 16
WINDOW = 512
Q_BLOCK = 128
CONV_W = 3
CONV_HALO = 8
PAGE = 128
LN_EPS = 1e-5
NEG = -1e30
FORCE_BONUS = 1e4
KV_CHUNK = 256
VMEM_LIMIT = 56 * 1024 * 1024

_NT = (((1,), (1,)), ((), ()))


def _params(*sem):
    return pltpu.CompilerParams(dimension_semantics=sem, vmem_limit_bytes=VMEM_LIMIT)


def _const_spec(shape):
    nd = len(shape)
    return pl.BlockSpec(shape, lambda *_: (0,) * nd, pipeline_mode=pl.Buffered(1))


def _layer_norm(y, g, b):
    mu = jnp.mean(y, axis=-1, keepdims=True)
    d = y - mu
    var = jnp.mean(d * d, axis=-1, keepdims=True)
    return d * lax.rsqrt(var + LN_EPS) * g + b


def _split3(x):
    hi = x.astype(BF16)
    r1 = x - hi.astype(F32)
    mid = r1.astype(BF16)
    lo = (r1 - mid.astype(F32)).astype(BF16)
    return hi, mid, lo


def _ada_kernel(c_ref, w_ref, b_ref, o_ref):
    c = c_ref[...]
    s = c * jax.nn.sigmoid(c)
    o_ref[0] = jnp.dot(s.astype(BF16), w_ref[0].astype(BF16), preferred_element_type=F32) + b_ref[0]


def _ada(c_all, ada_w, ada_b):
    depth, d, n = ada_w.shape
    r = c_all.shape[0]
    tn = n // 4
    return pl.pallas_call(
        _ada_kernel,
        grid=(depth, n // tn),
        in_specs=[pl.BlockSpec((r, d), lambda l, j: (0, 0)),
                  pl.BlockSpec((1, d, tn), lambda l, j: (l, 0, j)),
                  pl.BlockSpec((1, 1, tn), lambda l, j: (l, 0, j))],
        out_specs=pl.BlockSpec((1, r, tn), lambda l, j: (l, 0, j)),
        out_shape=jax.ShapeDtypeStruct((depth, r, n), F32),
        compiler_params=_params("parallel", "parallel"),
        name="ada_params",
    )(c_all, ada_w, ada_b.reshape(depth, 1, n))


def _pool_kernel(x_ref, mod_ref, prev_ref, pw_ref, ls_ref, g_ref, b_ref, o_ref, ps_ref, ext_ref,
                 *, start_pos, nj, alpha):
    j = pl.program_id(1)
    sb, tm, d = x_ref.shape
    pg = d // len(POOL_WINDOWS)
    h = POOL_HALO
    x = x_ref[...]
    shift = mod_ref[:, 0:1, :]
    scale = mod_ref[:, 1:2, :]
    gate = mod_ref[:, 2:3, :]
    u = x * (1.0 + scale) + shift

    @pl.when(j == 0)
    def _():
        ext_ref[:, 0:h, :] = prev_ref[...]

    ext_ref[:, h:h + tm, :] = u
    pos = start_pos + j * tm + lax.broadcasted_iota(jnp.int32, (1, tm, pg), 1)
    outs = []
    for gi, w in enumerate(POOL_WINDOWS):
        c0 = gi * pg
        acc = ext_ref[:, h:h + tm, c0:c0 + pg]
        for k in range(1, w):
            acc = acc + ext_ref[:, h - k:h - k + tm, c0:c0 + pg]
        cnt = jnp.minimum(pos + 1, w).astype(F32)
        pooled = acc / cnt - u[:, :, c0:c0 + pg]
        outs.append(jnp.dot(pooled.reshape(sb * tm, pg).astype(BF16), pw_ref[gi],
                            preferred_element_type=F32))
    mixed = jnp.concatenate(outs, axis=-1) * ls_ref[...]
    y = alpha * x + (1.0 + gate) * mixed.reshape(sb, tm, d)
    o_ref[...] = _layer_norm(y, g_ref[...], b_ref[...])

    if nj > 1:
        @pl.when(j < nj - 1)
        def _():
            ext_ref[:, 0:h, :] = ext_ref[:, tm:tm + h, :]

    @pl.when(j == nj - 1)
    def _():
        ps_ref[...] = ext_ref[:, tm:tm + h, :]


def _pool_layer(x, mod, prev, pw, ls, g, b, *, sb, tm, start_pos, alpha):
    bsz, t, d = x.shape
    nj = t // tm
    h = POOL_HALO
    kern = functools.partial(_pool_kernel, start_pos=start_pos, nj=nj, alpha=alpha)
    return pl.pallas_call(
        kern,
        grid=(bsz // sb, nj),
        in_specs=[pl.BlockSpec((sb, tm, d), lambda i, j: (i, j, 0)),
                  pl.BlockSpec((sb, 6, d), lambda i, j: (i, 0, 0)),
                  pl.BlockSpec((sb, h, d), lambda i, j: (i, 0, 0)),
                  _const_spec(pw.shape), _const_spec(ls.shape), _const_spec(g.shape), _const_spec(b.shape)],
        out_specs=[pl.BlockSpec((sb, tm, d), lambda i, j: (i, j, 0)),
                   pl.BlockSpec((sb, h, d), lambda i, j: (i, 0, 0))],
        out_shape=[jax.ShapeDtypeStruct((bsz, t, d), F32), jax.ShapeDtypeStruct((bsz, h, d), F32)],
        scratch_shapes=[pltpu.VMEM((sb, h + tm, d), F32)],
        compiler_params=_params("parallel", "arbitrary"),
        name="pool_layer",
    )(x, mod, prev, pw, ls, g, b)


def _ffn_kernel(x_ref, mod_ref, cprev_ref, wup_ref, bup_ref, wc_ref, bc_ref, wdn_ref, g_ref, b_ref,
                o_ref, cs_ref, hext_ref, *, cf, alpha):
    j = pl.program_id(1)
    sb, tm, d = x_ref.shape
    f = wdn_ref.shape[0]
    hh = CONV_HALO
    x = x_ref[...]
    shift = mod_ref[:, 3:4, :]
    scale = mod_ref[:, 4:5, :]
    gate = mod_ref[:, 5:6, :]
    ub = (x * (1.0 + scale) + shift).reshape(sb * tm, d).astype(BF16)

    @pl.when(j == 0)
    def _():
        cs_ref[...] = cprev_ref[...]

    acc = jnp.zeros((sb * tm, d), F32)
    for c in range(f // cf):
        parts = []
        for p in range(2):
            c0 = p * f + c * cf
            hp = jnp.dot(ub, wup_ref[:, c0:c0 + cf], preferred_element_type=F32) + bup_ref[:, c0:c0 + cf]
            hext_ref[p, :, 0:hh, :] = cs_ref[:, :, c0:c0 + cf]
            hext_ref[p, :, hh:hh + tm, :] = hp.reshape(sb, tm, cf)
            hc = (bc_ref[:, c0:c0 + cf]
                  + hext_ref[p, :, hh - 2:hh - 2 + tm, :] * wc_ref[0:1, c0:c0 + cf]
                  + hext_ref[p, :, hh - 1:hh - 1 + tm, :] * wc_ref[1:2, c0:c0 + cf]
                  + hext_ref[p, :, hh:hh + tm, :] * wc_ref[2:3, c0:c0 + cf])
            cs_ref[:, :, c0:c0 + cf] = hext_ref[p, :, tm:tm + hh, :]
            parts.append(hc)
        a, v = parts
        gated = (a * jax.nn.sigmoid(a)) * v
        acc = acc + jnp.dot(gated.reshape(sb * tm, cf).astype(BF16), wdn_ref[c * cf:(c + 1) * cf, :],
                            preferred_element_type=F32)
    y = alpha * x + (1.0 + gate) * acc.reshape(sb, tm, d)
    o_ref[...] = _layer_norm(y, g_ref[...], b_ref[...])


def _ffn_layer(x, mod, cprev, wup, bup, wc, bc, wdn, g, b, *, sb, tm, cf, alpha):
    bsz, t, d = x.shape
    f2 = wup.shape[1]
    hh = CONV_HALO
    kern = functools.partial(_ffn_kernel, cf=cf, alpha=alpha)
    return pl.pallas_call(
        kern,
        grid=(bsz // sb, t // tm),
        in_specs=[pl.BlockSpec((sb, tm, d), lambda i, j: (i, j, 0)),
                  pl.BlockSpec((sb, 6, d), lambda i, j: (i, 0, 0)),
                  pl.BlockSpec((sb, hh, f2), lambda i, j: (i, 0, 0)),
                  _const_spec(wup.shape), _const_spec(bup.shape), _const_spec(wc.shape),
                  _const_spec(bc.shape), _const_spec(wdn.shape), _const_spec(g.shape), _const_spec(b.shape)],
        out_specs=[pl.BlockSpec((sb, tm, d), lambda i, j: (i, j, 0)),
                   pl.BlockSpec((sb, hh, f2), lambda i, j: (i, 0, 0))],
        out_shape=[jax.ShapeDtypeStruct((bsz, t, d), F32), jax.ShapeDtypeStruct((bsz, hh, f2), F32)],
        scratch_shapes=[pltpu.VMEM((2, sb, hh + tm, cf), F32)],
        compiler_params=_params("parallel", "arbitrary"),
        name="conv_ffn_layer",
    )(x, mod, cprev, wup, bup, wc, bc, wdn, g, b)


def _kvproj_prompt_kernel(x_ref, w_ref, wvt_ref, kvp_ref, kvw_ref, ks_ref, kw_ref, vst_ref, vwt_ref):
    tm = x_ref.shape[1]
    nkv = N_KV * HEAD_DIM
    xb = x_ref[0].astype(BF16)
    kv = jnp.dot(xb, w_ref[...], preferred_element_type=F32)
    kvp_ref[0] = kv[:, :4 * nkv]
    kvw_ref[0] = kv[:, 4 * nkv:]
    ks_ref[0] = kv[:, 2 * nkv:3 * nkv].astype(BF16)
    kw_ref[0] = kv[:, 4 * nkv:5 * nkv].astype(BF16)
    vt = lax.dot_general(wvt_ref[...], xb, _NT, preferred_element_type=F32).astype(BF16)
    for c in range(tm // Q_BLOCK):
        vst_ref[0, c] = vt[0:nkv, c * Q_BLOCK:(c + 1) * Q_BLOCK]
        vwt_ref[0, c] = vt[nkv:2 * nkv, c * Q_BLOCK:(c + 1) * Q_BLOCK]


def _kvproj_prompt(x, w_kv_b, wvt_b, *, tm):
    bsz, t, d = x.shape
    nkv = N_KV * HEAD_DIM
    nb = tm // Q_BLOCK
    return pl.pallas_call(
        _kvproj_prompt_kernel,
        grid=(bsz, t // tm),
        in_specs=[pl.BlockSpec((1, tm, d), lambda i, j: (i, j, 0)),
                  _const_spec(w_kv_b.shape), _const_spec(wvt_b.shape)],
        out_specs=[pl.BlockSpec((1, tm, 4 * nkv), lambda i, j: (i, j, 0)),
                   pl.BlockSpec((1, tm, 2 * nkv), lambda i, j: (i, j, 0)),
                   pl.BlockSpec((1, tm, nkv), lambda i, j: (i, j, 0)),
                   pl.BlockSpec((1, tm, nkv), lambda i, j: (i, j, 0)),
                   pl.BlockSpec((1, nb, nkv, Q_BLOCK), lambda i, j: (i, j, 0, 0)),
                   pl.BlockSpec((1, nb, nkv, Q_BLOCK), lambda i, j: (i, j, 0, 0))],
        out_shape=[jax.ShapeDtypeStruct((bsz, t, 4 * nkv), F32),
                   jax.ShapeDtypeStruct((bsz, t, 2 * nkv), F32),
                   jax.ShapeDtypeStruct((bsz, t, nkv), BF16),
                   jax.ShapeDtypeStruct((bsz, t, nkv), BF16),
                   jax.ShapeDtypeStruct((bsz, t // Q_BLOCK, nkv, Q_BLOCK), BF16),
                   jax.ShapeDtypeStruct((bsz, t // Q_BLOCK, nkv, Q_BLOCK), BF16)],
        compiler_params=_params("parallel", "parallel"),
        name="kv_proj_prompt",
    )(x, w_kv_b, wvt_b)


def _kvproj_sample_kernel(x_ref, w_ref, kv_ref):
    sb, tm, d = x_ref.shape
    xb = x_ref[...].reshape(sb * tm, d).astype(BF16)
    kv_ref[...] = jnp.dot(xb, w_ref[...], preferred_element_type=F32).reshape(sb, tm, -1)


def _kvproj_sample(x, w_kv_b, *, sb):
    bsz, t, d = x.shape
    n = w_kv_b.shape[1]
    return pl.pallas_call(
        _kvproj_sample_kernel,
        grid=(bsz // sb,),
        in_specs=[pl.BlockSpec((sb, t, d), lambda i: (i, 0, 0)), _const_spec(w_kv_b.shape)],
        out_specs=pl.BlockSpec((sb, t, n), lambda i: (i, 0, 0)),
        out_shape=jax.ShapeDtypeStruct((bsz, t, n), F32),
        compiler_params=_params("parallel"),
        name="kv_proj_sample",
    )(x, w_kv_b)


def _compress_kernel(pt_ref, *refs):
    del pt_ref
    npg = 16
    page_refs = refs[:npg]
    (look_ref, pet_ref, peb_ref, w1t_ref, w1b_ref, b1_ref, w2_ref, w2t_ref, b2_ref, b2t_ref,
     kc_ref, vc_ref, vct_ref, slab_ref, xs_ref) = refs[npg:]
    rows = npg * PAGE
    nch = rows // CMP_STRIDE
    hd = HEAD_DIM
    for k in range(npg):
        for sl in range(4):
            slab_ref[sl, k * PAGE:(k + 1) * PAGE, :] = page_refs[k][:, sl * 128:(sl + 1) * 128]
    for sl in range(4):
        slab_ref[sl, rows:rows + CMP_STRIDE, :] = look_ref[:, sl * 128:(sl + 1) * 128]

    lane = lax.broadcasted_iota(jnp.int32, (nch, 128), 1)
    low = lane < hd
    for sl in range(4):
        s = sl // 2
        ga = 2 * (sl % 2)
        for half in range(2):
            pe_ref = pet_ref if half == 0 else peb_ref
            base = CMP_STRIDE * half
            for m in range(CMP_STRIDE // 2):
                ve = slab_ref[sl, pl.ds(base + 2 * m, nch, stride=CMP_STRIDE), :]
                vo = slab_ref[sl, pl.ds(base + 2 * m + 1, nch, stride=CMP_STRIDE), :]
                ca = jnp.where(low, ve, pltpu.roll(vo, hd, axis=1))
                cb = jnp.where(low, pltpu.roll(ve, hd, axis=1), vo)
                pe = pe_ref[s, :, m * 128:(m + 1) * 128]
                xs_ref[half, s, ga * nch:(ga + 1) * nch, m * 128:(m + 1) * 128] = (ca + pe).astype(BF16)
                xs_ref[half, s, (ga + 1) * nch:(ga + 2) * nch, m * 128:(m + 1) * 128] = (cb + pe).astype(BF16)

    for s in range(2):
        pre = (jnp.dot(xs_ref[0, s], w1t_ref[s], preferred_element_type=F32)
               + jnp.dot(xs_ref[1, s], w1b_ref[s], preferred_element_type=F32) + b1_ref[s])
        hid = jax.nn.gelu(pre).astype(BF16)
        out = jnp.dot(hid, w2_ref[s], preferred_element_type=F32) + b2_ref[s]
        nat = jnp.concatenate([out[g * nch:(g + 1) * nch, :] for g in range(N_KV)], axis=1)
        if s == 0:
            kc_ref[0] = nat.astype(BF16)
        else:
            vc_ref[0] = nat.astype(BF16)
            outs_t = [lax.dot_general(w2t_ref[s], hid[g * nch:(g + 1) * nch, :], _NT,
                                      preferred_element_type=F32) + b2t_ref[s] for g in range(N_KV)]
            vct_ref[0] = jnp.concatenate(outs_t, axis=0).astype(BF16)


def _compress(pages, ptab, pe, w1, b1, w2, b2):
    nt = ptab.shape[0]
    npg = 16
    nkv = N_KV * HEAD_DIM
    half = CMP_STRIDE * HEAD_DIM
    hidden = w1.shape[-1]
    pet = pe[:, :CMP_STRIDE].reshape(2, 1, half)
    peb = pe[:, CMP_STRIDE:].reshape(2, 1, half)
    w1t = w1[:, :half].astype(BF16)
    w1b = w1[:, half:].astype(BF16)
    w2b = w2.astype(BF16)
    w2t = jnp.swapaxes(w2, 1, 2).astype(BF16)
    b1r = b1.reshape(2, 1, hidden)
    b2r = b2.reshape(2, 1, HEAD_DIM)
    b2t = b2.reshape(2, HEAD_DIM, 1)
    nch = npg * PAGE // CMP_STRIDE

    def page_spec(k):
        return pl.BlockSpec((None, PAGE, 2 * nkv), lambda i, pt: (pt[i, k], 0, 0))

    def cs(shape):
        nd = len(shape)
        return pl.BlockSpec(shape, lambda i, pt: (0,) * nd, pipeline_mode=pl.Buffered(1))

    in_specs = [page_spec(k) for k in range(npg)]
    in_specs.append(pl.BlockSpec((None, CMP_STRIDE, 2 * nkv), lambda i, pt: (pt[i, npg], 0, 0)))
    consts = (pet, peb, w1t, w1b, b1r, w2b, w2t, b2r, b2t)
    in_specs += [cs(a.shape) for a in consts]
    grid_spec = pltpu.PrefetchScalarGridSpec(
        num_scalar_prefetch=1,
        grid=(nt,),
        in_specs=in_specs,
        out_specs=[pl.BlockSpec((1, nch, nkv), lambda i, pt: (i, 0, 0)),
                   pl.BlockSpec((1, nch, nkv), lambda i, pt: (i, 0, 0)),
                   pl.BlockSpec((1, nkv, nch), lambda i, pt: (i, 0, 0))],
        scratch_shapes=[pltpu.VMEM((4, npg * PAGE + CMP_STRIDE, 128), F32),
                        pltpu.VMEM((2, 2, N_KV * nch, half), BF16)],
    )
    return pl.pallas_call(
        _compress_kernel,
        grid_spec=grid_spec,
        out_shape=[jax.ShapeDtypeStruct((nt, nch, nkv), BF16),
                   jax.ShapeDtypeStruct((nt, nch, nkv), BF16),
                   jax.ShapeDtypeStruct((nt, nkv, nch), BF16)],
        compiler_params=_params("arbitrary"),
        name="compress",
    )(ptab, *([pages] * (npg + 1)), *consts)


def _qproj_prompt_kernel(x_ref, mod_ref, wqt_ref, wgt_ref, qt_ref, gt_ref):
    x = x_ref[0]
    shift = mod_ref[0, 0:1, :]
    scale = mod_ref[0, 1:2, :]
    ub = (x * (1.0 + scale) + shift).astype(BF16)
    qt = lax.dot_general(wqt_ref[...], ub, _NT, preferred_element_type=F32) * (HEAD_DIM ** -0.5)
    qt_ref[0] = qt.astype(BF16)
    gt_ref[0] = jax.nn.sigmoid(lax.dot_general(wgt_ref[...], ub, _NT, preferred_element_type=F32))


def _qproj_prompt(x, mod, wqt, wgt, *, tm):
    bsz, t, d = x.shape
    hq = wqt.shape[0]
    ng = wgt.shape[0]
    return pl.pallas_call(
        _qproj_prompt_kernel,
        grid=(bsz, t // tm),
        in_specs=[pl.BlockSpec((1, tm, d), lambda i, j: (i, j, 0)),
                  pl.BlockSpec((1, 6, d), lambda i, j: (i, 0, 0)),
                  _const_spec(wqt.shape), _const_spec(wgt.shape)],
        out_specs=[pl.BlockSpec((1, hq, tm), lambda i, j: (i, 0, j)),
                   pl.BlockSpec((1, ng, tm), lambda i, j: (i, 0, j))],
        out_shape=[jax.ShapeDtypeStruct((bsz, hq, t), BF16), jax.ShapeDtypeStruct((bsz, ng, t), F32)],
        compiler_params=_params("parallel", "parallel"),
        name="q_proj_prompt",
    )(x, mod, wqt, wgt)


def _qproj_sample_kernel(x_ref, mod_ref, wq_ref, wg_ref, q_ref, g_ref):
    sb, tm, d = x_ref.shape
    x = x_ref[...]
    shift = mod_ref[:, 0:1, :]
    scale = mod_ref[:, 1:2, :]
    ub = (x * (1.0 + scale) + shift).reshape(sb * tm, d).astype(BF16)
    q = jnp.dot(ub, wq_ref[...], preferred_element_type=F32) * (HEAD_DIM ** -0.5)
    q_ref[...] = q.reshape(sb, tm, -1)
    for i in range(3):
        gl = jnp.dot(ub, wg_ref[i], preferred_element_type=F32)
        g_ref[:, i] = jax.nn.sigmoid(gl).reshape(sb, tm, -1)


def _qproj_sample(x, mod, wq, wgx, *, sb):
    bsz, t, d = x.shape
    hq = wq.shape[1]
    return pl.pallas_call(
        _qproj_sample_kernel,
        grid=(bsz // sb,),
        in_specs=[pl.BlockSpec((sb, t, d), lambda i: (i, 0, 0)),
                  pl.BlockSpec((sb, 6, d), lambda i: (i, 0, 0)),
                  _const_spec(wq.shape), _const_spec(wgx.shape)],
        out_specs=[pl.BlockSpec((sb, t, hq), lambda i: (i, 0, 0)),
                   pl.BlockSpec((sb, 3, t, hq), lambda i: (i, 0, 0, 0))],
        out_shape=[jax.ShapeDtypeStruct((bsz, t, hq), F32), jax.ShapeDtypeStruct((bsz, 3, t, hq), F32)],
        compiler_params=_params("parallel"),
        name="q_proj_sample",
    )(x, mod, wq, wgx)


def _topk_rows(work, n_sel):
    n, q = work.shape
    rid = lax.broadcasted_iota(jnp.int32, (n, q), 0).astype(F32)

    def body(_, carry):
        wk, sel = carry
        m = jnp.max(wk, axis=0, keepdims=True)
        idx = jnp.min(jnp.where(wk == m, rid, 1e9), axis=0, keepdims=True)
        hit = rid == idx
        return jnp.where(hit, -jnp.inf, wk), jnp.where(hit, 1.0, sel)

    _, sel = lax.fori_loop(0, n_sel, body, (work, jnp.zeros((n, q), F32)))
    return sel


def _attn_prompt_kernel(qt_ref, gt_ref, kc_ref, vct_ref, ovt_ref, ks_ref, vst_ref, kw_ref, vwt_ref,
                        o_ref, sel_ref):
    i = pl.program_id(1)
    qb = Q_BLOCK
    hd = HEAD_DIM
    s0 = i * qb
    nc = kc_ref.shape[1]
    ns = ovt_ref.shape[0]
    nq4 = GROUP * qb
    qt = qt_ref[0]
    zeros = jnp.zeros((hd, nq4), BF16)
    qpos1 = s0 + lax.broadcasted_iota(jnp.int32, (1, qb), 1)
    qpos4 = jnp.concatenate([qpos1] * GROUP, axis=1)
    win_lo = jnp.maximum(i - WINDOW // qb, 0)
    n_wb = WINDOW // qb + 1
    n_chunks = (s0 + qb + KV_CHUNK - 1) // KV_CHUNK
    heads_out = []

    for g in range(N_KV):
        pair = slice((g // 2) * 128, (g // 2 + 1) * 128)
        grow = slice(g * hd, (g + 1) * hd)
        qg = jnp.concatenate([qt[(g * GROUP + r) * hd:(g * GROUP + r + 1) * hd, :] for r in range(GROUP)],
                             axis=1)
        qpad = jnp.concatenate([qg, zeros] if g % 2 == 0 else [zeros, qg], axis=0)

        st = jnp.dot(kc_ref[0, :, pair], qpad, preferred_element_type=F32)
        cend = lax.broadcasted_iota(jnp.int32, (nc, nq4), 0) * CMP_STRIDE + (CMP_LEN - 1)
        mc = cend <= qpos4
        st = jnp.where(mc, st, NEG)
        p = jnp.exp(st - jnp.max(st, axis=0, keepdims=True))
        p = jnp.where(mc, p * (1.0 / jnp.sum(p, axis=0, keepdims=True)), 0.0)
        vct = jnp.concatenate([vct_ref[0, t, grow, :] for t in range(vct_ref.shape[1])], axis=1)
        o_c = jnp.dot(vct, p.astype(BF16), preferred_element_type=F32)

        psum = p[:, 0:qb]
        for r in range(1, GROUP):
            psum = psum + p[:, r * qb:(r + 1) * qb]
        ovt = ovt_ref[...]
        imp = sum(jnp.dot(ovt, part, preferred_element_type=F32) for part in _split3(psum))
        jj = lax.broadcasted_iota(jnp.int32, (ns, qb), 0)
        cur = jnp.right_shift(qpos1, SLC_SHIFT)
        forced = (jj == 0) | (jj == cur) | (jj == cur - 1)
        valid = jj * SLC_LEN <= qpos1
        imp = jnp.where(valid, imp + jnp.where(forced, FORCE_BONUS, 0.0), NEG)
        sel_ref[g] = _topk_rows(imp, min(TOP_N, ns))

        def slc_body(c, carry, g=g, pair=pair, grow=grow, qpad=qpad):
            m, l, acc = carry
            k0 = pl.multiple_of(c * KV_CHUNK, KV_CHUNK)
            sc = jnp.dot(ks_ref[0, pl.ds(k0, KV_CHUNK), pair], qpad, preferred_element_type=F32)
            nblk = KV_CHUNK // SLC_LEN
            rows = [jnp.broadcast_to(sel_ref[g, pl.ds(c * nblk + t, 1), :], (SLC_LEN, qb))
                    for t in range(nblk)]
            selm = jnp.concatenate(rows, axis=0)
            kpos = k0 + lax.broadcasted_iota(jnp.int32, (KV_CHUNK, qb), 0)
            bias = jnp.where((selm > 0.5) & (kpos <= qpos1), 0.0, NEG)
            sc = sc + jnp.concatenate([bias] * GROUP, axis=1)
            m_new = jnp.maximum(m, jnp.max(sc, axis=0, keepdims=True))
            a = jnp.exp(m - m_new)
            pc = jnp.exp(sc - m_new)
            l_new = a * l + jnp.sum(pc, axis=0, keepdims=True)
            nvb = KV_CHUNK // qb
            vt = jnp.concatenate([vst_ref[0, c * nvb + t, grow, :] for t in range(nvb)], axis=1)
            acc_new = a * acc + jnp.dot(vt, pc.astype(BF16), preferred_element_type=F32)
            return m_new, l_new, acc_new

        init = (jnp.full((1, nq4), NEG, F32), jnp.zeros((1, nq4), F32), jnp.zeros((hd, nq4), F32))
        _, l_s, acc_s = lax.fori_loop(0, n_chunks, slc_body, init)
        o_s = acc_s * (1.0 / l_s)

        nwk = n_wb * qb
        w0 = pl.multiple_of(win_lo * qb, qb)
        sw = jnp.dot(kw_ref[0, pl.ds(w0, nwk), pair], qpad, preferred_element_type=F32)
        dist = qpos4 - (w0 + lax.broadcasted_iota(jnp.int32, (nwk, nq4), 0))
        sw = jnp.where((dist >= 0) & (dist < WINDOW), sw, NEG)
        pw = jnp.exp(sw - jnp.max(sw, axis=0, keepdims=True))
        lw = jnp.sum(pw, axis=0, keepdims=True)
        vwt = jnp.concatenate([vwt_ref[0, win_lo + t, grow, :] for t in range(n_wb)], axis=1)
        o_w = jnp.dot(vwt, pw.astype(BF16), preferred_element_type=F32) * (1.0 / lw)

        for r in range(GROUP):
            h = g * GROUP + r
            ls = slice(r * qb, (r + 1) * qb)
            heads_out.append(gt_ref[0, 3 * h:3 * h + 1, :] * o_c[:, ls]
                             + gt_ref[0, 3 * h + 1:3 * h + 2, :] * o_s[:, ls]
                             + gt_ref[0, 3 * h + 2:3 * h + 3, :] * o_w[:, ls])

    out_t = jnp.concatenate(heads_out, axis=0)
    o_ref[0] = out_t.T.astype(BF16)


def _attn_prompt(qt, gt, kc, vct, ovt, ks, vst, kw, vwt):
    bsz, hq, t = qt.shape
    ng = gt.shape[1]
    nkv = N_KV * HEAD_DIM
    nc = kc.shape[1]
    qb = Q_BLOCK

    def per_b(shape):
        nd = len(shape)
        return pl.BlockSpec((1,) + tuple(shape[1:]), lambda b, i: (b,) + (0,) * (nd - 1),
                            pipeline_mode=pl.Buffered(1))

    return pl.pallas_call(
        _attn_prompt_kernel,
        grid=(bsz, t // qb),
        in_specs=[pl.BlockSpec((1, hq, qb), lambda b, i: (b, 0, i)),
                  pl.BlockSpec((1, ng, qb), lambda b, i: (b, 0, i)),
                  per_b(kc.shape), per_b(vct.shape), _const_spec(ovt.shape),
                  per_b(ks.shape), per_b(vst.shape), per_b(kw.shape), per_b(vwt.shape)],
        out_specs=pl.BlockSpec((1, qb, hq), lambda b, i: (b, i, 0)),
        out_shape=jax.ShapeDtypeStruct((bsz, t, hq), BF16),
        scratch_shapes=[pltpu.VMEM((N_KV, t // SLC_LEN, qb), F32)],
        compiler_params=_params("parallel", "arbitrary"),
        name="nsa_prompt",
    )(qt, gt, kc, vct, ovt, ks, vst, kw, vwt)


def _topk_lanes(work, n_sel):
    r, n = work.shape
    lid = lax.broadcasted_iota(jnp.int32, (r, n), 1).astype(F32)

    def body(_, carry):
        wk, sel = carry
        m = jnp.max(wk, axis=1, keepdims=True)
        idx = jnp.min(jnp.where(wk == m, lid, 1e9), axis=1, keepdims=True)
        hit = lid == idx
        return jnp.where(hit, -jnp.inf, wk), jnp.where(hit, 1.0, sel)

    _, sel = lax.fori_loop(0, n_sel, body, (work, jnp.zeros((r, n), F32)))
    return sel


def _softmax_lanes(s):
    p = jnp.exp(s - jnp.max(s, axis=1, keepdims=True))
    return p, jnp.sum(p, axis=1, keepdims=True)


def _heads_to_lanes(o, tq):
    hd = HEAD_DIM
    nkv = N_KV * hd
    lane = lax.broadcasted_iota(jnp.int32, (tq, nkv), 1)
    pieces = []
    for h in range(N_HEADS):
        g = h // GROUP
        blk = o[h * tq:(h + 1) * tq, :]
        blk = jnp.where((lane >= g * hd) & (lane < (g + 1) * hd), blk, 0.0)
        dst = (h * hd) % nkv
        shift = (dst - g * hd) % nkv
        pieces.append(pltpu.roll(blk, shift, axis=1) if shift else blk)
    cols = []
    per = nkv // hd
    for c in range(N_HEADS // per):
        slab = pieces[c * per]
        for k in range(1, per):
            slab = slab + pieces[c * per + k]
        cols.append(slab)
    return jnp.concatenate(cols, axis=1)


def _attn_sample_kernel(pt_ref, *refs, past_len, n_ns):
    del pt_ref
    npg = past_len // PAGE
    page_refs = refs[:npg]
    (q_ref, g_ref, kc_ref, vc_ref, kvn_ref, win_ref, ov_ref, ex_ref,
     o_ref, kbuf, vbuf, kwbuf, vwbuf) = refs[npg:]
    hd = HEAD_DIM
    nkv = N_KV * hd
    tq = q_ref.shape[1]
    nrow = N_HEADS * tq
    nwin = win_ref.shape[1]
    nks = kbuf.shape[0]
    nkw = kwbuf.shape[0]

    for k in range(npg):
        kbuf[k * PAGE:(k + 1) * PAGE, :] = page_refs[k][:, 0:nkv].astype(BF16)
        vbuf[k * PAGE:(k + 1) * PAGE, :] = page_refs[k][:, nkv:2 * nkv].astype(BF16)
    kvn = kvn_ref[0]
    ztail = jnp.zeros((PAGE - tq, nkv), F32)
    kbuf[past_len:past_len + PAGE, :] = jnp.concatenate([kvn[:, 2 * nkv:3 * nkv], ztail], axis=0).astype(BF16)
    vbuf[past_len:past_len + PAGE, :] = jnp.concatenate([kvn[:, 3 * nkv:4 * nkv], ztail], axis=0).astype(BF16)
    kwbuf[0:nwin, :] = win_ref[0, :, 0:nkv].astype(BF16)
    vwbuf[0:nwin, :] = win_ref[0, :, nkv:2 * nkv].astype(BF16)
    kwbuf[nwin:nwin + PAGE, :] = jnp.concatenate([kvn[:, 4 * nkv:5 * nkv], ztail], axis=0).astype(BF16)
    vwbuf[nwin:nwin + PAGE, :] = jnp.concatenate([kvn[:, 5 * nkv:6 * nkv], ztail], axis=0).astype(BF16)

    q = q_ref[0]
    lane = lax.broadcasted_iota(jnp.int32, (tq, nkv), 1)
    qrows = []
    for h in range(N_HEADS):
        g, r = divmod(h, GROUP)
        slab = q[:, g * nkv:(g + 1) * nkv]
        shift = ((g - r) * hd) % nkv
        moved = pltpu.roll(slab, shift, axis=1) if shift else slab
        qrows.append(jnp.where((lane >= g * hd) & (lane < (g + 1) * hd), moved, 0.0))
    qbd = jnp.concatenate(qrows, axis=0).astype(BF16)

    tpos = jnp.bitwise_and(lax.broadcasted_iota(jnp.int32, (nrow, 1), 0), tq - 1)
    qpos = past_len + tpos

    nc = kc_ref.shape[1]
    s_c = lax.dot_general(qbd, kc_ref[0], _NT, preferred_element_type=F32)
    cend = lax.broadcasted_iota(jnp.int32, (nrow, nc), 1) * CMP_STRIDE + (CMP_LEN - 1)
    mc = cend <= qpos
    p_c, l_c = _softmax_lanes(jnp.where(mc, s_c, NEG))
    p_c = jnp.where(mc, p_c * (1.0 / l_c), 0.0)
    o_c = jnp.dot(p_c.astype(BF16), vc_ref[0], preferred_element_type=F32)

    p4 = p_c.reshape(N_KV, GROUP, tq, nc)
    psum = p4[:, 0]
    for r in range(1, GROUP):
        psum = psum + p4[:, r]
    psum = psum.reshape(N_KV * tq, nc)
    ov = ov_ref[...]
    imp = sum(jnp.dot(part, ov, preferred_element_type=F32) for part in _split3(psum))
    nsl = imp.shape[1]
    jj = lax.broadcasted_iota(jnp.int32, (N_KV * tq, nsl), 1)
    qp2 = past_len + jnp.bitwise_and(lax.broadcasted_iota(jnp.int32, (N_KV * tq, 1), 0), tq - 1)
    cur = jnp.right_shift(qp2, SLC_SHIFT)
    forced = (jj == 0) | (jj == cur) | (jj == cur - 1)
    valid = (jj * SLC_LEN <= qp2) & (jj < n_ns)
    imp = jnp.where(valid, imp + jnp.where(forced, FORCE_BONUS, 0.0), NEG)
    sel = _topk_lanes(imp, min(TOP_N, n_ns))
    selk = jnp.dot(sel.astype(BF16), ex_ref[...], preferred_element_type=F32)
    selk = jnp.broadcast_to(selk.reshape(N_KV, 1, tq, nks), (N_KV, GROUP, tq, nks)).reshape(nrow, nks)

    s_s = lax.dot_general(qbd, kbuf[...], _NT, preferred_element_type=F32)
    kpos = lax.broadcasted_iota(jnp.int32, (nrow, nks), 1)
    s_s = jnp.where((selk > 0.5) & (kpos <= qpos), s_s, NEG)
    p_s, l_s = _softmax_lanes(s_s)
    o_s = jnp.dot(p_s.astype(BF16), vbuf[...], preferred_element_type=F32) * (1.0 / l_s)

    s_w = lax.dot_general(qbd, kwbuf[...], _NT, preferred_element_type=F32)
    jw = lax.broadcasted_iota(jnp.int32, (nrow, nkw), 1)
    kwpos = past_len + tq - (nwin + tq) + jw
    dist = qpos - kwpos
    mw = (dist >= 0) & (dist < WINDOW) & (kwpos >= 0) & (jw < nwin + tq)
    p_w, l_w = _softmax_lanes(jnp.where(mw, s_w, NEG))
    o_w = jnp.dot(p_w.astype(BF16), vwbuf[...], preferred_element_type=F32) * (1.0 / l_w)

    out = (g_ref[0, 0] * _heads_to_lanes(o_c, tq) + g_ref[0, 1] * _heads_to_lanes(o_s, tq)
           + g_ref[0, 2] * _heads_to_lanes(o_w, tq))
    o_ref[0] = out


def _attn_sample(pages, page_table, q, gx, kc, vc, kvn, win, ov, ex, *, past_len):
    bsz, tq, hq = q.shape
    npg = past_len // PAGE
    nkv = N_KV * HEAD_DIM
    nwin = win.shape[1]
    n_ns = -(-(past_len + tq) // SLC_LEN)
    nks = past_len + PAGE
    nkw = nwin + PAGE
    kern = functools.partial(_attn_sample_kernel, past_len=past_len, n_ns=n_ns)

    def page_spec(k):
        return pl.BlockSpec((None, PAGE, 2 * nkv), lambda b, pt: (pt[b, k], 0, 1))

    def cs(shape):
        nd = len(shape)
        return pl.BlockSpec(shape, lambda b, pt: (0,) * nd, pipeline_mode=pl.Buffered(1))

    def per_b(shape):
        nd = len(shape)
        return pl.BlockSpec((1,) + tuple(shape[1:]), lambda b, pt: (b,) + (0,) * (nd - 1))

    in_specs = [page_spec(k) for k in range(npg)]
    in_specs += [per_b(q.shape), per_b(gx.shape), per_b(kc.shape), per_b(vc.shape), per_b(kvn.shape),
                 per_b(win.shape), cs(ov.shape), cs(ex.shape)]
    grid_spec = pltpu.PrefetchScalarGridSpec(
        num_scalar_prefetch=1,
        grid=(bsz,),
        in_specs=in_specs,
        out_specs=pl.BlockSpec((1, tq, hq), lambda b, pt: (b, 0, 0)),
        scratch_shapes=[pltpu.VMEM((nks, nkv), BF16), pltpu.VMEM((nks, nkv), BF16),
                        pltpu.VMEM((nkw, nkv), BF16), pltpu.VMEM((nkw, nkv), BF16)],
    )
    return pl.pallas_call(
        kern,
        grid_spec=grid_spec,
        out_shape=jax.ShapeDtypeStruct((bsz, tq, hq), F32),
        compiler_params=_params("arbitrary"),
        name="nsa_sample",
    )(page_table, *([pages] * npg), q, gx, kc, vc, kvn, win, ov, ex)


def _oproj_kernel(o_ref, x_ref, mod_ref, wo_ref, g_ref, b_ref, out_ref, *, alpha):
    sb, tm, d = x_ref.shape
    ob = o_ref[...].reshape(sb * tm, -1).astype(BF16)
    mix = jnp.dot(ob, wo_ref[...], preferred_element_type=F32).reshape(sb, tm, d)
    gate = mod_ref[:, 2:3, :]
    y = alpha * x_ref[...] + (1.0 + gate) * mix
    out_ref[...] = _layer_norm(y, g_ref[...], b_ref[...])


def _oproj(o, x, mod, wo, g, b, *, sb, tm, alpha):
    bsz, t, d = x.shape
    hq = o.shape[-1]
    kern = functools.partial(_oproj_kernel, alpha=alpha)
    return pl.pallas_call(
        kern,
        grid=(bsz // sb, t // tm),
        in_specs=[pl.BlockSpec((sb, tm, hq), lambda i, j: (i, j, 0)),
                  pl.BlockSpec((sb, tm, d), lambda i, j: (i, j, 0)),
                  pl.BlockSpec((sb, 6, d), lambda i, j: (i, 0, 0)),
                  _const_spec(wo.shape), _const_spec(g.shape), _const_spec(b.shape)],
        out_specs=pl.BlockSpec((sb, tm, d), lambda i, j: (i, j, 0)),
        out_shape=jax.ShapeDtypeStruct((bsz, t, d), F32),
        compiler_params=_params("parallel", "parallel"),
        name="o_proj",
    )(o, x, mod, wo, g, b)


def _overlap(n_c, n_s):
    ci = np.arange(n_c)[:, None] * CMP_STRIDE
    sj = np.arange(n_s)[None, :] * SLC_LEN
    return ((ci <= sj + SLC_LEN - 1) & (ci + CMP_LEN - 1 >= sj)).astype(np.float32)


def kernel(x_prompt, x_sample, cache_kv, state_kv_win, state_pool, state_conv, page_table, c_prompt, c_sample, ada_w, ada_b, ln_g, ln_b, pool_w, pool_ls, ffn_w_up, ffn_b_up, ffn_w_conv, ffn_b_conv, ffn_w_down, w_kv, cmp_pe, cmp_w1, cmp_b1, cmp_w2, cmp_b2, nsa_w_qg, nsa_w_o):
    bp, t, d = x_prompt.shape
    bs, ts, _ = x_sample.shape
    depth = ada_w.shape[0]
    n_a = pool_w.shape[0]
    f2 = ffn_w_up.shape[-1]
    f = f2 // 2
    past_len = page_table.shape[1] * PAGE
    nkv = N_KV * HEAD_DIM
    hq = N_HEADS * HEAD_DIM
    alpha = float((2 * depth) ** 0.25)
    assert t % (16 * PAGE) == 0 and t >= WINDOW + Q_BLOCK
    assert ts == 8 and past_len == 16 * PAGE and state_kv_win.shape[1] == WINDOW
    assert f % 1408 == 0 or f % 256 == 0

    tm_p = 512
    sb_pool = min(bs, 64)
    sb_ffn = min(bs, 16)
    cf = 1408 if f % 1408 == 0 else 256

    ada = _ada(jnp.concatenate([c_prompt, c_sample], axis=0), ada_w, ada_b)
    ada = ada.reshape(depth, bp + bs, 6, d)
    mod_p = [ada[l, :bp] for l in range(depth)]
    mod_s = [ada[l, bp:] for l in range(depth)]

    pool_wb = pool_w.astype(BF16)
    wup_b = ffn_w_up.astype(BF16)
    wdn_b = ffn_w_down.astype(BF16)
    w_kv_b = w_kv.astype(BF16)
    wvt_b = jnp.concatenate([w_kv[:, 3 * nkv:4 * nkv], w_kv[:, 5 * nkv:6 * nkv]], axis=1).T.astype(BF16)
    wq_b = nsa_w_qg[:, :, :hq].astype(BF16)
    wqt_b = jnp.swapaxes(nsa_w_qg[:, :, :hq], 1, 2).astype(BF16)
    wgt_b = jnp.swapaxes(nsa_w_qg[:, :, hq:], 1, 2).astype(BF16)
    wg = nsa_w_qg[:, :, hq:].reshape(-1, d, N_HEADS, 3)
    wgx_b = jnp.repeat(jnp.moveaxis(wg, 3, 1), HEAD_DIM, axis=3).astype(BF16)
    wo_b = nsa_w_o.astype(BF16)

    def vec(a):
        return a.reshape(1, -1)

    pool0 = jnp.zeros((bp, POOL_HALO, d), F32)
    conv0 = jnp.zeros((bp, CONV_HALO, f2), F32)
    pool_prev_s = jnp.pad(state_pool, ((0, 0), (0, 0), (POOL_HALO - state_pool.shape[2], 0), (0, 0)))
    conv_prev_s = jnp.pad(state_conv, ((0, 0), (0, 0), (CONV_HALO - state_conv.shape[2], 0), (0, 0)))

    xp, xs = x_prompt, x_sample
    pool_p, pool_s, conv_p, conv_s = [], [], [], []
    ctx_p = ctx_s = None
    kv_p = kv_s = None

    for l in range(depth):
        g1, b1, g2, b2 = vec(ln_g[l, 0]), vec(ln_b[l, 0]), vec(ln_g[l, 1]), vec(ln_b[l, 1])
        if l < n_a:
            xp, st = _pool_layer(xp, mod_p[l], pool0, pool_wb[l], vec(pool_ls[l]), g1, b1,
                                 sb=1, tm=tm_p, start_pos=0, alpha=alpha)
            pool_p.append(st[:, 1:])
            xs, st = _pool_layer(xs, mod_s[l], pool_prev_s[l], pool_wb[l], vec(pool_ls[l]), g1, b1,
                                 sb=sb_pool, tm=ts, start_pos=past_len, alpha=alpha)
            pool_s.append(st[:, 1:])
        else:
            jn = l - n_a
            qt, gt = _qproj_prompt(xp, mod_p[l], wqt_b[jn], wgt_b[jn], tm=tm_p)
            op = _attn_prompt(qt, gt, *ctx_p)
            xp = _oproj(op, xp, mod_p[l], wo_b[jn], g1, b1, sb=1, tm=tm_p, alpha=alpha)
            qs, gs = _qproj_sample(xs, mod_s[l], wq_b[jn], wgx_b[jn], sb=sb_pool)
            os_ = _attn_sample(ctx_s[0], page_table, qs, gs, *ctx_s[1:], past_len=past_len)
            xs = _oproj(os_, xs, mod_s[l], wo_b[jn], g1, b1, sb=sb_pool, tm=ts, alpha=alpha)

        xp, st = _ffn_layer(xp, mod_p[l], conv0, wup_b[l], vec(ffn_b_up[l]), ffn_w_conv[l], vec(ffn_b_conv[l]),
                            wdn_b[l], g2, b2, sb=1, tm=tm_p, cf=cf, alpha=alpha)
        conv_p.append(st[:, CONV_HALO - (CONV_W - 1):])
        xs, st = _ffn_layer(xs, mod_s[l], conv_prev_s[l], wup_b[l], vec(ffn_b_up[l]), ffn_w_conv[l],
                            vec(ffn_b_conv[l]), wdn_b[l], g2, b2, sb=sb_ffn, tm=ts, cf=cf, alpha=alpha)
        conv_s.append(st[:, CONV_HALO - (CONV_W - 1):])

        if l == n_a - 1:
            kvp, kvw, ks, kw, vst, vwt = _kvproj_prompt(xp, w_kv_b, wvt_b, tm=tm_p)
            kv_p = (kvp, kvw)
            tiles = t // (16 * PAGE)
            ppb = t // PAGE
            base = (np.arange(bp)[:, None] * ppb + np.arange(tiles)[None, :] * 16).reshape(-1, 1)
            ids = base + np.arange(17)[None, :]
            last = (np.arange(bp)[:, None] * ppb + ppb - 1).repeat(tiles, axis=1).reshape(-1)
            ids[:, 16] = np.minimum(ids[:, 16], last)
            kc, _, vct = _compress(kvp.reshape(bp * ppb, PAGE, 4 * nkv), jnp.asarray(ids, jnp.int32),
                                   cmp_pe, cmp_w1, cmp_b1, cmp_w2, cmp_b2)
            n_c = tiles * 128
            ovt = jnp.asarray(_overlap(n_c, t // SLC_LEN).T, BF16)
            ctx_p = (kc.reshape(bp, n_c, nkv), vct.reshape(bp, tiles, nkv, 128), ovt, ks, vst, kw, vwt)

            kvs = _kvproj_sample(xs, w_kv_b, sb=sb_pool)
            kv_s = kvs
            pages = cache_kv.reshape(cache_kv.shape[0], PAGE, 4 * nkv)
            ids_s = jnp.concatenate([page_table, page_table[:, :1]], axis=1).astype(jnp.int32)
            kc_s, vc_s, _ = _compress(pages, ids_s, cmp_pe, cmp_w1, cmp_b1, cmp_w2, cmp_b2)
            n_ns = -(-(past_len + ts) // SLC_LEN)
            ov_s = np.zeros((128, 128), np.float32)
            n_cs = (past_len + ts - CMP_LEN) // CMP_STRIDE + 1
            ov_s[:n_cs, :n_ns] = _overlap(n_cs, n_ns)
            nks = past_len + PAGE
            ex = (np.arange(nks)[None, :] // SLC_LEN == np.arange(128)[:, None]).astype(np.float32)
            win = state_kv_win.reshape(bs, state_kv_win.shape[1], 2 * nkv)
            ctx_s = (pages, kc_s, vc_s, kvs, win, jnp.asarray(ov_s, BF16), jnp.asarray(ex, BF16))

    kvp, kvw = kv_p
    wlen = min(WINDOW, t)
    wbuf = state_kv_win.shape[1]
    win_s = jnp.concatenate([state_kv_win, kv_s[:, :, 4 * nkv:].reshape(bs, ts, 2, N_KV, HEAD_DIM)], axis=1)[:, -wbuf:]
    return (xp, xs,
            kvp.reshape(bp, t, 4, N_KV, HEAD_DIM),
            kv_s[:, :, :4 * nkv].reshape(bs, ts, 4, N_KV, HEAD_DIM),
            kvw[:, t - wlen:].reshape(bp, wlen, 2, N_KV, HEAD_DIM),
            win_s,
            jnp.stack(pool_p), jnp.stack(pool_s), jnp.stack(conv_p), jnp.stack(conv_s))
```

```python
import functools

import numpy as np
import jax
import jax.numpy as jnp
from jax import lax
from jax.experimental import pallas as pl
from jax.experimental.pallas import tpu as pltpu

F32 = jnp.float32
BF16 = jnp.bfloat16

POOL_WINDOWS = (2, 4, 8, 16)
POOL_HALO = 16
N_HEADS = 16
HEAD_DIM = 64
N_KV = 4
GROUP = N_HEADS // N_KV
CMP_LEN = 32
CMP_STRIDE = 16
SLC_LEN = 64
SLC_SHIFT = 6
TOP_N = 16
WINDOW = 512
Q_BLOCK = 128
CONV_W = 3
CONV_HALO = 8
PAGE = 128
LN_EPS = 1e-5
NEG = -1e30
FORCE_BONUS = 1e4
KV_CHUNK = 512
Q_SCALE = HEAD_DIM ** -0.5 * 1.4426950408889634
VMEM_LIMIT = 56 * 1024 * 1024

_NT = (((1,), (1,)), ((), ()))


def _params(*sem):
    return pltpu.CompilerParams(dimension_semantics=sem, vmem_limit_bytes=VMEM_LIMIT)


def _const_spec(shape):
    nd = len(shape)
    return pl.BlockSpec(shape, lambda *_: (0,) * nd, pipeline_mode=pl.Buffered(1))


def _layer_norm(y, g, b):
    mu = jnp.mean(y, axis=-1, keepdims=True)
    d = y - mu
    var = jnp.mean(d * d, axis=-1, keepdims=True)
    return d * lax.rsqrt(var + LN_EPS) * g + b


def _split3(x):
    hi = x.astype(BF16)
    r1 = x - hi.astype(F32)
    mid = r1.astype(BF16)
    lo = (r1 - mid.astype(F32)).astype(BF16)
    return hi, mid, lo


def _ada_kernel(c_ref, w_ref, b_ref, o_ref):
    c = c_ref[...]
    s = c * jax.nn.sigmoid(c)
    o_ref[0] = jnp.dot(s.astype(BF16), w_ref[0].astype(BF16), preferred_element_type=F32) + b_ref[0]


def _ada(c_all, ada_w, ada_b):
    depth, d, n = ada_w.shape
    r = c_all.shape[0]
    tn = n // 4
    return pl.pallas_call(
        _ada_kernel,
        grid=(depth, n // tn),
        in_specs=[pl.BlockSpec((r, d), lambda l, j: (0, 0)),
                  pl.BlockSpec((1, d, tn), lambda l, j: (l, 0, j)),
                  pl.BlockSpec((1, 1, tn), lambda l, j: (l, 0, j))],
        out_specs=pl.BlockSpec((1, r, tn), lambda l, j: (l, 0, j)),
        out_shape=jax.ShapeDtypeStruct((depth, r, n), F32),
        compiler_params=_params("parallel", "parallel"),
        name="ada_params",
    )(c_all, ada_w, ada_b.reshape(depth, 1, n))


def _pool_kernel(x_ref, mod_ref, prev_ref, pw_ref, ls_ref, g_ref, b_ref, o_ref, ps_ref, ext_ref,
                 *, start_pos, nj, alpha):
    j = pl.program_id(1)
    sb, tm, d = x_ref.shape
    pg = d // len(POOL_WINDOWS)
    h = POOL_HALO
    x = x_ref[...]
    shift = mod_ref[:, 0:1, :]
    scale = mod_ref[:, 1:2, :]
    gate = mod_ref[:, 2:3, :]
    u = x * (1.0 + scale) + shift

    @pl.when(j == 0)
    def _():
        ext_ref[:, 0:h, :] = prev_ref[...]

    ext_ref[:, h:h + tm, :] = u
    pos = start_pos + j * tm + lax.broadcasted_iota(jnp.int32, (1, tm, pg), 1)
    outs = []
    for gi, w in enumerate(POOL_WINDOWS):
        c0 = gi * pg
        acc = ext_ref[:, h:h + tm, c0:c0 + pg]
        for k in range(1, w):
            acc = acc + ext_ref[:, h - k:h - k + tm, c0:c0 + pg]
        cnt = jnp.minimum(pos + 1, w).astype(F32)
        pooled = acc / cnt - u[:, :, c0:c0 + pg]
        outs.append(jnp.dot(pooled.reshape(sb * tm, pg).astype(BF16), pw_ref[gi],
                            preferred_element_type=F32))
    mixed = jnp.concatenate(outs, axis=-1) * ls_ref[...]
    y = alpha * x + (1.0 + gate) * mixed.reshape(sb, tm, d)
    o_ref[...] = _layer_norm(y, g_ref[...], b_ref[...])

    if nj > 1:
        @pl.when(j < nj - 1)
        def _():
            ext_ref[:, 0:h, :] = ext_ref[:, tm:tm + h, :]

    @pl.when(j == nj - 1)
    def _():
        ps_ref[...] = ext_ref[:, tm:tm + h, :]


def _pool_layer(x, mod, prev, pw, ls, g, b, *, sb, tm, start_pos, alpha):
    bsz, t, d = x.shape
    nj = t // tm
    h = POOL_HALO
    kern = functools.partial(_pool_kernel, start_pos=start_pos, nj=nj, alpha=alpha)
    return pl.pallas_call(
        kern,
        grid=(bsz // sb, nj),
        in_specs=[pl.BlockSpec((sb, tm, d), lambda i, j: (i, j, 0)),
                  pl.BlockSpec((sb, 6, d), lambda i, j: (i, 0, 0)),
                  pl.BlockSpec((sb, h, d), lambda i, j: (i, 0, 0)),
                  _const_spec(pw.shape), _const_spec(ls.shape), _const_spec(g.shape), _const_spec(b.shape)],
        out_specs=[pl.BlockSpec((sb, tm, d), lambda i, j: (i, j, 0)),
                   pl.BlockSpec((sb, h, d), lambda i, j: (i, 0, 0))],
        out_shape=[jax.ShapeDtypeStruct((bsz, t, d), F32), jax.ShapeDtypeStruct((bsz, h, d), F32)],
        scratch_shapes=[pltpu.VMEM((sb, h + tm, d), F32)],
        compiler_params=_params("parallel", "arbitrary"),
        name="pool_layer",
    )(x, mod, prev, pw, ls, g, b)


def _ffn_kernel(x_ref, mod_ref, cprev_ref, wup_ref, bup_ref, wc_ref, bc_ref, wdn_ref, g_ref, b_ref,
                o_ref, cs_ref, hext_ref, *, cf, alpha):
    j = pl.program_id(1)
    sb, tm, d = x_ref.shape
    f = wdn_ref.shape[0]
    hh = CONV_HALO
    x = x_ref[...]
    shift = mod_ref[:, 3:4, :]
    scale = mod_ref[:, 4:5, :]
    gate = mod_ref[:, 5:6, :]
    ub = (x * (1.0 + scale) + shift).reshape(sb * tm, d).astype(BF16)

    @pl.when(j == 0)
    def _():
        cs_ref[...] = cprev_ref[...]

    acc = jnp.zeros((sb * tm, d), F32)
    for c in range(f // cf):
        parts = []
        for p in range(2):
            c0 = p * f + c * cf
            hp = jnp.dot(ub, wup_ref[:, c0:c0 + cf], preferred_element_type=F32) + bup_ref[:, c0:c0 + cf]
            hext_ref[p, :, 0:hh, :] = cs_ref[:, :, c0:c0 + cf]
            hext_ref[p, :, hh:hh + tm, :] = hp.reshape(sb, tm, cf)
            hc = (bc_ref[:, c0:c0 + cf]
                  + hext_ref[p, :, hh - 2:hh - 2 + tm, :] * wc_ref[0:1, c0:c0 + cf]
                  + hext_ref[p, :, hh - 1:hh - 1 + tm, :] * wc_ref[1:2, c0:c0 + cf]
                  + hext_ref[p, :, hh:hh + tm, :] * wc_ref[2:3, c0:c0 + cf])
            cs_ref[:, :, c0:c0 + cf] = hext_ref[p, :, tm:tm + hh, :]
            parts.append(hc)
        a, v = parts
        gated = (a * jax.nn.sigmoid(a)) * v
        acc = acc + jnp.dot(gated.reshape(sb * tm, cf).astype(BF16), wdn_ref[c * cf:(c + 1) * cf, :],
                            preferred_element_type=F32)
    y = alpha * x + (1.0 + gate) * acc.reshape(sb, tm, d)
    o_ref[...] = _layer_norm(y, g_ref[...], b_ref[...])


def _ffn_layer(x, mod, cprev, wup, bup, wc, bc, wdn, g, b, *, sb, tm, cf, alpha):
    bsz, t, d = x.shape
    f2 = wup.shape[1]
    hh = CONV_HALO
    kern = functools.partial(_ffn_kernel, cf=cf, alpha=alpha)
    return pl.pallas_call(
        kern,
        grid=(bsz // sb, t // tm),
        in_specs=[pl.BlockSpec((sb, tm, d), lambda i, j: (i, j, 0)),
                  pl.BlockSpec((sb, 6, d), lambda i, j: (i, 0, 0)),
                  pl.BlockSpec((sb, hh, f2), lambda i, j: (i, 0, 0)),
                  _const_spec(wup.shape), _const_spec(bup.shape), _const_spec(wc.shape),
                  _const_spec(bc.shape), _const_spec(wdn.shape), _const_spec(g.shape), _const_spec(b.shape)],
        out_specs=[pl.BlockSpec((sb, tm, d), lambda i, j: (i, j, 0)),
                   pl.BlockSpec((sb, hh, f2), lambda i, j: (i, 0, 0))],
        out_shape=[jax.ShapeDtypeStruct((bsz, t, d), F32), jax.ShapeDtypeStruct((bsz, hh, f2), F32)],
        scratch_shapes=[pltpu.VMEM((2, sb, hh + tm, cf), F32)],
        compiler_params=_params("parallel", "arbitrary"),
        name="conv_ffn_layer",
    )(x, mod, cprev, wup, bup, wc, bc, wdn, g, b)


def _kvproj_prompt_kernel(x_ref, w_ref, wt_ref, kvpt_ref, kvwt_ref, cmp_ref, ks_ref, kw_ref, vst_ref, vwt_ref):
    tm = x_ref.shape[1]
    nkv = N_KV * HEAD_DIM
    xb = x_ref[0].astype(BF16)
    kv = jnp.dot(xb, w_ref[...], preferred_element_type=F32)
    cmp_ref[0] = kv[:, :2 * nkv].astype(BF16)
    ks_ref[0] = kv[:, 2 * nkv:3 * nkv].astype(BF16)
    kw_ref[0] = kv[:, 4 * nkv:5 * nkv].astype(BF16)
    kvt = lax.dot_general(wt_ref[...], xb, _NT, preferred_element_type=F32)
    kvpt_ref[0] = kvt[:4 * nkv]
    kvwt_ref[0] = kvt[4 * nkv:]
    for c in range(tm // Q_BLOCK):
        cols = slice(c * Q_BLOCK, (c + 1) * Q_BLOCK)
        vst_ref[0, c] = kvt[3 * nkv:4 * nkv, cols].astype(BF16)
        vwt_ref[0, c] = kvt[5 * nkv:6 * nkv, cols].astype(BF16)


def _kvproj_prompt(x, w_kv_b, w_kvt_b, *, tm):
    bsz, t, d = x.shape
    nkv = N_KV * HEAD_DIM
    nb = tm // Q_BLOCK
    return pl.pallas_call(
        _kvproj_prompt_kernel,
        grid=(bsz, t // tm),
        in_specs=[pl.BlockSpec((1, tm, d), lambda i, j: (i, j, 0)),
                  _const_spec(w_kv_b.shape), _const_spec(w_kvt_b.shape)],
        out_specs=[pl.BlockSpec((1, 4 * nkv, tm), lambda i, j: (i, 0, j)),
                   pl.BlockSpec((1, 2 * nkv, tm), lambda i, j: (i, 0, j)),
                   pl.BlockSpec((1, tm, 2 * nkv), lambda i, j: (i, j, 0)),
                   pl.BlockSpec((1, tm, nkv), lambda i, j: (i, j, 0)),
                   pl.BlockSpec((1, tm, nkv), lambda i, j: (i, j, 0)),
                   pl.BlockSpec((1, nb, nkv, Q_BLOCK), lambda i, j: (i, j, 0, 0)),
                   pl.BlockSpec((1, nb, nkv, Q_BLOCK), lambda i, j: (i, j, 0, 0))],
        out_shape=[jax.ShapeDtypeStruct((bsz, 4 * nkv, t), F32),
                   jax.ShapeDtypeStruct((bsz, 2 * nkv, t), F32),
                   jax.ShapeDtypeStruct((bsz, t, 2 * nkv), BF16),
                   jax.ShapeDtypeStruct((bsz, t, nkv), BF16),
                   jax.ShapeDtypeStruct((bsz, t, nkv), BF16),
                   jax.ShapeDtypeStruct((bsz, t // Q_BLOCK, nkv, Q_BLOCK), BF16),
                   jax.ShapeDtypeStruct((bsz, t // Q_BLOCK, nkv, Q_BLOCK), BF16)],
        compiler_params=_params("parallel", "parallel"),
        name="kv_proj_prompt",
    )(x, w_kv_b, w_kvt_b)


def _kvproj_sample_kernel(x_ref, w_ref, kv_ref):
    sb, tm, d = x_ref.shape
    xb = x_ref[...].reshape(sb * tm, d).astype(BF16)
    kv_ref[...] = jnp.dot(xb, w_ref[...], preferred_element_type=F32).reshape(sb, tm, -1)


def _kvproj_sample(x, w_kv_b, *, sb):
    bsz, t, d = x.shape
    n = w_kv_b.shape[1]
    return pl.pallas_call(
        _kvproj_sample_kernel,
        grid=(bsz // sb,),
        in_specs=[pl.BlockSpec((sb, t, d), lambda i: (i, 0, 0)), _const_spec(w_kv_b.shape)],
        out_specs=pl.BlockSpec((sb, t, n), lambda i: (i, 0, 0)),
        out_shape=jax.ShapeDtypeStruct((bsz, t, n), F32),
        compiler_params=_params("parallel"),
        name="kv_proj_sample",
    )(x, w_kv_b)


def _compress_kernel(pt_ref, *refs, transposed):
    del pt_ref
    npg = 16
    page_refs = refs[:npg]
    rest = refs[npg:]
    if not transposed:
        look_ref, rest = rest[0], rest[1:]
    (pet_ref, peb_ref, w1t_ref, w1b_ref, b1_ref, w2_ref, w2t_ref, b2_ref, b2t_ref,
     kc_ref, vc_ref, vct_ref, slab_ref, xs_ref) = rest
    rows = npg * PAGE
    nch = rows // CMP_STRIDE
    hd = HEAD_DIM
    if transposed:
        eye = (lax.broadcasted_iota(jnp.int32, (PAGE, PAGE), 0)
               == lax.broadcasted_iota(jnp.int32, (PAGE, PAGE), 1)).astype(F32).astype(BF16)
    for k in range(npg):
        if transposed:
            pg = lax.dot_general(eye, page_refs[k][...].astype(BF16), _NT, preferred_element_type=F32)
        else:
            pg = page_refs[k][...].astype(F32)
        for sl in range(4):
            slab_ref[sl, k * PAGE:(k + 1) * PAGE, :] = pg[:, sl * 128:(sl + 1) * 128]
    for sl in range(4):
        if transposed:
            slab_ref[sl, rows:rows + CMP_STRIDE, :] = jnp.zeros((CMP_STRIDE, 128), F32)
        else:
            slab_ref[sl, rows:rows + CMP_STRIDE, :] = look_ref[:, sl * 128:(sl + 1) * 128].astype(F32)

    lane = lax.broadcasted_iota(jnp.int32, (nch, 128), 1)
    low = lane < hd
    for sl in range(4):
        s = sl // 2
        ga = 2 * (sl % 2)
        for half in range(2):
            base = CMP_STRIDE * half
            for m in range(CMP_STRIDE // 2):
                ve = slab_ref[sl, pl.ds(base + 2 * m, nch, stride=CMP_STRIDE), :]
                vo = slab_ref[sl, pl.ds(base + 2 * m + 1, nch, stride=CMP_STRIDE), :]
                ca = jnp.where(low, ve, pltpu.roll(vo, hd, axis=1))
                cb = jnp.where(low, pltpu.roll(ve, hd, axis=1), vo)
                xs_ref[half, s, ga * nch:(ga + 1) * nch, m * 128:(m + 1) * 128] = ca.astype(BF16)
                xs_ref[half, s, (ga + 1) * nch:(ga + 2) * nch, m * 128:(m + 1) * 128] = cb.astype(BF16)
    nx = N_KV * nch
    for s in range(2):
        for half, pe_ref in enumerate((pet_ref, peb_ref)):
            pe = pe_ref[s]
            hi = pe.astype(BF16).astype(F32)
            tail = jnp.concatenate([hi, pe - hi, jnp.zeros((14, pe.shape[1]), F32)], axis=0)
            xs_ref[half, s, nx:nx + 16, :] = tail.astype(BF16)

    for s in range(2):
        pre_all = (jnp.dot(xs_ref[0, s], w1t_ref[s], preferred_element_type=F32)
                   + jnp.dot(xs_ref[1, s], w1b_ref[s], preferred_element_type=F32))
        pre = pre_all[:nx] + (pre_all[nx:nx + 1] + pre_all[nx + 1:nx + 2] + b1_ref[s])
        hid = jax.nn.gelu(pre).astype(BF16)
        out = jnp.dot(hid, w2_ref[s], preferred_element_type=F32) + b2_ref[s]
        nat = jnp.concatenate([out[g * nch:(g + 1) * nch, :] for g in range(N_KV)], axis=1)
        if s == 0:
            kc_ref[0] = nat.astype(BF16)
        else:
            vc_ref[0] = nat.astype(BF16)
            outs_t = [lax.dot_general(w2t_ref[s], hid[g * nch:(g + 1) * nch, :], _NT,
                                      preferred_element_type=F32) + b2t_ref[s] for g in range(N_KV)]
            vct_ref[0] = jnp.concatenate(outs_t, axis=0).astype(BF16)


def _compress(pages, ptab, pe, w1, b1, w2, b2, *, transposed):
    nt = ptab.shape[0]
    npg = 16
    nkv = N_KV * HEAD_DIM
    half = CMP_STRIDE * HEAD_DIM
    hidden = w1.shape[-1]
    pet = pe[:, :CMP_STRIDE].reshape(2, 1, half)
    peb = pe[:, CMP_STRIDE:].reshape(2, 1, half)
    w1t = w1[:, :half].astype(BF16)
    w1b = w1[:, half:].astype(BF16)
    w2b = w2.astype(BF16)
    w2t = jnp.swapaxes(w2, 1, 2).astype(BF16)
    b1r = b1.reshape(2, 1, hidden)
    b2r = b2.reshape(2, 1, HEAD_DIM)
    b2t = b2.reshape(2, HEAD_DIM, 1)
    nch = npg * PAGE // CMP_STRIDE

    def page_spec(k):
        if transposed:
            return pl.BlockSpec((None, 2 * nkv, PAGE), lambda i, pt: (pt[i, k], 0, 0))
        return pl.BlockSpec((None, PAGE, 2 * nkv), lambda i, pt: (pt[i, k], 0, 0))

    def cs(shape):
        nd = len(shape)
        return pl.BlockSpec(shape, lambda i, pt: (0,) * nd, pipeline_mode=pl.Buffered(1))

    in_specs = [page_spec(k) for k in range(npg)]
    operands = [pages] * npg
    if not transposed:
        in_specs.append(pl.BlockSpec((None, CMP_STRIDE, 2 * nkv), lambda i, pt: (pt[i, npg], 0, 0)))
        operands.append(pages)
    consts = (pet, peb, w1t, w1b, b1r, w2b, w2t, b2r, b2t)
    in_specs += [cs(a.shape) for a in consts]
    grid_spec = pltpu.PrefetchScalarGridSpec(
        num_scalar_prefetch=1,
        grid=(nt,),
        in_specs=in_specs,
        out_specs=[pl.BlockSpec((1, nch, nkv), lambda i, pt: (i, 0, 0)),
                   pl.BlockSpec((1, nch, nkv), lambda i, pt: (i, 0, 0)),
                   pl.BlockSpec((1, nkv, nch), lambda i, pt: (i, 0, 0))],
        scratch_shapes=[pltpu.VMEM((4, npg * PAGE + CMP_STRIDE, 128), F32),
                        pltpu.VMEM((2, 2, N_KV * nch + 16, half), BF16)],
    )
    return pl.pallas_call(
        functools.partial(_compress_kernel, transposed=transposed),
        grid_spec=grid_spec,
        out_shape=[jax.ShapeDtypeStruct((nt, nch, nkv), BF16),
                   jax.ShapeDtypeStruct((nt, nch, nkv), BF16),
                   jax.ShapeDtypeStruct((nt, nkv, nch), BF16)],
        compiler_params=_params("arbitrary"),
        name="compress",
    )(ptab, *operands, *consts)


def _qproj_prompt_kernel(x_ref, mod_ref, wqt_ref, wgt_ref, qt_ref, gt_ref):
    x = x_ref[0]
    shift = mod_ref[0, 0:1, :]
    scale = mod_ref[0, 1:2, :]
    ub = (x * (1.0 + scale) + shift).astype(BF16)
    qt = lax.dot_general(wqt_ref[...], ub, _NT, preferred_element_type=F32) * Q_SCALE
    qt_ref[0] = qt.astype(BF16)
    gt_ref[0] = jax.nn.sigmoid(lax.dot_general(wgt_ref[...], ub, _NT, preferred_element_type=F32))


def _qproj_prompt(x, mod, wqt, wgt, *, tm):
    bsz, t, d = x.shape
    hq = wqt.shape[0]
    ng = wgt.shape[0]
    return pl.pallas_call(
        _qproj_prompt_kernel,
        grid=(bsz, t // tm),
        in_specs=[pl.BlockSpec((1, tm, d), lambda i, j: (i, j, 0)),
                  pl.BlockSpec((1, 6, d), lambda i, j: (i, 0, 0)),
                  _const_spec(wqt.shape), _const_spec(wgt.shape)],
        out_specs=[pl.BlockSpec((1, hq, tm), lambda i, j: (i, 0, j)),
                   pl.BlockSpec((1, ng, tm), lambda i, j: (i, 0, j))],
        out_shape=[jax.ShapeDtypeStruct((bsz, hq, t), BF16), jax.ShapeDtypeStruct((bsz, ng, t), F32)],
        compiler_params=_params("parallel", "parallel"),
        name="q_proj_prompt",
    )(x, mod, wqt, wgt)


def _qproj_sample_kernel(x_ref, mod_ref, wq_ref, wg_ref, q_ref, g_ref):
    sb, tm, d = x_ref.shape
    x = x_ref[...]
    shift = mod_ref[:, 0:1, :]
    scale = mod_ref[:, 1:2, :]
    ub = (x * (1.0 + scale) + shift).reshape(sb * tm, d).astype(BF16)
    q = jnp.dot(ub, wq_ref[...], preferred_element_type=F32) * Q_SCALE
    q_ref[...] = q.reshape(sb, tm, -1)
    for i in range(3):
        gl = jnp.dot(ub, wg_ref[i], preferred_element_type=F32)
        g_ref[:, i] = jax.nn.sigmoid(gl).reshape(sb, tm, -1)


def _qproj_sample(x, mod, wq, wgx, *, sb):
    bsz, t, d = x.shape
    hq = wq.shape[1]
    return pl.pallas_call(
        _qproj_sample_kernel,
        grid=(bsz // sb,),
        in_specs=[pl.BlockSpec((sb, t, d), lambda i: (i, 0, 0)),
                  pl.BlockSpec((sb, 6, d), lambda i: (i, 0, 0)),
                  _const_spec(wq.shape), _const_spec(wgx.shape)],
        out_specs=[pl.BlockSpec((sb, t, hq), lambda i: (i, 0, 0)),
                   pl.BlockSpec((sb, 3, t, hq), lambda i: (i, 0, 0, 0))],
        out_shape=[jax.ShapeDtypeStruct((bsz, t, hq), F32), jax.ShapeDtypeStruct((bsz, 3, t, hq), F32)],
        compiler_params=_params("parallel"),
        name="q_proj_sample",
    )(x, mod, wq, wgx)


def _topk_rows(work, n_sel):
    rid = lax.broadcasted_iota(jnp.int32, work.shape, 1).astype(F32)

    def body(_, wk):
        m = jnp.max(wk, axis=1, keepdims=True)
        idx = jnp.min(jnp.where(wk == m, rid, 1e9), axis=1, keepdims=True)
        return jnp.where(rid == idx, -jnp.inf, wk)

    return lax.fori_loop(0, n_sel, body, work) == -jnp.inf


def _pipelined(n, issue, finish, ahead=2):
    pending = [issue(k) for k in range(min(ahead, n))]
    for k in range(n):
        if k + ahead < n:
            pending.append(issue(k + ahead))
        finish(k, pending.pop(0))


def _attn_prompt_kernel(qt_ref, gt_ref, kc_ref, vct_ref, ovt_ref, eb_ref, ks_ref, vst_ref, kw_ref, vwt_ref,
                        o_ref, selb_ref, oc_ref, ow_ref, acc_ref, m_ref):
    i = pl.program_id(1)
    qb = Q_BLOCK
    hd = HEAD_DIM
    s0 = i * qb
    nc = kc_ref.shape[1]
    ns = ovt_ref.shape[0]
    nq4 = GROUP * qb
    nblk = KV_CHUNK // SLC_LEN
    nvb = KV_CHUNK // qb
    zeros = jnp.zeros((hd, nq4), BF16)
    ones = jnp.ones((16, 1), BF16)
    qpos1 = s0 + lax.broadcasted_iota(jnp.int32, (1, qb), 1)
    win_lo = jnp.maximum(i - WINDOW // qb, 0)
    n_wb = WINDOW // qb + 1
    nwk = n_wb * qb

    def tile4(a):
        return jnp.concatenate([a] * GROUP, axis=1)

    def pair(g):
        return slice((g // 2) * 128, (g // 2 + 1) * 128)

    def grow(g):
        return slice(g * hd, (g + 1) * hd)

    def qpad(g):
        qg = jnp.concatenate([qt_ref[0, (g * GROUP + r) * hd:(g * GROUP + r + 1) * hd, :] for r in range(GROUP)],
                             axis=1)
        return jnp.concatenate([qg, zeros] if g % 2 == 0 else [zeros, qg], axis=0)

    def with_ones(vt):
        return jnp.concatenate([vt, jnp.broadcast_to(ones, (16, vt.shape[1]))], axis=0)

    cend = lax.broadcasted_iota(jnp.int32, (nc, qb), 0) * CMP_STRIDE + (CMP_LEN - 1)
    bias_c = tile4(jnp.where(cend <= qpos1, 0.0, NEG))
    seen_c = tile4((qpos1 >= CMP_LEN - 1).astype(F32))
    jj = lax.broadcasted_iota(jnp.int32, (ns, qb), 0)
    cur = jnp.right_shift(qpos1, SLC_SHIFT)
    forced = (jj == 0) | (jj == cur) | (jj == cur - 1)
    valid = jj * SLC_LEN <= qpos1
    ovt = ovt_ref[...]
    imps = [None] * N_KV
    w0 = pl.multiple_of(win_lo * qb, qb)
    dist = qpos1 - (w0 + lax.broadcasted_iota(jnp.int32, (nwk, qb), 0))
    bias_w = tile4(jnp.where((dist >= 0) & (dist < WINDOW), 0.0, NEG))

    def issue_cw(k):
        g = k % N_KV
        if k < N_KV:
            return jnp.dot(kc_ref[0, :, pair(g)], qpad(g), preferred_element_type=F32) + bias_c
        return jnp.dot(kw_ref[0, pl.ds(w0, nwk), pair(g)], qpad(g), preferred_element_type=F32) + bias_w

    def finish_cw(k, st):
        g = k % N_KV
        if k < N_KV:
            p = jnp.exp2(st - jnp.max(st, axis=0, keepdims=True))
            p = p * (seen_c / jnp.sum(p, axis=0, keepdims=True))
            vct = jnp.concatenate([vct_ref[0, t, grow(g), :] for t in range(vct_ref.shape[1])], axis=1)
            oc_ref[g] = jnp.dot(vct, p.astype(BF16), preferred_element_type=F32)
            psum = p[:, 0:qb]
            for r in range(1, GROUP):
                psum = psum + p[:, r * qb:(r + 1) * qb]
            imp = sum(jnp.dot(ovt, part, preferred_element_type=F32) for part in _split3(psum))
            imps[g] = jnp.where(valid, imp + jnp.where(forced, FORCE_BONUS, 0.0), NEG)
        else:
            pw = jnp.exp2(st - jnp.max(st, axis=0, keepdims=True))
            vwt = jnp.concatenate([vwt_ref[0, win_lo + t, grow(g), :] for t in range(n_wb)], axis=1)
            aw = jnp.dot(with_ones(vwt), pw.astype(BF16), preferred_element_type=F32)
            ow_ref[g] = aw[:hd] * (1.0 / aw[hd:hd + 1])

    _pipelined(2 * N_KV, issue_cw, finish_cw)

    sel = _topk_rows(jnp.stack(imps), min(TOP_N, ns))
    selb_ref[...] = jnp.where(sel, 0.0, NEG)

    m_ref[...] = jnp.full(m_ref.shape, NEG, F32)
    acc_ref[...] = jnp.zeros(acc_ref.shape, F32)
    zpad = jnp.zeros((128 - 16, nq4), BF16)

    def chunk(c, causal):
        k0 = pl.multiple_of(c * KV_CHUNK, KV_CHUNK)
        if causal:
            kpos = k0 + lax.broadcasted_iota(jnp.int32, (KV_CHUNK, qb), 0)
            cb = tile4(jnp.where(kpos <= qpos1, 0.0, NEG))

        def issue(g):
            kaug = jnp.concatenate([ks_ref[0, pl.ds(k0, KV_CHUNK), pair(g)], eb_ref[...]], axis=1)
            sb = tile4(selb_ref[g, pl.ds(pl.multiple_of(c * nblk, nblk), nblk), :])
            sbp = jnp.concatenate([sb, jnp.zeros((16 - nblk, nq4), F32)], axis=0).astype(BF16)
            qaug = jnp.concatenate([qpad(g), sbp, zpad], axis=0)
            sc = jnp.dot(kaug, qaug, preferred_element_type=F32)
            return sc + cb if causal else sc

        def finish(g, sc):
            m_old = m_ref[g]
            m_new = jnp.maximum(m_old, jnp.max(sc, axis=0, keepdims=True))
            a = jnp.exp2(m_old - m_new)
            pc = jnp.exp2(sc - m_new)
            vt = jnp.concatenate([vst_ref[0, c * nvb + t, grow(g), :] for t in range(nvb)], axis=1)
            acc_ref[g] = a * acc_ref[g] + jnp.dot(with_ones(vt), pc.astype(BF16), preferred_element_type=F32)
            m_ref[g] = m_new

        _pipelined(N_KV, issue, finish)

    n_full = s0 // KV_CHUNK

    def body(c, carry):
        chunk(c, False)
        return carry

    lax.fori_loop(0, n_full, body, 0)
    chunk(n_full, True)

    heads_out = []
    for g in range(N_KV):
        acc = acc_ref[g]
        o_s = acc[:hd] * (1.0 / acc[hd:hd + 1])
        o_c = oc_ref[g]
        o_w = ow_ref[g]
        for r in range(GROUP):
            h = g * GROUP + r
            ls = slice(r * qb, (r + 1) * qb)
            heads_out.append(gt_ref[0, 3 * h:3 * h + 1, :] * o_c[:, ls]
                             + gt_ref[0, 3 * h + 1:3 * h + 2, :] * o_s[:, ls]
                             + gt_ref[0, 3 * h + 2:3 * h + 3, :] * o_w[:, ls])
    out_t = jnp.concatenate(heads_out, axis=0)
    o_ref[0] = out_t.T.astype(BF16)


def _attn_prompt(qt, gt, kc, vct, ovt, eb, ks, vst, kw, vwt):
    bsz, hq, t = qt.shape
    ng = gt.shape[1]
    qb = Q_BLOCK
    nq4 = GROUP * qb

    def per_b(shape):
        nd = len(shape)
        return pl.BlockSpec((1,) + tuple(shape[1:]), lambda b, i: (b,) + (0,) * (nd - 1),
                            pipeline_mode=pl.Buffered(1))

    return pl.pallas_call(
        _attn_prompt_kernel,
        grid=(bsz, t // qb),
        in_specs=[pl.BlockSpec((1, hq, qb), lambda b, i: (b, 0, i)),
                  pl.BlockSpec((1, ng, qb), lambda b, i: (b, 0, i)),
                  per_b(kc.shape), per_b(vct.shape), _const_spec(ovt.shape), _const_spec(eb.shape),
                  per_b(ks.shape), per_b(vst.shape), per_b(kw.shape), per_b(vwt.shape)],
        out_specs=pl.BlockSpec((1, qb, hq), lambda b, i: (b, i, 0)),
        out_shape=jax.ShapeDtypeStruct((bsz, t, hq), BF16),
        scratch_shapes=[pltpu.VMEM((N_KV, t // SLC_LEN, qb), F32),
                        pltpu.VMEM((N_KV, HEAD_DIM, nq4), F32),
                        pltpu.VMEM((N_KV, HEAD_DIM, nq4), F32),
                        pltpu.VMEM((N_KV, HEAD_DIM + 16, nq4), F32),
                        pltpu.VMEM((N_KV, 1, nq4), F32)],
        compiler_params=_params("parallel", "arbitrary"),
        name="nsa_prompt",
    )(qt, gt, kc, vct, ovt, eb, ks, vst, kw, vwt)


def _topk_cols(work, n_sel):
    rid = lax.broadcasted_iota(jnp.int32, work.shape, 0).astype(F32)
    for _ in range(n_sel):
        m = jnp.max(work, axis=0, keepdims=True)
        idx = jnp.min(jnp.where(work == m, rid, 1e9), axis=0, keepdims=True)
        work = jnp.where(rid == idx, -jnp.inf, work)
    return jnp.where(work == -jnp.inf, 1.0, 0.0)


def _softmax_lanes(s):
    p = jnp.exp2(s - jnp.max(s, axis=1, keepdims=True))
    return p, jnp.sum(p, axis=1, keepdims=True)


def _heads_to_lanes(o, tq):
    hd = HEAD_DIM
    nkv = N_KV * hd
    lane = lax.broadcasted_iota(jnp.int32, (tq, nkv), 1)
    pieces = []
    for h in range(N_HEADS):
        g = h // GROUP
        blk = o[h * tq:(h + 1) * tq, :]
        blk = jnp.where((lane >= g * hd) & (lane < (g + 1) * hd), blk, 0.0)
        dst = (h * hd) % nkv
        shift = (dst - g * hd) % nkv
        pieces.append(pltpu.roll(blk, shift, axis=1) if shift else blk)
    cols = []
    per = nkv // hd
    for c in range(N_HEADS // per):
        slab = pieces[c * per]
        for k in range(1, per):
            slab = slab + pieces[c * per + k]
        cols.append(slab)
    return jnp.concatenate(cols, axis=1)


def _attn_sample_kernel(pt_ref, *refs, past_len, n_ns, spb):
    del pt_ref
    npg = past_len // PAGE
    page_refs = refs[:npg * spb]
    (q_ref, g_ref, kc_ref, vc_ref, kvn_ref, win_ref, ovt_ref, ex_ref,
     o_ref, kbuf, vbuf, kwbuf, vwbuf) = refs[npg * spb:]
    hd = HEAD_DIM
    nkv = N_KV * hd
    tq = q_ref.shape[1]
    nrow = N_HEADS * tq
    ngt = N_KV * tq
    nwin = win_ref.shape[2]
    nks = kbuf.shape[2]
    nkw = kwbuf.shape[2]
    nc = kc_ref.shape[1]
    seqs = range(spb)

    ztail = jnp.zeros((PAGE - tq, nkv), F32)
    eye = (lax.broadcasted_iota(jnp.int32, (nkv, nkv), 0)
           == lax.broadcasted_iota(jnp.int32, (nkv, nkv), 1)).astype(F32).astype(BF16)
    for s in seqs:
        for k in range(npg):
            pg = page_refs[s * npg + k]
            kbuf[s, :, k * PAGE:(k + 1) * PAGE] = pg[0:nkv, :].astype(BF16)
            vbuf[s, :, k * PAGE:(k + 1) * PAGE] = pg[nkv:2 * nkv, :].astype(BF16)
        kwbuf[s, :, 0:nwin] = win_ref[s, 0:nkv, :].astype(BF16)
        vwbuf[s, :, 0:nwin] = win_ref[s, nkv:2 * nkv, :].astype(BF16)
        kvn = kvn_ref[s]

        def new_t(slot, kvn=kvn):
            rows = jnp.concatenate([kvn[:, slot * nkv:(slot + 1) * nkv], ztail], axis=0).astype(BF16)
            return lax.dot_general(eye, rows, _NT, preferred_element_type=F32).astype(BF16)

        kbuf[s, :, past_len:past_len + PAGE] = new_t(2)
        vbuf[s, :, past_len:past_len + PAGE] = new_t(3)
        kwbuf[s, :, nwin:nwin + PAGE] = new_t(4)
        vwbuf[s, :, nwin:nwin + PAGE] = new_t(5)

    lane = lax.broadcasted_iota(jnp.int32, (tq, nkv), 1)
    qbd = []
    for s in seqs:
        q = q_ref[s]
        qrows = []
        for h in range(N_HEADS):
            g, r = divmod(h, GROUP)
            slab = q[:, g * nkv:(g + 1) * nkv]
            shift = ((g - r) * hd) % nkv
            moved = pltpu.roll(slab, shift, axis=1) if shift else slab
            qrows.append(jnp.where((lane >= g * hd) & (lane < (g + 1) * hd), moved, 0.0))
        qbd.append(jnp.concatenate(qrows, axis=0).astype(BF16))

    tpos = jnp.bitwise_and(lax.broadcasted_iota(jnp.int32, (nrow, 1), 0), tq - 1)
    qpos = past_len + tpos

    s_c = [lax.dot_general(qbd[s], kc_ref[s], _NT, preferred_element_type=F32) for s in seqs]
    s_w = [jnp.dot(qbd[s], kwbuf[s], preferred_element_type=F32) for s in seqs]
    s_s = [jnp.dot(qbd[s], kbuf[s], preferred_element_type=F32) for s in seqs]

    cend = lax.broadcasted_iota(jnp.int32, (nrow, nc), 1) * CMP_STRIDE + (CMP_LEN - 1)
    mc = cend <= qpos
    nsr = -(-n_ns // 8) * 8
    jj = lax.broadcasted_iota(jnp.int32, (nsr, ngt), 0)
    qp2 = past_len + jnp.bitwise_and(lax.broadcasted_iota(jnp.int32, (1, ngt), 1), tq - 1)
    cur = jnp.right_shift(qp2, SLC_SHIFT)
    forced = (jj == 0) | (jj == cur) | (jj == cur - 1)
    valid = (jj * SLC_LEN <= qp2) & (jj < n_ns)
    o_c, imp = [], []
    for s in seqs:
        p_c, l_c = _softmax_lanes(jnp.where(mc, s_c[s], NEG))
        p_c = jnp.where(mc, p_c * (1.0 / l_c), 0.0)
        o_c.append(jnp.dot(p_c.astype(BF16), vc_ref[s], preferred_element_type=F32))
        p4 = p_c.reshape(N_KV, GROUP, tq, nc)
        psum = p4[:, 0]
        for r in range(1, GROUP):
            psum = psum + p4[:, r]
        psum = psum.reshape(ngt, nc)
        it = sum(lax.dot_general(ovt_ref[...], part, _NT, preferred_element_type=F32)
                 for part in _split3(psum))
        imp.append(jnp.where(valid, it[:nsr] + jnp.where(forced, FORCE_BONUS, 0.0), NEG))

    jw = lax.broadcasted_iota(jnp.int32, (nrow, nkw), 1)
    kwpos = past_len + tq - (nwin + tq) + jw
    dist = qpos - kwpos
    mw = (dist >= 0) & (dist < WINDOW) & (kwpos >= 0) & (jw < nwin + tq)
    o_w = []
    for s in seqs:
        p_w, l_w = _softmax_lanes(jnp.where(mw, s_w[s], NEG))
        o_w.append(lax.dot_general(p_w.astype(BF16), vwbuf[s], _NT, preferred_element_type=F32) * (1.0 / l_w))

    nblk = ex_ref.shape[0]
    selk = []
    for s in seqs:
        sel_t = _topk_cols(imp[s], min(TOP_N, n_ns))
        sel_t = jnp.concatenate([sel_t, jnp.zeros((nblk - nsr, ngt), F32)], axis=0)
        sel_t = jnp.concatenate([sel_t, jnp.zeros((nblk, nblk - ngt), F32)], axis=1)
        sel = sel_t.T[:ngt]
        sk = jnp.dot(sel.astype(BF16), ex_ref[...], preferred_element_type=F32)
        selk.append(jnp.broadcast_to(sk.reshape(N_KV, 1, tq, nks), (N_KV, GROUP, tq, nks)).reshape(nrow, nks))

    kpos = lax.broadcasted_iota(jnp.int32, (nrow, nks), 1)
    causal = kpos <= qpos
    o_s = []
    for s in seqs:
        p_s, l_s = _softmax_lanes(jnp.where((selk[s] > 0.5) & causal, s_s[s], NEG))
        o_s.append(lax.dot_general(p_s.astype(BF16), vbuf[s], _NT, preferred_element_type=F32) * (1.0 / l_s))

    for s in seqs:
        o_ref[s] = (g_ref[s, 0] * _heads_to_lanes(o_c[s], tq) + g_ref[s, 1] * _heads_to_lanes(o_s[s], tq)
                    + g_ref[s, 2] * _heads_to_lanes(o_w[s], tq))


def _attn_sample(pages, page_table, q, gx, kc, vc, kvn, win, ovt, ex, *, past_len, spb):
    bsz, tq, hq = q.shape
    npg = past_len // PAGE
    nkv = N_KV * HEAD_DIM
    nwin = win.shape[2]
    n_ns = -(-(past_len + tq) // SLC_LEN)
    nks = past_len + PAGE
    nkw = nwin + PAGE
    kern = functools.partial(_attn_sample_kernel, past_len=past_len, n_ns=n_ns, spb=spb)

    def page_spec(s, k):
        return pl.BlockSpec((None, 2 * nkv, PAGE), lambda b, pt: (pt[b * spb + s, k], 1, 0))

    def cs(shape):
        nd = len(shape)
        return pl.BlockSpec(shape, lambda b, pt: (0,) * nd, pipeline_mode=pl.Buffered(1))

    def per_b(shape):
        nd = len(shape)
        return pl.BlockSpec((spb,) + tuple(shape[1:]), lambda b, pt: (b,) + (0,) * (nd - 1))

    in_specs = [page_spec(s, k) for s in range(spb) for k in range(npg)]
    in_specs += [per_b(q.shape), per_b(gx.shape), per_b(kc.shape), per_b(vc.shape), per_b(kvn.shape),
                 per_b(win.shape), cs(ovt.shape), cs(ex.shape)]
    grid_spec = pltpu.PrefetchScalarGridSpec(
        num_scalar_prefetch=1,
        grid=(bsz // spb,),
        in_specs=in_specs,
        out_specs=pl.BlockSpec((spb, tq, hq), lambda b, pt: (b, 0, 0)),
        scratch_shapes=[pltpu.VMEM((spb, nkv, nks), BF16), pltpu.VMEM((spb, nkv, nks), BF16),
                        pltpu.VMEM((spb, nkv, nkw), BF16), pltpu.VMEM((spb, nkv, nkw), BF16)],
    )
    return pl.pallas_call(
        kern,
        grid_spec=grid_spec,
        out_shape=jax.ShapeDtypeStruct((bsz, tq, hq), F32),
        compiler_params=_params("arbitrary"),
        name="nsa_sample",
    )(page_table, *([pages] * (npg * spb)), q, gx, kc, vc, kvn, win, ovt, ex)


def _oproj_kernel(o_ref, x_ref, mod_ref, wo_ref, g_ref, b_ref, out_ref, *, alpha):
    sb, tm, d = x_ref.shape
    ob = o_ref[...].reshape(sb * tm, -1).astype(BF16)
    mix = jnp.dot(ob, wo_ref[...], preferred_element_type=F32).reshape(sb, tm, d)
    gate = mod_ref[:, 2:3, :]
    y = alpha * x_ref[...] + (1.0 + gate) * mix
    out_ref[...] = _layer_norm(y, g_ref[...], b_ref[...])


def _oproj(o, x, mod, wo, g, b, *, sb, tm, alpha):
    bsz, t, d = x.shape
    hq = o.shape[-1]
    kern = functools.partial(_oproj_kernel, alpha=alpha)
    return pl.pallas_call(
        kern,
        grid=(bsz // sb, t // tm),
        in_specs=[pl.BlockSpec((sb, tm, hq), lambda i, j: (i, j, 0)),
                  pl.BlockSpec((sb, tm, d), lambda i, j: (i, j, 0)),
                  pl.BlockSpec((sb, 6, d), lambda i, j: (i, 0, 0)),
                  _const_spec(wo.shape), _const_spec(g.shape), _const_spec(b.shape)],
        out_specs=pl.BlockSpec((sb, tm, d), lambda i, j: (i, j, 0)),
        out_shape=jax.ShapeDtypeStruct((bsz, t, d), F32),
        compiler_params=_params("parallel", "parallel"),
        name="o_proj",
    )(o, x, mod, wo, g, b)


def _overlap(n_c, n_s):
    ci = np.arange(n_c)[:, None] * CMP_STRIDE
    sj = np.arange(n_s)[None, :] * SLC_LEN
    return ((ci <= sj + SLC_LEN - 1) & (ci + CMP_LEN - 1 >= sj)).astype(np.float32)


def kernel(x_prompt, x_sample, cache_kv, state_kv_win, state_pool, state_conv, page_table, c_prompt, c_sample, ada_w, ada_b, ln_g, ln_b, pool_w, pool_ls, ffn_w_up, ffn_b_up, ffn_w_conv, ffn_b_conv, ffn_w_down, w_kv, cmp_pe, cmp_w1, cmp_b1, cmp_w2, cmp_b2, nsa_w_qg, nsa_w_o):
    bp, t, d = x_prompt.shape
    bs, ts, _ = x_sample.shape
    depth = ada_w.shape[0]
    n_a = pool_w.shape[0]
    f2 = ffn_w_up.shape[-1]
    f = f2 // 2
    past_len = page_table.shape[1] * PAGE
    nkv = N_KV * HEAD_DIM
    hq = N_HEADS * HEAD_DIM
    alpha = float((2 * depth) ** 0.25)
    assert t % (16 * PAGE) == 0 and t >= WINDOW + Q_BLOCK
    assert ts == 8 and past_len == 16 * PAGE and state_kv_win.shape[1] == WINDOW
    assert f % 1408 == 0 or f % 256 == 0

    tm_p = 512
    sb_pool = min(bs, 64)
    sb_ffn = min(bs, 16)
    cf = 1408 if f % 1408 == 0 else 256

    ada = _ada(jnp.concatenate([c_prompt, c_sample], axis=0), ada_w, ada_b)
    ada = ada.reshape(depth, bp + bs, 6, d)
    mod_p = [ada[l, :bp] for l in range(depth)]
    mod_s = [ada[l, bp:] for l in range(depth)]

    pool_wb = pool_w.astype(BF16)
    wup_b = ffn_w_up.astype(BF16)
    wdn_b = ffn_w_down.astype(BF16)
    w_kv_b = w_kv.astype(BF16)
    w_kvt_b = w_kv.T.astype(BF16)
    wq_b = nsa_w_qg[:, :, :hq].astype(BF16)
    wqt_b = jnp.swapaxes(nsa_w_qg[:, :, :hq], 1, 2).astype(BF16)
    wgt_b = jnp.swapaxes(nsa_w_qg[:, :, hq:], 1, 2).astype(BF16)
    wg = nsa_w_qg[:, :, hq:].reshape(-1, d, N_HEADS, 3)
    wgx_b = jnp.repeat(jnp.moveaxis(wg, 3, 1), HEAD_DIM, axis=3).astype(BF16)
    wo_b = nsa_w_o.astype(BF16)

    def vec(a):
        return a.reshape(1, -1)

    pool0 = jnp.zeros((bp, POOL_HALO, d), F32)
    conv0 = jnp.zeros((bp, CONV_HALO, f2), F32)
    pool_prev_s = jnp.pad(state_pool, ((0, 0), (0, 0), (POOL_HALO - state_pool.shape[2], 0), (0, 0)))
    conv_prev_s = jnp.pad(state_conv, ((0, 0), (0, 0), (CONV_HALO - state_conv.shape[2], 0), (0, 0)))

    xp, xs = x_prompt, x_sample
    pool_p, pool_s, conv_p, conv_s = [], [], [], []
    ctx_p = ctx_s = None
    kv_p = kv_s = None

    for l in range(depth):
        g1, b1, g2, b2 = vec(ln_g[l, 0]), vec(ln_b[l, 0]), vec(ln_g[l, 1]), vec(ln_b[l, 1])
        if l < n_a:
            xp, st = _pool_layer(xp, mod_p[l], pool0, pool_wb[l], vec(pool_ls[l]), g1, b1,
                                 sb=1, tm=tm_p, start_pos=0, alpha=alpha)
            pool_p.append(st[:, 1:])
            xs, st = _pool_layer(xs, mod_s[l], pool_prev_s[l], pool_wb[l], vec(pool_ls[l]), g1, b1,
                                 sb=sb_pool, tm=ts, start_pos=past_len, alpha=alpha)
            pool_s.append(st[:, 1:])
        else:
            jn = l - n_a
            qt, gt = _qproj_prompt(xp, mod_p[l], wqt_b[jn], wgt_b[jn], tm=tm_p)
            op = _attn_prompt(qt, gt, *ctx_p)
            xp = _oproj(op, xp, mod_p[l], wo_b[jn], g1, b1, sb=1, tm=tm_p, alpha=alpha)
            qs, gs = _qproj_sample(xs, mod_s[l], wq_b[jn], wgx_b[jn], sb=sb_pool)
            os_ = _attn_sample(ctx_s[0], page_table, qs, gs, *ctx_s[1:], past_len=past_len,
                               spb=2 if bs % 2 == 0 else 1)
            xs = _oproj(os_, xs, mod_s[l], wo_b[jn], g1, b1, sb=sb_pool, tm=ts, alpha=alpha)

        xp, st = _ffn_layer(xp, mod_p[l], conv0, wup_b[l], vec(ffn_b_up[l]), ffn_w_conv[l], vec(ffn_b_conv[l]),
                            wdn_b[l], g2, b2, sb=1, tm=tm_p, cf=cf, alpha=alpha)
        conv_p.append(st[:, CONV_HALO - (CONV_W - 1):])
        xs, st = _ffn_layer(xs, mod_s[l], conv_prev_s[l], wup_b[l], vec(ffn_b_up[l]), ffn_w_conv[l],
                            vec(ffn_b_conv[l]), wdn_b[l], g2, b2, sb=sb_ffn, tm=ts, cf=cf, alpha=alpha)
        conv_s.append(st[:, CONV_HALO - (CONV_W - 1):])

        if l == n_a - 1:
            kvpt, kvwt, cmp_rows, ks, kw, vst, vwt = _kvproj_prompt(xp, w_kv_b, w_kvt_b, tm=tm_p)
            kv_p = (kvpt, kvwt)
            tiles = t // (16 * PAGE)
            ppb = t // PAGE
            base = (np.arange(bp)[:, None] * ppb + np.arange(tiles)[None, :] * 16).reshape(-1, 1)
            ids = base + np.arange(17)[None, :]
            last = (np.arange(bp)[:, None] * ppb + ppb - 1).repeat(tiles, axis=1).reshape(-1)
            ids[:, 16] = np.minimum(ids[:, 16], last)
            kc, _, vct = _compress(cmp_rows.reshape(bp * ppb, PAGE, 2 * nkv), jnp.asarray(ids, jnp.int32),
                                   cmp_pe, cmp_w1, cmp_b1, cmp_w2, cmp_b2, transposed=False)
            n_c = tiles * 128
            ovt = jnp.asarray(_overlap(n_c, t // SLC_LEN).T, BF16)
            eb = (np.arange(KV_CHUNK)[:, None] // SLC_LEN == np.arange(128)[None, :]).astype(np.float32)
            ctx_p = (kc.reshape(bp, n_c, nkv), vct.reshape(bp, tiles, nkv, 128), ovt, jnp.asarray(eb, BF16),
                     ks, vst, kw, vwt)

            kvs = _kvproj_sample(xs, w_kv_b, sb=sb_pool)
            kv_s = kvs
            assert (past_len + ts - CMP_LEN) // CMP_STRIDE + 1 == past_len // CMP_STRIDE - 1
            pages_t = jnp.transpose(cache_kv, (0, 2, 3, 4, 1)).reshape(cache_kv.shape[0], 4 * nkv, PAGE)
            kc_s, vc_s, _ = _compress(pages_t, page_table.astype(jnp.int32), cmp_pe, cmp_w1, cmp_b1, cmp_w2, cmp_b2,
                                      transposed=True)
            n_ns = -(-(past_len + ts) // SLC_LEN)
            ov_s = np.zeros((128, 128), np.float32)
            n_cs = (past_len + ts - CMP_LEN) // CMP_STRIDE + 1
            ov_s[:n_cs, :n_ns] = _overlap(n_cs, n_ns)
            nks = past_len + PAGE
            ex = (np.arange(nks)[None, :] // SLC_LEN == np.arange(128)[:, None]).astype(np.float32)
            win_t = jnp.transpose(state_kv_win, (0, 2, 3, 4, 1)).reshape(bs, 2 * nkv, state_kv_win.shape[1])
            ctx_s = (pages_t, kc_s, vc_s, kvs, win_t, jnp.asarray(ov_s.T, BF16), jnp.asarray(ex, BF16))

    kvpt, kvwt = kv_p
    wlen = min(WINDOW, t)
    wbuf = state_kv_win.shape[1]
    win_s = jnp.concatenate([state_kv_win, kv_s[:, :, 4 * nkv:].reshape(bs, ts, 2, N_KV, HEAD_DIM)], axis=1)[:, -wbuf:]
    kv_prompt = jnp.transpose(kvpt.reshape(bp, 4, N_KV, HEAD_DIM, t), (0, 4, 1, 2, 3))
    win_prompt = jnp.transpose(kvwt[:, :, t - wlen:].reshape(bp, 2, N_KV, HEAD_DIM, wlen), (0, 4, 1, 2, 3))
    return (xp, xs,
            kv_prompt,
            kv_s[:, :, :4 * nkv].reshape(bs, ts, 4, N_KV, HEAD_DIM),
            win_prompt,
            win_s,
            jnp.stack(pool_p), jnp.stack(pool_s), jnp.stack(conv_p), jnp.stack(conv_s))
```

```python
import functools

import numpy as np
import jax
import jax.numpy as jnp
from jax import lax
from jax.experimental import pallas as pl
from jax.experimental.pallas import tpu as pltpu

F32 = jnp.float32
BF16 = jnp.bfloat16

POOL_WINDOWS = (2, 4, 8, 16)
POOL_HALO = 16
N_HEADS = 16
HEAD_DIM = 64
N_KV = 4
GROUP = N_HEADS // N_KV
CMP_LEN = 32
CMP_STRIDE = 16
SLC_LEN = 64
SLC_SHIFT = 6
TOP_N = 16
WINDOW = 512
Q_BLOCK = 128
CONV_W = 3
CONV_HALO = 8
PAGE = 128
LN_EPS = 1e-5
NEG = -1e30
FORCE_BONUS = 1e4
KV_CHUNK = 512
Q_SCALE = HEAD_DIM ** -0.5 * 1.4426950408889634
EXP_HEADROOM = 100.0
VMEM_LIMIT = 56 * 1024 * 1024

_NT = (((1,), (1,)), ((), ()))


def _params(*sem):
    return pltpu.CompilerParams(dimension_semantics=sem, vmem_limit_bytes=VMEM_LIMIT)


def _const_spec(shape):
    nd = len(shape)
    return pl.BlockSpec(shape, lambda *_: (0,) * nd, pipeline_mode=pl.Buffered(1))


def _layer_norm(y, g, b):
    mu = jnp.mean(y, axis=-1, keepdims=True)
    d = y - mu
    var = jnp.mean(d * d, axis=-1, keepdims=True)
    return d * lax.rsqrt(var + LN_EPS) * g + b


def _split3(x):
    hi = x.astype(BF16)
    r1 = x - hi.astype(F32)
    mid = r1.astype(BF16)
    lo = (r1 - mid.astype(F32)).astype(BF16)
    return hi, mid, lo


def _ada_kernel(c_ref, w_ref, b_ref, o_ref):
    c = c_ref[...]
    s = c * jax.nn.sigmoid(c)
    o_ref[0] = jnp.dot(s.astype(BF16), w_ref[0].astype(BF16), preferred_element_type=F32) + b_ref[0]


def _ada(c_all, ada_w, ada_b):
    depth, d, n = ada_w.shape
    r = c_all.shape[0]
    tn = n // 4
    return pl.pallas_call(
        _ada_kernel,
        grid=(depth, n // tn),
        in_specs=[pl.BlockSpec((r, d), lambda l, j: (0, 0)),
                  pl.BlockSpec((1, d, tn), lambda l, j: (l, 0, j)),
                  pl.BlockSpec((1, 1, tn), lambda l, j: (l, 0, j))],
        out_specs=pl.BlockSpec((1, r, tn), lambda l, j: (l, 0, j)),
        out_shape=jax.ShapeDtypeStruct((depth, r, n), F32),
        compiler_params=_params("parallel", "parallel"),
        name="ada_params",
    )(c_all, ada_w, ada_b.reshape(depth, 1, n))


def _pool_kernel(x_ref, mod_ref, prev_ref, pw_ref, ls_ref, g_ref, b_ref, o_ref, ps_ref, ext_ref,
                 *, start_pos, nj, alpha):
    j = pl.program_id(1)
    sb, tm, d = x_ref.shape
    pg = d // len(POOL_WINDOWS)
    h = POOL_HALO
    x = x_ref[...]
    shift = mod_ref[:, 0:1, :]
    scale = mod_ref[:, 1:2, :]
    gate = mod_ref[:, 2:3, :]
    u = x * (1.0 + scale) + shift

    @pl.when(j == 0)
    def _():
        ext_ref[:, 0:h, :] = prev_ref[...]

    ext_ref[:, h:h + tm, :] = u
    pos = start_pos + j * tm + lax.broadcasted_iota(jnp.int32, (1, tm, pg), 1)
    outs = []
    for gi, w in enumerate(POOL_WINDOWS):
        c0 = gi * pg
        acc = ext_ref[:, h:h + tm, c0:c0 + pg]
        for k in range(1, w):
            acc = acc + ext_ref[:, h - k:h - k + tm, c0:c0 + pg]
        cnt = jnp.minimum(pos + 1, w).astype(F32)
        pooled = acc / cnt - u[:, :, c0:c0 + pg]
        outs.append(jnp.dot(pooled.reshape(sb * tm, pg).astype(BF16), pw_ref[gi],
                            preferred_element_type=F32))
    mixed = jnp.concatenate(outs, axis=-1) * ls_ref[...]
    y = alpha * x + (1.0 + gate) * mixed.reshape(sb, tm, d)
    o_ref[...] = _layer_norm(y, g_ref[...], b_ref[...])

    if nj > 1:
        @pl.when(j < nj - 1)
        def _():
            ext_ref[:, 0:h, :] = ext_ref[:, tm:tm + h, :]

    @pl.when(j == nj - 1)
    def _():
        ps_ref[...] = ext_ref[:, tm:tm + h, :]


def _pool_layer(x, mod, prev, pw, ls, g, b, *, sb, tm, start_pos, alpha):
    bsz, t, d = x.shape
    nj = t // tm
    h = POOL_HALO
    kern = functools.partial(_pool_kernel, start_pos=start_pos, nj=nj, alpha=alpha)
    return pl.pallas_call(
        kern,
        grid=(bsz // sb, nj),
        in_specs=[pl.BlockSpec((sb, tm, d), lambda i, j: (i, j, 0)),
                  pl.BlockSpec((sb, 6, d), lambda i, j: (i, 0, 0)),
                  pl.BlockSpec((sb, h, d), lambda i, j: (i, 0, 0)),
                  _const_spec(pw.shape), _const_spec(ls.shape), _const_spec(g.shape), _const_spec(b.shape)],
        out_specs=[pl.BlockSpec((sb, tm, d), lambda i, j: (i, j, 0)),
                   pl.BlockSpec((sb, h, d), lambda i, j: (i, 0, 0))],
        out_shape=[jax.ShapeDtypeStruct((bsz, t, d), F32), jax.ShapeDtypeStruct((bsz, h, d), F32)],
        scratch_shapes=[pltpu.VMEM((sb, h + tm, d), F32)],
        compiler_params=_params("parallel", "arbitrary"),
        name="pool_layer",
    )(x, mod, prev, pw, ls, g, b)


def _ffn_kernel(x_ref, mod_ref, cprev_ref, wup_ref, bup_ref, wc_ref, bc_ref, wdn_ref, g_ref, b_ref,
                o_ref, cs_ref, hext_ref, *, cf, alpha):
    j = pl.program_id(1)
    sb, tm, d = x_ref.shape
    f = wdn_ref.shape[0]
    hh = CONV_HALO
    x = x_ref[...]
    shift = mod_ref[:, 3:4, :]
    scale = mod_ref[:, 4:5, :]
    gate = mod_ref[:, 5:6, :]
    ub = (x * (1.0 + scale) + shift).reshape(sb * tm, d).astype(BF16)

    @pl.when(j == 0)
    def _():
        cs_ref[...] = cprev_ref[...]

    acc = jnp.zeros((sb * tm, d), F32)
    for c in range(f // cf):
        parts = []
        for p in range(2):
            c0 = p * f + c * cf
            hp = jnp.dot(ub, wup_ref[:, c0:c0 + cf], preferred_element_type=F32) + bup_ref[:, c0:c0 + cf]
            hext_ref[p, :, 0:hh, :] = cs_ref[:, :, c0:c0 + cf]
            hext_ref[p, :, hh:hh + tm, :] = hp.reshape(sb, tm, cf)
            hc = (bc_ref[:, c0:c0 + cf]
                  + hext_ref[p, :, hh - 2:hh - 2 + tm, :] * wc_ref[0:1, c0:c0 + cf]
                  + hext_ref[p, :, hh - 1:hh - 1 + tm, :] * wc_ref[1:2, c0:c0 + cf]
                  + hext_ref[p, :, hh:hh + tm, :] * wc_ref[2:3, c0:c0 + cf])
            cs_ref[:, :, c0:c0 + cf] = hext_ref[p, :, tm:tm + hh, :]
            parts.append(hc)
        a, v = parts
        gated = (a * jax.nn.sigmoid(a)) * v
        acc = acc + jnp.dot(gated.reshape(sb * tm, cf).astype(BF16), wdn_ref[c * cf:(c + 1) * cf, :],
                            preferred_element_type=F32)
    y = alpha * x + (1.0 + gate) * acc.reshape(sb, tm, d)
    o_ref[...] = _layer_norm(y, g_ref[...], b_ref[...])


def _ffn_layer(x, mod, cprev, wup, bup, wc, bc, wdn, g, b, *, sb, tm, cf, alpha):
    bsz, t, d = x.shape
    f2 = wup.shape[1]
    hh = CONV_HALO
    kern = functools.partial(_ffn_kernel, cf=cf, alpha=alpha)
    return pl.pallas_call(
        kern,
        grid=(bsz // sb, t // tm),
        in_specs=[pl.BlockSpec((sb, tm, d), lambda i, j: (i, j, 0)),
                  pl.BlockSpec((sb, 6, d), lambda i, j: (i, 0, 0)),
                  pl.BlockSpec((sb, hh, f2), lambda i, j: (i, 0, 0)),
                  _const_spec(wup.shape), _const_spec(bup.shape), _const_spec(wc.shape),
                  _const_spec(bc.shape), _const_spec(wdn.shape), _const_spec(g.shape), _const_spec(b.shape)],
        out_specs=[pl.BlockSpec((sb, tm, d), lambda i, j: (i, j, 0)),
                   pl.BlockSpec((sb, hh, f2), lambda i, j: (i, 0, 0))],
        out_shape=[jax.ShapeDtypeStruct((bsz, t, d), F32), jax.ShapeDtypeStruct((bsz, hh, f2), F32)],
        scratch_shapes=[pltpu.VMEM((2, sb, hh + tm, cf), F32)],
        compiler_params=_params("parallel", "arbitrary"),
        name="conv_ffn_layer",
    )(x, mod, cprev, wup, bup, wc, bc, wdn, g, b)


def _kvproj_prompt_kernel(x_ref, w_ref, wt_ref, kvpt_ref, kvwt_ref, cmp_ref, ks_ref, kw_ref, vst_ref, vwt_ref):
    tm = x_ref.shape[1]
    nkv = N_KV * HEAD_DIM
    xb = x_ref[0].astype(BF16)
    kv = jnp.dot(xb, w_ref[...], preferred_element_type=F32)
    cmp_ref[0] = kv[:, :2 * nkv].astype(BF16)
    ks_ref[0] = kv[:, 2 * nkv:3 * nkv].astype(BF16)
    kw_ref[0] = kv[:, 4 * nkv:5 * nkv].astype(BF16)
    kvt = lax.dot_general(wt_ref[...], xb, _NT, preferred_element_type=F32)
    kvpt_ref[0] = kvt[:4 * nkv]
    kvwt_ref[0] = kvt[4 * nkv:]
    for c in range(tm // Q_BLOCK):
        cols = slice(c * Q_BLOCK, (c + 1) * Q_BLOCK)
        vst_ref[0, c] = kvt[3 * nkv:4 * nkv, cols].astype(BF16)
        vwt_ref[0, c] = kvt[5 * nkv:6 * nkv, cols].astype(BF16)


def _kvproj_prompt(x, w_kv_b, w_kvt_b, *, tm):
    bsz, t, d = x.shape
    nkv = N_KV * HEAD_DIM
    nb = tm // Q_BLOCK
    return pl.pallas_call(
        _kvproj_prompt_kernel,
        grid=(bsz, t // tm),
        in_specs=[pl.BlockSpec((1, tm, d), lambda i, j: (i, j, 0)),
                  _const_spec(w_kv_b.shape), _const_spec(w_kvt_b.shape)],
        out_specs=[pl.BlockSpec((1, 4 * nkv, tm), lambda i, j: (i, 0, j)),
                   pl.BlockSpec((1, 2 * nkv, tm), lambda i, j: (i, 0, j)),
                   pl.BlockSpec((1, tm, 2 * nkv), lambda i, j: (i, j, 0)),
                   pl.BlockSpec((1, tm, nkv), lambda i, j: (i, j, 0)),
                   pl.BlockSpec((1, tm, nkv), lambda i, j: (i, j, 0)),
                   pl.BlockSpec((1, nb, nkv, Q_BLOCK), lambda i, j: (i, j, 0, 0)),
                   pl.BlockSpec((1, nb, nkv, Q_BLOCK), lambda i, j: (i, j, 0, 0))],
        out_shape=[jax.ShapeDtypeStruct((bsz, 4 * nkv, t), F32),
                   jax.ShapeDtypeStruct((bsz, 2 * nkv, t), F32),
                   jax.ShapeDtypeStruct((bsz, t, 2 * nkv), BF16),
                   jax.ShapeDtypeStruct((bsz, t, nkv), BF16),
                   jax.ShapeDtypeStruct((bsz, t, nkv), BF16),
                   jax.ShapeDtypeStruct((bsz, t // Q_BLOCK, nkv, Q_BLOCK), BF16),
                   jax.ShapeDtypeStruct((bsz, t // Q_BLOCK, nkv, Q_BLOCK), BF16)],
        compiler_params=_params("parallel", "parallel"),
        name="kv_proj_prompt",
    )(x, w_kv_b, w_kvt_b)


def _kvproj_sample_kernel(x_ref, w_ref, kv_ref):
    sb, tm, d = x_ref.shape
    xb = x_ref[...].reshape(sb * tm, d).astype(BF16)
    kv_ref[...] = jnp.dot(xb, w_ref[...], preferred_element_type=F32).reshape(sb, tm, -1)


def _kvproj_sample(x, w_kv_b, *, sb):
    bsz, t, d = x.shape
    n = w_kv_b.shape[1]
    return pl.pallas_call(
        _kvproj_sample_kernel,
        grid=(bsz // sb,),
        in_specs=[pl.BlockSpec((sb, t, d), lambda i: (i, 0, 0)), _const_spec(w_kv_b.shape)],
        out_specs=pl.BlockSpec((sb, t, n), lambda i: (i, 0, 0)),
        out_shape=jax.ShapeDtypeStruct((bsz, t, n), F32),
        compiler_params=_params("parallel"),
        name="kv_proj_sample",
    )(x, w_kv_b)


def _compress_kernel(pt_ref, *refs, transposed):
    del pt_ref
    npg = 16
    page_refs = refs[:npg]
    rest = refs[npg:]
    if not transposed:
        look_ref, swap_ref, rest = rest[0], rest[1], rest[2:]
    (pet_ref, peb_ref, w1_ref, b1_ref, w2_ref, w2t_ref, b2_ref, b2t_ref,
     kc_ref, vc_ref, vct_ref, xs_ref, p_ref) = rest
    nch = npg * PAGE // CMP_STRIDE
    cpp = PAGE // CMP_STRIDE
    gs = nch + 2 * cpp
    hd = HEAD_DIM
    hidden = b1_ref.shape[-1]
    pr = lax.broadcasted_iota(jnp.int32, (PAGE, PAGE), 0)
    pc = lax.broadcasted_iota(jnp.int32, (PAGE, PAGE), 1)
    perm = (jnp.bitwise_and(pc, CMP_STRIDE - 1) * cpp + jnp.right_shift(pc, 4) == pr).astype(F32).astype(BF16)

    def permuted(ref):
        if transposed:
            pg = ref[...].astype(BF16)
            sw = jnp.concatenate([pg[(b ^ 1) * hd:((b ^ 1) + 1) * hd] for b in range(pg.shape[0] // hd)], axis=0)
            return (lax.dot_general(perm, pg, _NT, preferred_element_type=F32),
                    lax.dot_general(perm, sw, _NT, preferred_element_type=F32))
        tn = jnp.dot(perm, ref[...], preferred_element_type=F32)
        return tn, jnp.dot(tn.astype(BF16), swap_ref[...], preferred_element_type=F32)

    low = lax.broadcasted_iota(jnp.int32, (2 * cpp, 128), 1) < hd

    def scatter(pair_n, pair_s, row0):
        for sl in range(4):
            s = sl // 2
            ga = 2 * (sl % 2)
            cols = slice(sl * 128, (sl + 1) * 128)
            for m in range(CMP_STRIDE // 2):
                r0 = slice(2 * m * cpp, (2 * m + 1) * cpp)
                r1 = slice((2 * m + 1) * cpp, (2 * m + 2) * cpp)
                n0 = jnp.concatenate([t[r0, cols] for t in pair_n], axis=0)
                n1 = jnp.concatenate([t[r1, cols] for t in pair_n], axis=0)
                s0 = jnp.concatenate([t[r0, cols] for t in pair_s], axis=0)
                s1 = jnp.concatenate([t[r1, cols] for t in pair_s], axis=0)
                xs_ref[s, ga * gs + row0:ga * gs + row0 + 2 * cpp, m * 128:(m + 1) * 128] = (
                    jnp.where(low, n0, s1).astype(BF16))
                xs_ref[s, (ga + 1) * gs + row0:(ga + 1) * gs + row0 + 2 * cpp, m * 128:(m + 1) * 128] = (
                    jnp.where(low, s0, n1).astype(BF16))

    for k in range(0, npg, 2):
        ta, tb = permuted(page_refs[k]), permuted(page_refs[k + 1])
        scatter((ta[0], tb[0]), (ta[1], tb[1]), k * cpp)
    if transposed:
        for s in range(2):
            for g in range(N_KV):
                xs_ref[s, g * gs + nch:(g + 1) * gs, :] = jnp.zeros((2 * cpp, xs_ref.shape[-1]), BF16)
    else:
        tl = permuted(look_ref)
        scatter((tl[0], tl[0]), (tl[1], tl[1]), nch)

    nx = N_KV * gs
    for s in range(2):
        tails = []
        for pe_ref in (pet_ref, peb_ref):
            pe = pe_ref[s]
            hi = pe.astype(BF16).astype(F32)
            tails += [hi, pe - hi]
        tails.append(jnp.zeros((12, tails[0].shape[1]), F32))
        xs_ref[s, nx:nx + 16, :] = jnp.concatenate(tails, axis=0).astype(BF16)

    for s in range(2):
        p_ref[...] = jnp.dot(xs_ref[s], w1_ref[s], preferred_element_type=F32)
        cvec = (p_ref[nx:nx + 1, 0:hidden] + p_ref[nx + 1:nx + 2, 0:hidden]
                + p_ref[nx + 2:nx + 3, hidden:] + p_ref[nx + 3:nx + 4, hidden:] + b1_ref[s])
        pre = jnp.concatenate([p_ref[g * gs:g * gs + nch, 0:hidden] + p_ref[g * gs + 1:g * gs + nch + 1, hidden:]
                               for g in range(N_KV)], axis=0) + cvec
        hid = jax.nn.gelu(pre).astype(BF16)
        out = jnp.dot(hid, w2_ref[s], preferred_element_type=F32) + b2_ref[s]
        nat = jnp.concatenate([out[g * nch:(g + 1) * nch, :] for g in range(N_KV)], axis=1)
        if s == 0:
            kc_ref[0] = nat.astype(BF16)
        else:
            vc_ref[0] = nat.astype(BF16)
            outs_t = [lax.dot_general(w2t_ref[s], hid[g * nch:(g + 1) * nch, :], _NT,
                                      preferred_element_type=F32) + b2t_ref[s] for g in range(N_KV)]
            vct_ref[0] = jnp.concatenate(outs_t, axis=0).astype(BF16)


def _compress(pages, ptab, pe, w1, b1, w2, b2, *, transposed):
    nt = ptab.shape[0]
    npg = 16
    nkv = N_KV * HEAD_DIM
    half = CMP_STRIDE * HEAD_DIM
    hidden = w1.shape[-1]
    pet = pe[:, :CMP_STRIDE].reshape(2, 1, half)
    peb = pe[:, CMP_STRIDE:].reshape(2, 1, half)
    w1c = jnp.concatenate([w1[:, :half], w1[:, half:]], axis=2).astype(BF16)
    w2b = w2.astype(BF16)
    w2t = jnp.swapaxes(w2, 1, 2).astype(BF16)
    b1r = b1.reshape(2, 1, hidden)
    b2r = b2.reshape(2, 1, HEAD_DIM)
    b2t = b2.reshape(2, HEAD_DIM, 1)
    nch = npg * PAGE // CMP_STRIDE

    def page_spec(k):
        if transposed:
            return pl.BlockSpec((None, 2 * nkv, PAGE), lambda i, pt: (pt[i, k], 0, 0))
        return pl.BlockSpec((None, PAGE, 2 * nkv), lambda i, pt: (pt[i, k], 0, 0))

    def cs(shape):
        nd = len(shape)
        return pl.BlockSpec(shape, lambda i, pt: (0,) * nd, pipeline_mode=pl.Buffered(1))

    in_specs = [page_spec(k) for k in range(npg)]
    operands = [pages] * npg
    consts = (pet, peb, w1c, b1r, w2b, w2t, b2r, b2t)
    if not transposed:
        in_specs.append(page_spec(npg))
        operands.append(pages)
        f = np.arange(2 * nkv)
        swap = (f[:, None] == (f[None, :] ^ HEAD_DIM)).astype(np.float32)
        consts = (jnp.asarray(swap, BF16),) + consts
    in_specs += [cs(a.shape) for a in consts]
    gs = nch + 2 * (PAGE // CMP_STRIDE)
    grid_spec = pltpu.PrefetchScalarGridSpec(
        num_scalar_prefetch=1,
        grid=(nt,),
        in_specs=in_specs,
        out_specs=[pl.BlockSpec((1, nch, nkv), lambda i, pt: (i, 0, 0)),
                   pl.BlockSpec((1, nch, nkv), lambda i, pt: (i, 0, 0)),
                   pl.BlockSpec((1, nkv, nch), lambda i, pt: (i, 0, 0))],
        scratch_shapes=[pltpu.VMEM((2, N_KV * gs + 16, half), BF16),
                        pltpu.VMEM((N_KV * gs + 16, 2 * hidden), F32)],
    )
    return pl.pallas_call(
        functools.partial(_compress_kernel, transposed=transposed),
        grid_spec=grid_spec,
        out_shape=[jax.ShapeDtypeStruct((nt, nch, nkv), BF16),
                   jax.ShapeDtypeStruct((nt, nch, nkv), BF16),
                   jax.ShapeDtypeStruct((nt, nkv, nch), BF16)],
        compiler_params=_params("arbitrary"),
        name="compress",
    )(ptab, *operands, *consts)


def _qproj_prompt_kernel(x_ref, mod_ref, wqt_ref, wgt_ref, qt_ref, gt_ref):
    x = x_ref[0]
    shift = mod_ref[0, 0:1, :]
    scale = mod_ref[0, 1:2, :]
    ub = (x * (1.0 + scale) + shift).astype(BF16)
    qt = lax.dot_general(wqt_ref[...], ub, _NT, preferred_element_type=F32) * Q_SCALE
    qt_ref[0] = qt.astype(BF16)
    gt_ref[0] = jax.nn.sigmoid(lax.dot_general(wgt_ref[...], ub, _NT, preferred_element_type=F32))


def _qproj_prompt(x, mod, wqt, wgt, *, tm):
    bsz, t, d = x.shape
    hq = wqt.shape[0]
    ng = wgt.shape[0]
    return pl.pallas_call(
        _qproj_prompt_kernel,
        grid=(bsz, t // tm),
        in_specs=[pl.BlockSpec((1, tm, d), lambda i, j: (i, j, 0)),
                  pl.BlockSpec((1, 6, d), lambda i, j: (i, 0, 0)),
                  _const_spec(wqt.shape), _const_spec(wgt.shape)],
        out_specs=[pl.BlockSpec((1, hq, tm), lambda i, j: (i, 0, j)),
                   pl.BlockSpec((1, ng, tm), lambda i, j: (i, 0, j))],
        out_shape=[jax.ShapeDtypeStruct((bsz, hq, t), BF16), jax.ShapeDtypeStruct((bsz, ng, t), F32)],
        compiler_params=_params("parallel", "parallel"),
        name="q_proj_prompt",
    )(x, mod, wqt, wgt)


def _qproj_sample_kernel(x_ref, mod_ref, wq_ref, wg_ref, q_ref, g_ref):
    sb, tm, d = x_ref.shape
    x = x_ref[...]
    shift = mod_ref[:, 0:1, :]
    scale = mod_ref[:, 1:2, :]
    ub = (x * (1.0 + scale) + shift).reshape(sb * tm, d).astype(BF16)
    q = jnp.dot(ub, wq_ref[...], preferred_element_type=F32) * Q_SCALE
    q_ref[...] = q.reshape(sb, tm, -1)
    for i in range(3):
        gl = jnp.dot(ub, wg_ref[i], preferred_element_type=F32)
        g_ref[:, i] = jax.nn.sigmoid(gl).reshape(sb, tm, -1)


def _qproj_sample(x, mod, wq, wgx, *, sb):
    bsz, t, d = x.shape
    hq = wq.shape[1]
    return pl.pallas_call(
        _qproj_sample_kernel,
        grid=(bsz // sb,),
        in_specs=[pl.BlockSpec((sb, t, d), lambda i: (i, 0, 0)),
                  pl.BlockSpec((sb, 6, d), lambda i: (i, 0, 0)),
                  _const_spec(wq.shape), _const_spec(wgx.shape)],
        out_specs=[pl.BlockSpec((sb, t, hq), lambda i: (i, 0, 0)),
                   pl.BlockSpec((sb, 3, t, hq), lambda i: (i, 0, 0, 0))],
        out_shape=[jax.ShapeDtypeStruct((bsz, t, hq), F32), jax.ShapeDtypeStruct((bsz, 3, t, hq), F32)],
        compiler_params=_params("parallel"),
        name="q_proj_sample",
    )(x, mod, wq, wgx)


def _topk_rows(work, n_sel):
    rid = lax.broadcasted_iota(jnp.int32, work.shape, 1).astype(F32)

    def body(_, wk):
        m = jnp.max(wk, axis=1, keepdims=True)
        idx = jnp.min(jnp.where(wk == m, rid, 1e9), axis=1, keepdims=True)
        return jnp.where(rid == idx, -jnp.inf, wk)

    return lax.fori_loop(0, n_sel, body, work) == -jnp.inf


def _pipelined(n, issue, finish, ahead=2):
    pending = [issue(k) for k in range(min(ahead, n))]
    for k in range(n):
        if k + ahead < n:
            pending.append(issue(k + ahead))
        finish(k, pending.pop(0))


def _attn_prompt_kernel(qt_ref, gt_ref, kc_ref, vct_ref, ovt_ref, eb_ref, ks_ref, vst_ref, kw_ref, vwt_ref,
                        o_ref, selb_ref, oc_ref, ow_ref, acc_ref, m_ref, tmp_ref, cmax_ref):
    i = pl.program_id(1)
    qb = Q_BLOCK
    hd = HEAD_DIM
    s0 = i * qb
    nc = kc_ref.shape[1]
    ns = ovt_ref.shape[0]
    nq4 = GROUP * qb
    nblk = KV_CHUNK // SLC_LEN
    nvb = KV_CHUNK // qb
    zeros = jnp.zeros((hd, nq4), BF16)
    ones = jnp.ones((16, 1), BF16)
    qpos1 = s0 + lax.broadcasted_iota(jnp.int32, (1, qb), 1)
    win_lo = jnp.maximum(i - WINDOW // qb, 0)
    n_wb = WINDOW // qb + 1
    nwk = n_wb * qb

    def tile4(a):
        return jnp.concatenate([a] * GROUP, axis=1)

    def pair(g):
        return slice((g // 2) * 128, (g // 2 + 1) * 128)

    def grow(g):
        return slice(g * hd, (g + 1) * hd)

    def qpad(g):
        qg = jnp.concatenate([qt_ref[0, (g * GROUP + r) * hd:(g * GROUP + r + 1) * hd, :] for r in range(GROUP)],
                             axis=1)
        return jnp.concatenate([qg, zeros] if g % 2 == 0 else [zeros, qg], axis=0)

    def with_ones(vt):
        return jnp.concatenate([vt, jnp.broadcast_to(ones, (16, vt.shape[1]))], axis=0)

    cend = lax.broadcasted_iota(jnp.int32, (nc, qb), 0) * CMP_STRIDE + (CMP_LEN - 1)
    bias_c = tile4(jnp.where(cend <= qpos1, 0.0, NEG))
    seen_c = tile4((qpos1 >= CMP_LEN - 1).astype(F32))
    jj = lax.broadcasted_iota(jnp.int32, (ns, qb), 0)
    cur = jnp.right_shift(qpos1, SLC_SHIFT)
    forced = (jj == 0) | (jj == cur) | (jj == cur - 1)
    valid = jj * SLC_LEN <= qpos1
    ovt = ovt_ref[...]
    imps = [None] * N_KV
    w0 = pl.multiple_of(win_lo * qb, qb)
    dist = qpos1 - (w0 + lax.broadcasted_iota(jnp.int32, (nwk, qb), 0))
    bias_w = tile4(jnp.where((dist >= 0) & (dist < WINDOW), 0.0, NEG))

    def issue_cw(k):
        g = k % N_KV
        if k < N_KV:
            return jnp.dot(kc_ref[0, :, pair(g)], qpad(g), preferred_element_type=F32) + bias_c
        return jnp.dot(kw_ref[0, pl.ds(w0, nwk), pair(g)], qpad(g), preferred_element_type=F32) + bias_w

    def finish_cw(k, st):
        g = k % N_KV
        if k < N_KV:
            p = jnp.exp2(st - jnp.max(st, axis=0, keepdims=True))
            p = p * (seen_c / jnp.sum(p, axis=0, keepdims=True))
            vct = jnp.concatenate([vct_ref[0, t, grow(g), :] for t in range(vct_ref.shape[1])], axis=1)
            oc_ref[g] = jnp.dot(vct, p.astype(BF16), preferred_element_type=F32)
            psum = p[:, 0:qb]
            for r in range(1, GROUP):
                psum = psum + p[:, r * qb:(r + 1) * qb]
            imp = sum(jnp.dot(ovt, part, preferred_element_type=F32) for part in _split3(psum))
            imps[g] = jnp.where(valid, imp + jnp.where(forced, FORCE_BONUS, 0.0), NEG)
        else:
            pw = jnp.exp2(st - jnp.max(st, axis=0, keepdims=True))
            vwt = jnp.concatenate([vwt_ref[0, win_lo + t, grow(g), :] for t in range(n_wb)], axis=1)
            aw = jnp.dot(with_ones(vwt), pw.astype(BF16), preferred_element_type=F32)
            ow_ref[g] = aw[:hd] * (1.0 / aw[hd:hd + 1])

    _pipelined(2 * N_KV, issue_cw, finish_cw)

    sel = _topk_rows(jnp.stack(imps), min(TOP_N, ns))
    selb_ref[...] = jnp.where(sel, 0.0, NEG)

    m_ref[...] = jnp.zeros(m_ref.shape, F32)
    acc_ref[...] = jnp.zeros(acc_ref.shape, F32)
    zpad = jnp.zeros((128 - 16, nq4), BF16)

    def scores(c, g, k0, cb):
        kaug = jnp.concatenate([ks_ref[0, pl.ds(k0, KV_CHUNK), pair(g)], eb_ref[...]], axis=1)
        sb = tile4(selb_ref[g, pl.ds(pl.multiple_of(c * nblk, nblk), nblk), :])
        sbp = jnp.concatenate([sb, -m_ref[g], jnp.zeros((15 - nblk, nq4), F32)], axis=0).astype(BF16)
        qaug = jnp.concatenate([qpad(g), sbp, zpad], axis=0)
        sc = jnp.dot(kaug, qaug, preferred_element_type=F32)
        return sc if cb is None else sc + cb

    def vt_ext(c, g):
        return with_ones(jnp.concatenate([vst_ref[0, c * nvb + t, grow(g), :] for t in range(nvb)], axis=1))

    def next_ref(c, r_old, cmax):
        floor = jnp.where(c == 0, -jnp.inf, 0.0)
        return (r_old + jnp.maximum(cmax, floor)).astype(BF16).astype(F32)

    def chunk(c, causal):
        k0 = pl.multiple_of(c * KV_CHUNK, KV_CHUNK)
        cb = None
        if causal:
            kpos = k0 + lax.broadcasted_iota(jnp.int32, (KV_CHUNK, qb), 0)
            cb = tile4(jnp.where(kpos <= qpos1, 0.0, NEG))

        def finish(g, sc):
            cmax_ref[g] = jnp.max(sc, axis=0, keepdims=True)
            tmp_ref[g] = jnp.dot(vt_ext(c, g), jnp.exp2(sc).astype(BF16), preferred_element_type=F32)

        _pipelined(N_KV, lambda g: scores(c, g, k0, cb), finish)
        cm = cmax_ref[...]
        in_range = (jnp.max(cm) <= EXP_HEADROOM) & ((c > 0) | (jnp.min(cm) >= -EXP_HEADROOM))

        @pl.when(in_range)
        def _():
            for g in range(N_KV):
                r_old = m_ref[g]
                r_new = next_ref(c, r_old, cmax_ref[g])
                acc_ref[g] = (acc_ref[g] + tmp_ref[g]) * jnp.exp2(r_old - r_new)
                m_ref[g] = r_new

        @pl.when(jnp.logical_not(in_range))
        def _():
            for g in range(N_KV):
                sc = scores(c, g, k0, cb)
                r_old = m_ref[g]
                r_new = next_ref(c, r_old, jnp.max(sc, axis=0, keepdims=True))
                delta = r_new - r_old
                pc = jnp.exp2(sc - delta)
                keep = jnp.exp2(-jnp.maximum(delta, 0.0))
                acc_ref[g] = keep * acc_ref[g] + jnp.dot(vt_ext(c, g), pc.astype(BF16), preferred_element_type=F32)
                m_ref[g] = r_new

    n_full = s0 // KV_CHUNK

    def body(c, carry):
        chunk(c, False)
        return carry

    lax.fori_loop(0, n_full, body, 0)
    chunk(n_full, True)

    heads_out = []
    for g in range(N_KV):
        acc = acc_ref[g]
        o_s = acc[:hd] * (1.0 / acc[hd:hd + 1])
        o_c = oc_ref[g]
        o_w = ow_ref[g]
        for r in range(GROUP):
            h = g * GROUP + r
            ls = slice(r * qb, (r + 1) * qb)
            heads_out.append(gt_ref[0, 3 * h:3 * h + 1, :] * o_c[:, ls]
                             + gt_ref[0, 3 * h + 1:3 * h + 2, :] * o_s[:, ls]
                             + gt_ref[0, 3 * h + 2:3 * h + 3, :] * o_w[:, ls])
    out_t = jnp.concatenate(heads_out, axis=0)
    o_ref[0] = out_t.T.astype(BF16)


def _attn_prompt(qt, gt, kc, vct, ovt, eb, ks, vst, kw, vwt):
    bsz, hq, t = qt.shape
    ng = gt.shape[1]
    qb = Q_BLOCK
    nq4 = GROUP * qb

    def per_b(shape):
        nd = len(shape)
        return pl.BlockSpec((1,) + tuple(shape[1:]), lambda b, i: (b,) + (0,) * (nd - 1),
                            pipeline_mode=pl.Buffered(1))

    return pl.pallas_call(
        _attn_prompt_kernel,
        grid=(bsz, t // qb),
        in_specs=[pl.BlockSpec((1, hq, qb), lambda b, i: (b, 0, i)),
                  pl.BlockSpec((1, ng, qb), lambda b, i: (b, 0, i)),
                  per_b(kc.shape), per_b(vct.shape), _const_spec(ovt.shape), _const_spec(eb.shape),
                  per_b(ks.shape), per_b(vst.shape), per_b(kw.shape), per_b(vwt.shape)],
        out_specs=pl.BlockSpec((1, qb, hq), lambda b, i: (b, i, 0)),
        out_shape=jax.ShapeDtypeStruct((bsz, t, hq), BF16),
        scratch_shapes=[pltpu.VMEM((N_KV, t // SLC_LEN, qb), F32),
                        pltpu.VMEM((N_KV, HEAD_DIM, nq4), F32),
                        pltpu.VMEM((N_KV, HEAD_DIM, nq4), F32),
                        pltpu.VMEM((N_KV, HEAD_DIM + 16, nq4), F32),
                        pltpu.VMEM((N_KV, 1, nq4), F32),
                        pltpu.VMEM((N_KV, HEAD_DIM + 16, nq4), F32),
                        pltpu.VMEM((N_KV, 1, nq4), F32)],
        compiler_params=_params("parallel", "arbitrary"),
        name="nsa_prompt",
    )(qt, gt, kc, vct, ovt, eb, ks, vst, kw, vwt)


def _topk_cols(work, n_sel):
    rid = lax.broadcasted_iota(jnp.int32, work.shape, 0).astype(F32)
    for _ in range(n_sel):
        m = jnp.max(work, axis=0, keepdims=True)
        idx = jnp.min(jnp.where(work == m, rid, 1e9), axis=0, keepdims=True)
        work = jnp.where(rid == idx, -jnp.inf, work)
    return jnp.where(work == -jnp.inf, 1.0, 0.0)


def _softmax_lanes(s):
    p = jnp.exp2(s - jnp.max(s, axis=1, keepdims=True))
    return p, jnp.sum(p, axis=1, keepdims=True)


def _heads_to_lanes(o, tq):
    hd = HEAD_DIM
    nkv = N_KV * hd
    lane = lax.broadcasted_iota(jnp.int32, (tq, nkv), 1)
    pieces = []
    for h in range(N_HEADS):
        g = h // GROUP
        blk = o[h * tq:(h + 1) * tq, :]
        blk = jnp.where((lane >= g * hd) & (lane < (g + 1) * hd), blk, 0.0)
        dst = (h * hd) % nkv
        shift = (dst - g * hd) % nkv
        pieces.append(pltpu.roll(blk, shift, axis=1) if shift else blk)
    cols = []
    per = nkv // hd
    for c in range(N_HEADS // per):
        slab = pieces[c * per]
        for k in range(1, per):
            slab = slab + pieces[c * per + k]
        cols.append(slab)
    return jnp.concatenate(cols, axis=1)


def _attn_sample_kernel(pt_ref, *refs, past_len, n_ns, spb):
    del pt_ref
    npg = past_len // PAGE
    page_refs = refs[:npg * spb]
    (q_ref, g_ref, kc_ref, vc_ref, kvn_ref, win_ref, ovt_ref, ex_ref,
     o_ref, kbuf, vbuf, kwbuf, vwbuf) = refs[npg * spb:]
    hd = HEAD_DIM
    nkv = N_KV * hd
    tq = q_ref.shape[1]
    nrow = N_HEADS * tq
    ngt = N_KV * tq
    nwin = win_ref.shape[2]
    nks = kbuf.shape[2]
    nkw = kwbuf.shape[2]
    nc = kc_ref.shape[1]
    seqs = range(spb)

    ztail = jnp.zeros((PAGE - tq, nkv), F32)
    eye = (lax.broadcasted_iota(jnp.int32, (nkv, nkv), 0)
           == lax.broadcasted_iota(jnp.int32, (nkv, nkv), 1)).astype(F32).astype(BF16)
    for s in seqs:
        for k in range(npg):
            pg = page_refs[s * npg + k]
            kbuf[s, :, k * PAGE:(k + 1) * PAGE] = pg[0:nkv, :].astype(BF16)
            vbuf[s, :, k * PAGE:(k + 1) * PAGE] = pg[nkv:2 * nkv, :].astype(BF16)
        kwbuf[s, :, 0:nwin] = win_ref[s, 0:nkv, :].astype(BF16)
        vwbuf[s, :, 0:nwin] = win_ref[s, nkv:2 * nkv, :].astype(BF16)
        kvn = kvn_ref[s]

        def new_t(slot, kvn=kvn):
            rows = jnp.concatenate([kvn[:, slot * nkv:(slot + 1) * nkv], ztail], axis=0).astype(BF16)
            return lax.dot_general(eye, rows, _NT, preferred_element_type=F32).astype(BF16)

        kbuf[s, :, past_len:past_len + PAGE] = new_t(2)
        vbuf[s, :, past_len:past_len + PAGE] = new_t(3)
        kwbuf[s, :, nwin:nwin + PAGE] = new_t(4)
        vwbuf[s, :, nwin:nwin + PAGE] = new_t(5)

    lane = lax.broadcasted_iota(jnp.int32, (tq, nkv), 1)
    qbd = []
    for s in seqs:
        q = q_ref[s]
        qrows = []
        for h in range(N_HEADS):
            g, r = divmod(h, GROUP)
            slab = q[:, g * nkv:(g + 1) * nkv]
            shift = ((g - r) * hd) % nkv
            moved = pltpu.roll(slab, shift, axis=1) if shift else slab
            qrows.append(jnp.where((lane >= g * hd) & (lane < (g + 1) * hd), moved, 0.0))
        qbd.append(jnp.concatenate(qrows, axis=0).astype(BF16))

    tpos = jnp.bitwise_and(lax.broadcasted_iota(jnp.int32, (nrow, 1), 0), tq - 1)
    qpos = past_len + tpos

    s_c = [lax.dot_general(qbd[s], kc_ref[s], _NT, preferred_element_type=F32) for s in seqs]
    s_w = [jnp.dot(qbd[s], kwbuf[s], preferred_element_type=F32) for s in seqs]
    s_s = [jnp.dot(qbd[s], kbuf[s], preferred_element_type=F32) for s in seqs]

    cend = lax.broadcasted_iota(jnp.int32, (nrow, nc), 1) * CMP_STRIDE + (CMP_LEN - 1)
    mc = cend <= qpos
    nsr = -(-n_ns // 8) * 8
    jj = lax.broadcasted_iota(jnp.int32, (nsr, ngt), 0)
    qp2 = past_len + jnp.bitwise_and(lax.broadcasted_iota(jnp.int32, (1, ngt), 1), tq - 1)
    cur = jnp.right_shift(qp2, SLC_SHIFT)
    forced = (jj == 0) | (jj == cur) | (jj == cur - 1)
    valid = (jj * SLC_LEN <= qp2) & (jj < n_ns)
    o_c, imp = [], []
    for s in seqs:
        p_c, l_c = _softmax_lanes(jnp.where(mc, s_c[s], NEG))
        p_c = jnp.where(mc, p_c * (1.0 / l_c), 0.0)
        o_c.append(jnp.dot(p_c.astype(BF16), vc_ref[s], preferred_element_type=F32))
        p4 = p_c.reshape(N_KV, GROUP, tq, nc)
        psum = p4[:, 0]
        for r in range(1, GROUP):
            psum = psum + p4[:, r]
        psum = psum.reshape(ngt, nc)
        it = sum(lax.dot_general(ovt_ref[...], part, _NT, preferred_element_type=F32)
                 for part in _split3(psum))
        imp.append(jnp.where(valid, it[:nsr] + jnp.where(forced, FORCE_BONUS, 0.0), NEG))

    jw = lax.broadcasted_iota(jnp.int32, (nrow, nkw), 1)
    kwpos = past_len + tq - (nwin + tq) + jw
    dist = qpos - kwpos
    mw = (dist >= 0) & (dist < WINDOW) & (kwpos >= 0) & (jw < nwin + tq)
    o_w = []
    for s in seqs:
        p_w, l_w = _softmax_lanes(jnp.where(mw, s_w[s], NEG))
        o_w.append(lax.dot_general(p_w.astype(BF16), vwbuf[s], _NT, preferred_element_type=F32) * (1.0 / l_w))

    nblk = ex_ref.shape[0]
    selk = []
    for s in seqs:
        sel_t = _topk_cols(imp[s], min(TOP_N, n_ns))
        sel_t = jnp.concatenate([sel_t, jnp.zeros((nblk - nsr, ngt), F32)], axis=0)
        sel_t = jnp.concatenate([sel_t, jnp.zeros((nblk, nblk - ngt), F32)], axis=1)
        sel = sel_t.T[:ngt]
        sk = jnp.dot(sel.astype(BF16), ex_ref[...], preferred_element_type=F32)
        selk.append(jnp.broadcast_to(sk.reshape(N_KV, 1, tq, nks), (N_KV, GROUP, tq, nks)).reshape(nrow, nks))

    kpos = lax.broadcasted_iota(jnp.int32, (nrow, nks), 1)
    causal = kpos <= qpos
    o_s = []
    for s in seqs:
        p_s, l_s = _softmax_lanes(jnp.where((selk[s] > 0.5) & causal, s_s[s], NEG))
        o_s.append(lax.dot_general(p_s.astype(BF16), vbuf[s], _NT, preferred_element_type=F32) * (1.0 / l_s))

    for s in seqs:
        o_ref[s] = (g_ref[s, 0] * _heads_to_lanes(o_c[s], tq) + g_ref[s, 1] * _heads_to_lanes(o_s[s], tq)
                    + g_ref[s, 2] * _heads_to_lanes(o_w[s], tq))


def _attn_sample(pages, page_table, q, gx, kc, vc, kvn, win, ovt, ex, *, past_len, spb):
    bsz, tq, hq = q.shape
    npg = past_len // PAGE
    nkv = N_KV * HEAD_DIM
    nwin = win.shape[2]
    n_ns = -(-(past_len + tq) // SLC_LEN)
    nks = past_len + PAGE
    nkw = nwin + PAGE
    kern = functools.partial(_attn_sample_kernel, past_len=past_len, n_ns=n_ns, spb=spb)

    def page_spec(s, k):
        return pl.BlockSpec((None, 2 * nkv, PAGE), lambda b, pt: (pt[b * spb + s, k], 1, 0))

    def cs(shape):
        nd = len(shape)
        return pl.BlockSpec(shape, lambda b, pt: (0,) * nd, pipeline_mode=pl.Buffered(1))

    def per_b(shape):
        nd = len(shape)
        return pl.BlockSpec((spb,) + tuple(shape[1:]), lambda b, pt: (b,) + (0,) * (nd - 1))

    in_specs = [page_spec(s, k) for s in range(spb) for k in range(npg)]
    in_specs += [per_b(q.shape), per_b(gx.shape), per_b(kc.shape), per_b(vc.shape), per_b(kvn.shape),
                 per_b(win.shape), cs(ovt.shape), cs(ex.shape)]
    grid_spec = pltpu.PrefetchScalarGridSpec(
        num_scalar_prefetch=1,
        grid=(bsz // spb,),
        in_specs=in_specs,
        out_specs=pl.BlockSpec((spb, tq, hq), lambda b, pt: (b, 0, 0)),
        scratch_shapes=[pltpu.VMEM((spb, nkv, nks), BF16), pltpu.VMEM((spb, nkv, nks), BF16),
                        pltpu.VMEM((spb, nkv, nkw), BF16), pltpu.VMEM((spb, nkv, nkw), BF16)],
    )
    return pl.pallas_call(
        kern,
        grid_spec=grid_spec,
        out_shape=jax.ShapeDtypeStruct((bsz, tq, hq), F32),
        compiler_params=_params("arbitrary"),
        name="nsa_sample",
    )(page_table, *([pages] * (npg * spb)), q, gx, kc, vc, kvn, win, ovt, ex)


def _oproj_kernel(o_ref, x_ref, mod_ref, wo_ref, g_ref, b_ref, out_ref, *, alpha):
    sb, tm, d = x_ref.shape
    ob = o_ref[...].reshape(sb * tm, -1).astype(BF16)
    mix = jnp.dot(ob, wo_ref[...], preferred_element_type=F32).reshape(sb, tm, d)
    gate = mod_ref[:, 2:3, :]
    y = alpha * x_ref[...] + (1.0 + gate) * mix
    out_ref[...] = _layer_norm(y, g_ref[...], b_ref[...])


def _oproj(o, x, mod, wo, g, b, *, sb, tm, alpha):
    bsz, t, d = x.shape
    hq = o.shape[-1]
    kern = functools.partial(_oproj_kernel, alpha=alpha)
    return pl.pallas_call(
        kern,
        grid=(bsz // sb, t // tm),
        in_specs=[pl.BlockSpec((sb, tm, hq), lambda i, j: (i, j, 0)),
                  pl.BlockSpec((sb, tm, d), lambda i, j: (i, j, 0)),
                  pl.BlockSpec((sb, 6, d), lambda i, j: (i, 0, 0)),
                  _const_spec(wo.shape), _const_spec(g.shape), _const_spec(b.shape)],
        out_specs=pl.BlockSpec((sb, tm, d), lambda i, j: (i, j, 0)),
        out_shape=jax.ShapeDtypeStruct((bsz, t, d), F32),
        compiler_params=_params("parallel", "parallel"),
        name="o_proj",
    )(o, x, mod, wo, g, b)


def _overlap(n_c, n_s):
    ci = np.arange(n_c)[:, None] * CMP_STRIDE
    sj = np.arange(n_s)[None, :] * SLC_LEN
    return ((ci <= sj + SLC_LEN - 1) & (ci + CMP_LEN - 1 >= sj)).astype(np.float32)


def kernel(x_prompt, x_sample, cache_kv, state_kv_win, state_pool, state_conv, page_table, c_prompt, c_sample, ada_w, ada_b, ln_g, ln_b, pool_w, pool_ls, ffn_w_up, ffn_b_up, ffn_w_conv, ffn_b_conv, ffn_w_down, w_kv, cmp_pe, cmp_w1, cmp_b1, cmp_w2, cmp_b2, nsa_w_qg, nsa_w_o):
    bp, t, d = x_prompt.shape
    bs, ts, _ = x_sample.shape
    depth = ada_w.shape[0]
    n_a = pool_w.shape[0]
    f2 = ffn_w_up.shape[-1]
    f = f2 // 2
    past_len = page_table.shape[1] * PAGE
    nkv = N_KV * HEAD_DIM
    hq = N_HEADS * HEAD_DIM
    alpha = float((2 * depth) ** 0.25)
    assert t % (16 * PAGE) == 0 and t >= WINDOW + Q_BLOCK
    assert ts == 8 and past_len == 16 * PAGE and state_kv_win.shape[1] == WINDOW
    assert f % 1408 == 0 or f % 256 == 0

    tm_p = 512
    sb_pool = min(bs, 64)
    sb_ffn = min(bs, 16)
    cf = 1408 if f % 1408 == 0 else 256

    ada = _ada(jnp.concatenate([c_prompt, c_sample], axis=0), ada_w, ada_b)
    ada = ada.reshape(depth, bp + bs, 6, d)
    mod_p = [ada[l, :bp] for l in range(depth)]
    mod_s = [ada[l, bp:] for l in range(depth)]

    pool_wb = pool_w.astype(BF16)
    wup_b = ffn_w_up.astype(BF16)
    wdn_b = ffn_w_down.astype(BF16)
    w_kv_b = w_kv.astype(BF16)
    w_kvt_b = w_kv.T.astype(BF16)
    wq_b = nsa_w_qg[:, :, :hq].astype(BF16)
    wqt_b = jnp.swapaxes(nsa_w_qg[:, :, :hq], 1, 2).astype(BF16)
    wgt_b = jnp.swapaxes(nsa_w_qg[:, :, hq:], 1, 2).astype(BF16)
    wg = nsa_w_qg[:, :, hq:].reshape(-1, d, N_HEADS, 3)
    wgx_b = jnp.repeat(jnp.moveaxis(wg, 3, 1), HEAD_DIM, axis=3).astype(BF16)
    wo_b = nsa_w_o.astype(BF16)

    def vec(a):
        return a.reshape(1, -1)

    pool0 = jnp.zeros((bp, POOL_HALO, d), F32)
    conv0 = jnp.zeros((bp, CONV_HALO, f2), F32)
    pool_prev_s = jnp.pad(state_pool, ((0, 0), (0, 0), (POOL_HALO - state_pool.shape[2], 0), (0, 0)))
    conv_prev_s = jnp.pad(state_conv, ((0, 0), (0, 0), (CONV_HALO - state_conv.shape[2], 0), (0, 0)))

    xp, xs = x_prompt, x_sample
    pool_p, pool_s, conv_p, conv_s = [], [], [], []
    ctx_p = ctx_s = None
    kv_p = kv_s = None

    for l in range(depth):
        g1, b1, g2, b2 = vec(ln_g[l, 0]), vec(ln_b[l, 0]), vec(ln_g[l, 1]), vec(ln_b[l, 1])
        if l < n_a:
            xp, st = _pool_layer(xp, mod_p[l], pool0, pool_wb[l], vec(pool_ls[l]), g1, b1,
                                 sb=1, tm=tm_p, start_pos=0, alpha=alpha)
            pool_p.append(st[:, 1:])
            xs, st = _pool_layer(xs, mod_s[l], pool_prev_s[l], pool_wb[l], vec(pool_ls[l]), g1, b1,
                                 sb=sb_pool, tm=ts, start_pos=past_len, alpha=alpha)
            pool_s.append(st[:, 1:])
        else:
            jn = l - n_a
            qt, gt = _qproj_prompt(xp, mod_p[l], wqt_b[jn], wgt_b[jn], tm=tm_p)
            op = _attn_prompt(qt, gt, *ctx_p)
            xp = _oproj(op, xp, mod_p[l], wo_b[jn], g1, b1, sb=1, tm=tm_p, alpha=alpha)
            qs, gs = _qproj_sample(xs, mod_s[l], wq_b[jn], wgx_b[jn], sb=sb_pool)
            os_ = _attn_sample(ctx_s[0], page_table, qs, gs, *ctx_s[1:], past_len=past_len,
                               spb=2 if bs % 2 == 0 else 1)
            xs = _oproj(os_, xs, mod_s[l], wo_b[jn], g1, b1, sb=sb_pool, tm=ts, alpha=alpha)

        xp, st = _ffn_layer(xp, mod_p[l], conv0, wup_b[l], vec(ffn_b_up[l]), ffn_w_conv[l], vec(ffn_b_conv[l]),
                            wdn_b[l], g2, b2, sb=1, tm=tm_p, cf=cf, alpha=alpha)
        conv_p.append(st[:, CONV_HALO - (CONV_W - 1):])
        xs, st = _ffn_layer(xs, mod_s[l], conv_prev_s[l], wup_b[l], vec(ffn_b_up[l]), ffn_w_conv[l],
                            vec(ffn_b_conv[l]), wdn_b[l], g2, b2, sb=sb_ffn, tm=ts, cf=cf, alpha=alpha)
        conv_s.append(st[:, CONV_HALO - (CONV_W - 1):])

        if l == n_a - 1:
            kvpt, kvwt, cmp_rows, ks, kw, vst, vwt = _kvproj_prompt(xp, w_kv_b, w_kvt_b, tm=tm_p)
            kv_p = (kvpt, kvwt)
            tiles = t // (16 * PAGE)
            ppb = t // PAGE
            base = (np.arange(bp)[:, None] * ppb + np.arange(tiles)[None, :] * 16).reshape(-1, 1)
            ids = base + np.arange(17)[None, :]
            last = (np.arange(bp)[:, None] * ppb + ppb - 1).repeat(tiles, axis=1).reshape(-1)
            ids[:, 16] = np.minimum(ids[:, 16], last)
            kc, _, vct = _compress(cmp_rows.reshape(bp * ppb, PAGE, 2 * nkv), jnp.asarray(ids, jnp.int32),
                                   cmp_pe, cmp_w1, cmp_b1, cmp_w2, cmp_b2, transposed=False)
            n_c = tiles * 128
            ovt = jnp.asarray(_overlap(n_c, t // SLC_LEN).T, BF16)
            eb = (np.arange(KV_CHUNK)[:, None] // SLC_LEN == np.arange(128)[None, :]).astype(np.float32)
            eb[:, KV_CHUNK // SLC_LEN] = 1.0
            ctx_p = (kc.reshape(bp, n_c, nkv), vct.reshape(bp, tiles, nkv, 128), ovt, jnp.asarray(eb, BF16),
                     ks, vst, kw, vwt)

            kvs = _kvproj_sample(xs, w_kv_b, sb=sb_pool)
            kv_s = kvs
            assert (past_len + ts - CMP_LEN) // CMP_STRIDE + 1 == past_len // CMP_STRIDE - 1
            pages_t = jnp.transpose(cache_kv, (0, 2, 3, 4, 1)).reshape(cache_kv.shape[0], 4 * nkv, PAGE)
            kc_s, vc_s, _ = _compress(pages_t, page_table.astype(jnp.int32), cmp_pe, cmp_w1, cmp_b1, cmp_w2, cmp_b2,
                                      transposed=True)
            n_ns = -(-(past_len + ts) // SLC_LEN)
            ov_s = np.zeros((128, 128), np.float32)
            n_cs = (past_len + ts - CMP_LEN) // CMP_STRIDE + 1
            ov_s[:n_cs, :n_ns] = _overlap(n_cs, n_ns)
            nks = past_len + PAGE
            ex = (np.arange(nks)[None, :] // SLC_LEN == np.arange(128)[:, None]).astype(np.float32)
            win_t = jnp.transpose(state_kv_win, (0, 2, 3, 4, 1)).reshape(bs, 2 * nkv, state_kv_win.shape[1])
            ctx_s = (pages_t, kc_s, vc_s, kvs, win_t, jnp.asarray(ov_s.T, BF16), jnp.asarray(ex, BF16))

    kvpt, kvwt = kv_p
    wlen = min(WINDOW, t)
    wbuf = state_kv_win.shape[1]
    win_s = jnp.concatenate([state_kv_win, kv_s[:, :, 4 * nkv:].reshape(bs, ts, 2, N_KV, HEAD_DIM)], axis=1)[:, -wbuf:]
    kv_prompt = jnp.transpose(kvpt.reshape(bp, 4, N_KV, HEAD_DIM, t), (0, 4, 1, 2, 3))
    win_prompt = jnp.transpose(kvwt[:, :, t - wlen:].reshape(bp, 2, N_KV, HEAD_DIM, wlen), (0, 4, 1, 2, 3))
    return (xp, xs,
            kv_prompt,
            kv_s[:, :, :4 * nkv].reshape(bs, ts, 4, N_KV, HEAD_DIM),
            win_prompt,
            win_s,
            jnp.stack(pool_p), jnp.stack(pool_s), jnp.stack(conv_p), jnp.stack(conv_s))
```

```python
import functools

import numpy as np
import jax
import jax.numpy as jnp
from jax import lax
from jax.experimental import pallas as pl
from jax.experimental.pallas import tpu as pltpu

F32 = jnp.float32
BF16 = jnp.bfloat16

POOL_WINDOWS = (2, 4, 8, 16)
POOL_HALO = 16
N_HEADS = 16
HEAD_DIM = 64
N_KV = 4
GROUP = N_HEADS // N_KV
CMP_LEN = 32
CMP_STRIDE = 16
SLC_LEN = 64
SLC_SHIFT = 6
TOP_N = 16
WINDOW = 512
Q_BLOCK = 128
CONV_W = 3
CONV_HALO = 8
PAGE = 128
LN_EPS = 1e-5
NEG = -1e30
N_FORCED = 3
KV_CHUNK = 512
Q_SCALE = HEAD_DIM ** -0.5 * 1.4426950408889634
EXP_HEADROOM = 100.0
VMEM_LIMIT = 56 * 1024 * 1024

_NT = (((1,), (1,)), ((), ()))


def _params(*sem):
    return pltpu.CompilerParams(dimension_semantics=sem, vmem_limit_bytes=VMEM_LIMIT)


def _const_spec(shape):
    nd = len(shape)
    return pl.BlockSpec(shape, lambda *_: (0,) * nd, pipeline_mode=pl.Buffered(1))


def _mod_spec(mod, sb):
    ada, layer, row0 = mod
    assert row0 % sb == 0
    return pl.BlockSpec((None, sb, 6, ada.shape[-1]), lambda i, *_: (layer, row0 // sb + i, 0, 0))


def _layer_norm(y, g, b):
    mu = jnp.mean(y, axis=-1, keepdims=True)
    d = y - mu
    var = jnp.mean(d * d, axis=-1, keepdims=True)
    return d * lax.rsqrt(var + LN_EPS) * g + b


def _split3(x):
    hi = x.astype(BF16)
    r1 = x - hi.astype(F32)
    mid = r1.astype(BF16)
    lo = (r1 - mid.astype(F32)).astype(BF16)
    return hi, mid, lo


def _ada_kernel(c_ref, w_ref, b_ref, o_ref):
    c = c_ref[...]
    s = c * jax.nn.sigmoid(c)
    o_ref[0] = jnp.dot(s.astype(BF16), w_ref[0].astype(BF16), preferred_element_type=F32) + b_ref[0]


def _ada(c_all, ada_w, ada_b):
    depth, d, n = ada_w.shape
    r = c_all.shape[0]
    tn = n // 4
    return pl.pallas_call(
        _ada_kernel,
        grid=(depth, n // tn),
        in_specs=[pl.BlockSpec((r, d), lambda l, j: (0, 0)),
                  pl.BlockSpec((1, d, tn), lambda l, j: (l, 0, j)),
                  pl.BlockSpec((1, 1, tn), lambda l, j: (l, 0, j))],
        out_specs=pl.BlockSpec((1, r, tn), lambda l, j: (l, 0, j)),
        out_shape=jax.ShapeDtypeStruct((depth, r, n), F32),
        compiler_params=_params("parallel", "parallel"),
        name="ada_params",
    )(c_all, ada_w, ada_b.reshape(depth, 1, n))


def _pool_kernel(x_ref, mod_ref, prev_ref, pw_ref, ls_ref, g_ref, b_ref, o_ref, ps_ref, ext_ref,
                 *, start_pos, nj, alpha):
    j = pl.program_id(1)
    sb, tm, d = x_ref.shape
    pg = d // len(POOL_WINDOWS)
    h = POOL_HALO
    x = x_ref[...]
    shift = mod_ref[:, 0:1, :]
    scale = mod_ref[:, 1:2, :]
    gate = mod_ref[:, 2:3, :]
    u = x * (1.0 + scale) + shift

    @pl.when(j == 0)
    def _():
        ext_ref[:, 0:h, :] = prev_ref[...]

    ext_ref[:, h:h + tm, :] = u
    pos = start_pos + j * tm + lax.broadcasted_iota(jnp.int32, (1, tm, pg), 1)
    outs = []
    for gi, w in enumerate(POOL_WINDOWS):
        c0 = gi * pg
        acc = ext_ref[:, h:h + tm, c0:c0 + pg]
        for k in range(1, w):
            acc = acc + ext_ref[:, h - k:h - k + tm, c0:c0 + pg]
        cnt = jnp.minimum(pos + 1, w).astype(F32)
        pooled = acc / cnt - u[:, :, c0:c0 + pg]
        outs.append(jnp.dot(pooled.reshape(sb * tm, pg).astype(BF16), pw_ref[gi],
                            preferred_element_type=F32))
    mixed = jnp.concatenate(outs, axis=-1) * ls_ref[...]
    y = alpha * x + (1.0 + gate) * mixed.reshape(sb, tm, d)
    o_ref[...] = _layer_norm(y, g_ref[...], b_ref[...])

    if nj > 1:
        @pl.when(j < nj - 1)
        def _():
            ext_ref[:, 0:h, :] = ext_ref[:, tm:tm + h, :]

    @pl.when(j == nj - 1)
    def _():
        ps_ref[...] = ext_ref[:, tm:tm + h, :]


def _pool_layer(x, mod, prev, pw, ls, g, b, *, sb, tm, start_pos, alpha):
    bsz, t, d = x.shape
    nj = t // tm
    h = POOL_HALO
    kern = functools.partial(_pool_kernel, start_pos=start_pos, nj=nj, alpha=alpha)
    return pl.pallas_call(
        kern,
        grid=(bsz // sb, nj),
        in_specs=[pl.BlockSpec((sb, tm, d), lambda i, j: (i, j, 0)),
                  _mod_spec(mod, sb),
                  pl.BlockSpec((sb, h, d), lambda i, j: (i, 0, 0)),
                  _const_spec(pw.shape), _const_spec(ls.shape), _const_spec(g.shape), _const_spec(b.shape)],
        out_specs=[pl.BlockSpec((sb, tm, d), lambda i, j: (i, j, 0)),
                   pl.BlockSpec((sb, h, d), lambda i, j: (i, 0, 0))],
        out_shape=[jax.ShapeDtypeStruct((bsz, t, d), F32), jax.ShapeDtypeStruct((bsz, h, d), F32)],
        scratch_shapes=[pltpu.VMEM((sb, h + tm, d), F32)],
        compiler_params=_params("parallel", "arbitrary"),
        name="pool_layer",
    )(x, mod[0], prev, pw, ls, g, b)


def _ffn_kernel(x_ref, mod_ref, cprev_ref, wup_ref, bup_ref, wc_ref, bc_ref, wdn_ref, g_ref, b_ref,
                o_ref, cs_ref, hext_ref, *, cf, alpha):
    j = pl.program_id(1)
    sb, tm, d = x_ref.shape
    f = wdn_ref.shape[0]
    hh = CONV_HALO
    x = x_ref[...]
    shift = mod_ref[:, 3:4, :]
    scale = mod_ref[:, 4:5, :]
    gate = mod_ref[:, 5:6, :]
    ub = (x * (1.0 + scale) + shift).reshape(sb * tm, d).astype(BF16)

    @pl.when(j == 0)
    def _():
        cs_ref[...] = cprev_ref[...]

    acc = jnp.zeros((sb * tm, d), F32)
    for c in range(f // cf):
        parts = []
        for p in range(2):
            c0 = p * f + c * cf
            hp = jnp.dot(ub, wup_ref[:, c0:c0 + cf], preferred_element_type=F32) + bup_ref[:, c0:c0 + cf]
            hext_ref[p, :, 0:hh, :] = cs_ref[:, :, c0:c0 + cf]
            hext_ref[p, :, hh:hh + tm, :] = hp.reshape(sb, tm, cf)
            hc = (bc_ref[:, c0:c0 + cf]
                  + hext_ref[p, :, hh - 2:hh - 2 + tm, :] * wc_ref[0:1, c0:c0 + cf]
                  + hext_ref[p, :, hh - 1:hh - 1 + tm, :] * wc_ref[1:2, c0:c0 + cf]
                  + hext_ref[p, :, hh:hh + tm, :] * wc_ref[2:3, c0:c0 + cf])
            cs_ref[:, :, c0:c0 + cf] = hext_ref[p, :, tm:tm + hh, :]
            parts.append(hc)
        a, v = parts
        gated = (a * jax.nn.sigmoid(a)) * v
        acc = acc + jnp.dot(gated.reshape(sb * tm, cf).astype(BF16), wdn_ref[c * cf:(c + 1) * cf, :],
                            preferred_element_type=F32)
    y = alpha * x + (1.0 + gate) * acc.reshape(sb, tm, d)
    o_ref[...] = _layer_norm(y, g_ref[...], b_ref[...])


def _ffn_layer(x, mod, cprev, wup, bup, wc, bc, wdn, g, b, *, sb, tm, cf, alpha):
    bsz, t, d = x.shape
    f2 = wup.shape[1]
    hh = CONV_HALO
    kern = functools.partial(_ffn_kernel, cf=cf, alpha=alpha)
    return pl.pallas_call(
        kern,
        grid=(bsz // sb, t // tm),
        in_specs=[pl.BlockSpec((sb, tm, d), lambda i, j: (i, j, 0)),
                  _mod_spec(mod, sb),
                  pl.BlockSpec((sb, hh, f2), lambda i, j: (i, 0, 0)),
                  _const_spec(wup.shape), _const_spec(bup.shape), _const_spec(wc.shape),
                  _const_spec(bc.shape), _const_spec(wdn.shape), _const_spec(g.shape), _const_spec(b.shape)],
        out_specs=[pl.BlockSpec((sb, tm, d), lambda i, j: (i, j, 0)),
                   pl.BlockSpec((sb, hh, f2), lambda i, j: (i, 0, 0))],
        out_shape=[jax.ShapeDtypeStruct((bsz, t, d), F32), jax.ShapeDtypeStruct((bsz, hh, f2), F32)],
        scratch_shapes=[pltpu.VMEM((2, sb, hh + tm, cf), F32)],
        compiler_params=_params("parallel", "arbitrary"),
        name="conv_ffn_layer",
    )(x, mod[0], cprev, wup, bup, wc, bc, wdn, g, b)


def _kvproj_prompt_kernel(x_ref, w_ref, wt_ref, kvpt_ref, kvwt_ref, cmp_ref, ks_ref, kw_ref, vst_ref, vwt_ref):
    tm = x_ref.shape[1]
    nkv = N_KV * HEAD_DIM
    xb = x_ref[0].astype(BF16)
    kv = jnp.dot(xb, w_ref[...], preferred_element_type=F32)
    cmp_ref[0] = kv[:, :2 * nkv].astype(BF16)
    ks_ref[0] = kv[:, 2 * nkv:3 * nkv].astype(BF16)
    kw_ref[0] = kv[:, 4 * nkv:5 * nkv].astype(BF16)
    kvt = lax.dot_general(wt_ref[...], xb, _NT, preferred_element_type=F32)
    kvpt_ref[0] = kvt[:4 * nkv]
    kvwt_ref[0] = kvt[4 * nkv:]
    for c in range(tm // Q_BLOCK):
        cols = slice(c * Q_BLOCK, (c + 1) * Q_BLOCK)
        vst_ref[0, c] = kvt[3 * nkv:4 * nkv, cols].astype(BF16)
        vwt_ref[0, c] = kvt[5 * nkv:6 * nkv, cols].astype(BF16)


def _kvproj_prompt(x, w_kv_b, w_kvt_b, *, tm):
    bsz, t, d = x.shape
    nkv = N_KV * HEAD_DIM
    nb = tm // Q_BLOCK
    return pl.pallas_call(
        _kvproj_prompt_kernel,
        grid=(bsz, t // tm),
        in_specs=[pl.BlockSpec((1, tm, d), lambda i, j: (i, j, 0)),
                  _const_spec(w_kv_b.shape), _const_spec(w_kvt_b.shape)],
        out_specs=[pl.BlockSpec((1, 4 * nkv, tm), lambda i, j: (i, 0, j)),
                   pl.BlockSpec((1, 2 * nkv, tm), lambda i, j: (i, 0, j)),
                   pl.BlockSpec((1, tm, 2 * nkv), lambda i, j: (i, j, 0)),
                   pl.BlockSpec((1, tm, nkv), lambda i, j: (i, j, 0)),
                   pl.BlockSpec((1, tm, nkv), lambda i, j: (i, j, 0)),
                   pl.BlockSpec((1, nb, nkv, Q_BLOCK), lambda i, j: (i, j, 0, 0)),
                   pl.BlockSpec((1, nb, nkv, Q_BLOCK), lambda i, j: (i, j, 0, 0))],
        out_shape=[jax.ShapeDtypeStruct((bsz, 4 * nkv, t), F32),
                   jax.ShapeDtypeStruct((bsz, 2 * nkv, t), F32),
                   jax.ShapeDtypeStruct((bsz, t, 2 * nkv), BF16),
                   jax.ShapeDtypeStruct((bsz, t, nkv), BF16),
                   jax.ShapeDtypeStruct((bsz, t, nkv), BF16),
                   jax.ShapeDtypeStruct((bsz, t // Q_BLOCK, nkv, Q_BLOCK), BF16),
                   jax.ShapeDtypeStruct((bsz, t // Q_BLOCK, nkv, Q_BLOCK), BF16)],
        compiler_params=_params("parallel", "parallel"),
        name="kv_proj_prompt",
    )(x, w_kv_b, w_kvt_b)


def _kvproj_sample_kernel(x_ref, w_ref, kv_ref):
    sb, tm, d = x_ref.shape
    xb = x_ref[...].reshape(sb * tm, d).astype(BF16)
    kv_ref[...] = jnp.dot(xb, w_ref[...], preferred_element_type=F32).reshape(sb, tm, -1)


def _kvproj_sample(x, w_kv_b, *, sb):
    bsz, t, d = x.shape
    n = w_kv_b.shape[1]
    return pl.pallas_call(
        _kvproj_sample_kernel,
        grid=(bsz // sb,),
        in_specs=[pl.BlockSpec((sb, t, d), lambda i: (i, 0, 0)), _const_spec(w_kv_b.shape)],
        out_specs=pl.BlockSpec((sb, t, n), lambda i: (i, 0, 0)),
        out_shape=jax.ShapeDtypeStruct((bsz, t, n), F32),
        compiler_params=_params("parallel"),
        name="kv_proj_sample",
    )(x, w_kv_b)


def _compress_kernel(pt_ref, *refs, transposed):
    del pt_ref
    npg = 16
    page_refs = refs[:npg]
    rest = refs[npg:]
    if not transposed:
        look_ref, swap_ref, rest = rest[0], rest[1], rest[2:]
    (pet_ref, peb_ref, w1_ref, b1_ref, w2_ref, w2t_ref, b2_ref, b2t_ref,
     kc_ref, vc_ref, vct_ref, xs_ref, p_ref) = rest
    nch = npg * PAGE // CMP_STRIDE
    cpp = PAGE // CMP_STRIDE
    gs = nch + 2 * cpp
    hd = HEAD_DIM
    hidden = b1_ref.shape[-1]
    pr = lax.broadcasted_iota(jnp.int32, (PAGE, PAGE), 0)
    pc = lax.broadcasted_iota(jnp.int32, (PAGE, PAGE), 1)
    perm = (jnp.bitwise_and(pc, CMP_STRIDE - 1) * cpp + jnp.right_shift(pc, 4) == pr).astype(F32).astype(BF16)

    def permuted(ref):
        if transposed:
            pg = ref[...].astype(BF16)
            sw = jnp.concatenate([pg[(b ^ 1) * hd:((b ^ 1) + 1) * hd] for b in range(pg.shape[0] // hd)], axis=0)
            return (lax.dot_general(perm, pg, _NT, preferred_element_type=F32),
                    lax.dot_general(perm, sw, _NT, preferred_element_type=F32))
        tn = jnp.dot(perm, ref[...], preferred_element_type=F32)
        return tn, jnp.dot(tn.astype(BF16), swap_ref[...], preferred_element_type=F32)

    low = lax.broadcasted_iota(jnp.int32, (2 * cpp, 128), 1) < hd

    def scatter(pair_n, pair_s, row0):
        for sl in range(4):
            s = sl // 2
            ga = 2 * (sl % 2)
            cols = slice(sl * 128, (sl + 1) * 128)
            for m in range(CMP_STRIDE // 2):
                r0 = slice(2 * m * cpp, (2 * m + 1) * cpp)
                r1 = slice((2 * m + 1) * cpp, (2 * m + 2) * cpp)
                n0 = jnp.concatenate([t[r0, cols] for t in pair_n], axis=0)
                n1 = jnp.concatenate([t[r1, cols] for t in pair_n], axis=0)
                s0 = jnp.concatenate([t[r0, cols] for t in pair_s], axis=0)
                s1 = jnp.concatenate([t[r1, cols] for t in pair_s], axis=0)
                xs_ref[s, ga * gs + row0:ga * gs + row0 + 2 * cpp, m * 128:(m + 1) * 128] = (
                    jnp.where(low, n0, s1).astype(BF16))
                xs_ref[s, (ga + 1) * gs + row0:(ga + 1) * gs + row0 + 2 * cpp, m * 128:(m + 1) * 128] = (
                    jnp.where(low, s0, n1).astype(BF16))

    for k in range(0, npg, 2):
        ta, tb = permuted(page_refs[k]), permuted(page_refs[k + 1])
        scatter((ta[0], tb[0]), (ta[1], tb[1]), k * cpp)
    if transposed:
        for s in range(2):
            for g in range(N_KV):
                xs_ref[s, g * gs + nch:(g + 1) * gs, :] = jnp.zeros((2 * cpp, xs_ref.shape[-1]), BF16)
    else:
        tl = permuted(look_ref)
        scatter((tl[0], tl[0]), (tl[1], tl[1]), nch)

    nx = N_KV * gs
    for s in range(2):
        tails = []
        for pe_ref in (pet_ref, peb_ref):
            pe = pe_ref[s]
            hi = pe.astype(BF16).astype(F32)
            tails += [hi, pe - hi]
        tails.append(jnp.zeros((12, tails[0].shape[1]), F32))
        xs_ref[s, nx:nx + 16, :] = jnp.concatenate(tails, axis=0).astype(BF16)

    for s in range(2):
        p_ref[...] = jnp.dot(xs_ref[s], w1_ref[s], preferred_element_type=F32)
        cvec = (p_ref[nx:nx + 1, 0:hidden] + p_ref[nx + 1:nx + 2, 0:hidden]
                + p_ref[nx + 2:nx + 3, hidden:] + p_ref[nx + 3:nx + 4, hidden:] + b1_ref[s])
        pre = jnp.concatenate([p_ref[g * gs:g * gs + nch, 0:hidden] + p_ref[g * gs + 1:g * gs + nch + 1, hidden:]
                               for g in range(N_KV)], axis=0) + cvec
        hid = jax.nn.gelu(pre).astype(BF16)
        out = jnp.dot(hid, w2_ref[s], preferred_element_type=F32) + b2_ref[s]
        nat = jnp.concatenate([out[g * nch:(g + 1) * nch, :] for g in range(N_KV)], axis=1)
        if s == 0:
            kc_ref[0] = nat.astype(BF16)
        else:
            vc_ref[0] = nat.astype(BF16)
            outs_t = [lax.dot_general(w2t_ref[s], hid[g * nch:(g + 1) * nch, :], _NT,
                                      preferred_element_type=F32) + b2t_ref[s] for g in range(N_KV)]
            vct_ref[0] = jnp.concatenate(outs_t, axis=0).astype(BF16)


def _compress(pages, ptab, pe, w1, b1, w2, b2, *, transposed):
    nt = ptab.shape[0]
    npg = 16
    nkv = N_KV * HEAD_DIM
    half = CMP_STRIDE * HEAD_DIM
    hidden = w1.shape[-1]
    pet = pe[:, :CMP_STRIDE].reshape(2, 1, half)
    peb = pe[:, CMP_STRIDE:].reshape(2, 1, half)
    w1c = jnp.concatenate([w1[:, :half], w1[:, half:]], axis=2).astype(BF16)
    w2b = w2.astype(BF16)
    w2t = jnp.swapaxes(w2, 1, 2).astype(BF16)
    b1r = b1.reshape(2, 1, hidden)
    b2r = b2.reshape(2, 1, HEAD_DIM)
    b2t = b2.reshape(2, HEAD_DIM, 1)
    nch = npg * PAGE // CMP_STRIDE

    def page_spec(k):
        if transposed:
            return pl.BlockSpec((None, 2 * nkv, PAGE), lambda i, pt: (pt[i, k], 0, 0))
        return pl.BlockSpec((None, PAGE, 2 * nkv), lambda i, pt: (pt[i, k], 0, 0))

    def cs(shape):
        nd = len(shape)
        return pl.BlockSpec(shape, lambda i, pt: (0,) * nd, pipeline_mode=pl.Buffered(1))

    in_specs = [page_spec(k) for k in range(npg)]
    operands = [pages] * npg
    consts = (pet, peb, w1c, b1r, w2b, w2t, b2r, b2t)
    if not transposed:
        in_specs.append(page_spec(npg))
        operands.append(pages)
        f = np.arange(2 * nkv)
        swap = (f[:, None] == (f[None, :] ^ HEAD_DIM)).astype(np.float32)
        consts = (jnp.asarray(swap, BF16),) + consts
    in_specs += [cs(a.shape) for a in consts]
    gs = nch + 2 * (PAGE // CMP_STRIDE)
    grid_spec = pltpu.PrefetchScalarGridSpec(
        num_scalar_prefetch=1,
        grid=(nt,),
        in_specs=in_specs,
        out_specs=[pl.BlockSpec((1, nch, nkv), lambda i, pt: (i, 0, 0)),
                   pl.BlockSpec((1, nch, nkv), lambda i, pt: (i, 0, 0)),
                   pl.BlockSpec((1, nkv, nch), lambda i, pt: (i, 0, 0))],
        scratch_shapes=[pltpu.VMEM((2, N_KV * gs + 16, half), BF16),
                        pltpu.VMEM((N_KV * gs + 16, 2 * hidden), F32)],
    )
    return pl.pallas_call(
        functools.partial(_compress_kernel, transposed=transposed),
        grid_spec=grid_spec,
        out_shape=[jax.ShapeDtypeStruct((nt, nch, nkv), BF16),
                   jax.ShapeDtypeStruct((nt, nch, nkv), BF16),
                   jax.ShapeDtypeStruct((nt, nkv, nch), BF16)],
        compiler_params=_params("arbitrary"),
        name="compress",
    )(ptab, *operands, *consts)


def _qproj_prompt_kernel(x_ref, mod_ref, wqt_ref, wgt_ref, qt_ref, gt_ref):
    x = x_ref[0]
    shift = mod_ref[0, 0:1, :]
    scale = mod_ref[0, 1:2, :]
    ub = (x * (1.0 + scale) + shift).astype(BF16)
    qt = lax.dot_general(wqt_ref[...], ub, _NT, preferred_element_type=F32) * Q_SCALE
    qt_ref[0] = qt.astype(BF16)
    gt_ref[0] = jax.nn.sigmoid(lax.dot_general(wgt_ref[...], ub, _NT, preferred_element_type=F32))


def _qproj_prompt(x, mod, wqt, wgt, *, tm):
    bsz, t, d = x.shape
    hq = wqt.shape[0]
    ng = wgt.shape[0]
    return pl.pallas_call(
        _qproj_prompt_kernel,
        grid=(bsz, t // tm),
        in_specs=[pl.BlockSpec((1, tm, d), lambda i, j: (i, j, 0)),
                  _mod_spec(mod, 1),
                  _const_spec(wqt.shape), _const_spec(wgt.shape)],
        out_specs=[pl.BlockSpec((1, hq, tm), lambda i, j: (i, 0, j)),
                   pl.BlockSpec((1, ng, tm), lambda i, j: (i, 0, j))],
        out_shape=[jax.ShapeDtypeStruct((bsz, hq, t), BF16), jax.ShapeDtypeStruct((bsz, ng, t), F32)],
        compiler_params=_params("parallel", "parallel"),
        name="q_proj_prompt",
    )(x, mod[0], wqt, wgt)


def _qproj_sample_kernel(x_ref, mod_ref, wq_ref, wg_ref, q_ref, g_ref):
    sb, tm, d = x_ref.shape
    x = x_ref[...]
    shift = mod_ref[:, 0:1, :]
    scale = mod_ref[:, 1:2, :]
    ub = (x * (1.0 + scale) + shift).reshape(sb * tm, d).astype(BF16)
    q = jnp.dot(ub, wq_ref[...], preferred_element_type=F32) * Q_SCALE
    q_ref[...] = q.reshape(sb, tm, -1)
    for i in range(3):
        gl = jnp.dot(ub, wg_ref[i], preferred_element_type=F32)
        g_ref[:, i] = jax.nn.sigmoid(gl).reshape(sb, tm, -1)


def _qproj_sample(x, mod, wq, wgx, *, sb):
    bsz, t, d = x.shape
    hq = wq.shape[1]
    return pl.pallas_call(
        _qproj_sample_kernel,
        grid=(bsz // sb,),
        in_specs=[pl.BlockSpec((sb, t, d), lambda i: (i, 0, 0)),
                  _mod_spec(mod, sb),
                  _const_spec(wq.shape), _const_spec(wgx.shape)],
        out_specs=[pl.BlockSpec((sb, t, hq), lambda i: (i, 0, 0)),
                   pl.BlockSpec((sb, 3, t, hq), lambda i: (i, 0, 0, 0))],
        out_shape=[jax.ShapeDtypeStruct((bsz, t, hq), F32), jax.ShapeDtypeStruct((bsz, 3, t, hq), F32)],
        compiler_params=_params("parallel"),
        name="q_proj_sample",
    )(x, mod[0], wq, wgx)


def _topk_rows(work, n_sel):
    rid = lax.broadcasted_iota(jnp.int32, work.shape, 1).astype(F32)

    def body(_, wk):
        m = jnp.max(wk, axis=1, keepdims=True)
        idx = jnp.min(jnp.where(wk == m, rid, 1e9), axis=1, keepdims=True)
        return jnp.where(rid == idx, -jnp.inf, wk)

    return lax.fori_loop(0, n_sel, body, work) == -jnp.inf


def _pipelined(n, issue, finish, ahead=2):
    pending = [issue(k) for k in range(min(ahead, n))]
    for k in range(n):
        if k + ahead < n:
            pending.append(issue(k + ahead))
        finish(k, pending.pop(0))


def _attn_prompt_kernel(qt_ref, gt_ref, kc_ref, vct_ref, ovt_ref, eb_ref, ks_ref, vst_ref, kw_ref, vwt_ref,
                        o_ref, selb_ref, oc_ref, ow_ref, acc_ref, m_ref, bad_ref):
    i = pl.program_id(1)
    qb = Q_BLOCK
    hd = HEAD_DIM
    s0 = i * qb
    nc = kc_ref.shape[1]
    ns = ovt_ref.shape[0]
    nq4 = GROUP * qb
    nblk = KV_CHUNK // SLC_LEN
    nvb = KV_CHUNK // qb
    zeros = jnp.zeros((hd, nq4), BF16)
    ones = jnp.ones((16, 1), BF16)
    qpos1 = s0 + lax.broadcasted_iota(jnp.int32, (1, qb), 1)
    win_lo = jnp.maximum(i - WINDOW // qb, 0)
    n_wb = WINDOW // qb + 1
    nwk = n_wb * qb

    def tile4(a):
        return jnp.concatenate([a] * GROUP, axis=1)

    def pair(g):
        return slice((g // 2) * 128, (g // 2 + 1) * 128)

    def grow(g):
        return slice(g * hd, (g + 1) * hd)

    def qpad(g):
        qg = jnp.concatenate([qt_ref[0, (g * GROUP + r) * hd:(g * GROUP + r + 1) * hd, :] for r in range(GROUP)],
                             axis=1)
        return jnp.concatenate([qg, zeros] if g % 2 == 0 else [zeros, qg], axis=0)

    def with_ones(vt):
        return jnp.concatenate([vt, jnp.broadcast_to(ones, (16, vt.shape[1]))], axis=0)

    cend = lax.broadcasted_iota(jnp.int32, (nc, qb), 0) * CMP_STRIDE + (CMP_LEN - 1)
    bias_c = tile4(jnp.where(cend <= qpos1, 0.0, NEG))
    seen_c = tile4((qpos1 >= CMP_LEN - 1).astype(F32))
    jj = lax.broadcasted_iota(jnp.int32, (ns, qb), 0)
    cur = jnp.right_shift(qpos1, SLC_SHIFT)
    forced = (jj == 0) | (jj == cur) | (jj == cur - 1)
    valid = jj * SLC_LEN <= qpos1
    ovt = ovt_ref[...]
    imps = [None] * N_KV
    w0 = pl.multiple_of(win_lo * qb, qb)
    dist = qpos1 - (w0 + lax.broadcasted_iota(jnp.int32, (nwk, qb), 0))
    bias_w = tile4(jnp.where((dist >= 0) & (dist < WINDOW), 0.0, NEG))

    def issue_cw(k):
        g = k % N_KV
        if k < N_KV:
            return jnp.dot(kc_ref[0, :, pair(g)], qpad(g), preferred_element_type=F32) + bias_c
        return jnp.dot(kw_ref[0, pl.ds(w0, nwk), pair(g)], qpad(g), preferred_element_type=F32) + bias_w

    def finish_cw(k, st):
        g = k % N_KV
        if k < N_KV:
            p = jnp.exp2(st - jnp.max(st, axis=0, keepdims=True))
            p = p * (seen_c / jnp.sum(p, axis=0, keepdims=True))
            vct = jnp.concatenate([vct_ref[0, t, grow(g), :] for t in range(vct_ref.shape[1])], axis=1)
            oc_ref[g] = jnp.dot(vct, p.astype(BF16), preferred_element_type=F32)
            psum = p[:, 0:qb]
            for r in range(1, GROUP):
                psum = psum + p[:, r * qb:(r + 1) * qb]
            imp = sum(jnp.dot(ovt, part, preferred_element_type=F32) for part in _split3(psum))
            imps[g] = jnp.where(forced, -jnp.inf, jnp.where(valid, imp, NEG))
        else:
            pw = jnp.exp2(st - jnp.max(st, axis=0, keepdims=True))
            vwt = jnp.concatenate([vwt_ref[0, win_lo + t, grow(g), :] for t in range(n_wb)], axis=1)
            aw = jnp.dot(with_ones(vwt), pw.astype(BF16), preferred_element_type=F32)
            ow_ref[g] = aw[:hd] * (1.0 / aw[hd:hd + 1])

    _pipelined(2 * N_KV, issue_cw, finish_cw)

    few = s0 + qb <= TOP_N * SLC_LEN
    n_pick = jnp.where(few, 0, min(TOP_N, ns) - N_FORCED)
    sel = _topk_rows(jnp.stack(imps), n_pick) | (few & valid)
    selb_ref[...] = jnp.where(sel, 0.0, NEG)

    zpad = jnp.zeros((128 - 16, nq4), BF16)

    def reset():
        m_ref[...] = jnp.zeros(m_ref.shape, F32)
        acc_ref[...] = jnp.zeros(acc_ref.shape, F32)

    reset()
    bad_ref[...] = jnp.full(bad_ref.shape, -jnp.inf, F32)

    def scores(c, g, k0, cb):
        kaug = jnp.concatenate([ks_ref[0, pl.ds(k0, KV_CHUNK), pair(g)], eb_ref[...]], axis=1)
        sb = tile4(selb_ref[g, pl.ds(pl.multiple_of(c * nblk, nblk), nblk), :])
        sbp = jnp.concatenate([sb, -m_ref[g], jnp.zeros((15 - nblk, nq4), F32)], axis=0).astype(BF16)
        qaug = jnp.concatenate([qpad(g), sbp, zpad], axis=0)
        sc = jnp.dot(kaug, qaug, preferred_element_type=F32)
        return sc if cb is None else sc + cb

    def vt_ext(c, g):
        return with_ones(jnp.concatenate([vst_ref[0, c * nvb + t, grow(g), :] for t in range(nvb)], axis=1))

    def next_ref(c, r_old, cmax):
        floor = jnp.where(c == 0, -jnp.inf, 0.0)
        return (r_old + jnp.maximum(cmax, floor)).astype(BF16).astype(F32)

    def chunk(c, causal, exact):
        k0 = pl.multiple_of(c * KV_CHUNK, KV_CHUNK)
        cb = None
        if causal:
            kpos = k0 + lax.broadcasted_iota(jnp.int32, (KV_CHUNK, qb), 0)
            cb = tile4(jnp.where(kpos <= qpos1, 0.0, NEG))

        def finish(g, sc):
            cmax = jnp.max(sc, axis=0, keepdims=True)
            r_old = m_ref[g]
            r_new = next_ref(c, r_old, cmax)
            if exact:
                delta = r_new - r_old
                pv = jnp.dot(vt_ext(c, g), jnp.exp2(sc - delta).astype(BF16), preferred_element_type=F32)
                keep = jnp.exp2(-jnp.maximum(delta, 0.0))
                acc_ref[g] = keep * acc_ref[g] + pv
            else:
                pv = jnp.dot(vt_ext(c, g), jnp.exp2(sc).astype(BF16), preferred_element_type=F32)
                acc_ref[g] = (acc_ref[g] + pv) * jnp.exp2(r_old - r_new)
                low = jnp.where(c == 0, -EXP_HEADROOM - cmax, -jnp.inf)
                bad_ref[g] = jnp.maximum(bad_ref[g], jnp.maximum(cmax - EXP_HEADROOM, low))
            m_ref[g] = r_new

        _pipelined(N_KV, lambda g: scores(c, g, k0, cb), finish, ahead=1 if exact else 2)

    n_full = s0 // KV_CHUNK

    def sweep(exact):
        def body(c, carry):
            chunk(c, False, exact)
            return carry

        lax.fori_loop(0, n_full, body, 0)
        chunk(n_full, True, exact)

    sweep(False)

    @pl.when(jnp.max(bad_ref[...]) > 0.0)
    def _():
        reset()
        sweep(True)

    heads_out = []
    for g in range(N_KV):
        acc = acc_ref[g]
        o_s = acc[:hd] * (1.0 / acc[hd:hd + 1])
        o_c = oc_ref[g]
        o_w = ow_ref[g]
        for r in range(GROUP):
            h = g * GROUP + r
            ls = slice(r * qb, (r + 1) * qb)
            heads_out.append(gt_ref[0, 3 * h:3 * h + 1, :] * o_c[:, ls]
                             + gt_ref[0, 3 * h + 1:3 * h + 2, :] * o_s[:, ls]
                             + gt_ref[0, 3 * h + 2:3 * h + 3, :] * o_w[:, ls])
    out_t = jnp.concatenate(heads_out, axis=0)
    o_ref[0] = out_t.T.astype(BF16)


def _attn_prompt(qt, gt, kc, vct, ovt, eb, ks, vst, kw, vwt):
    bsz, hq, t = qt.shape
    ng = gt.shape[1]
    qb = Q_BLOCK
    nq4 = GROUP * qb

    def per_b(shape):
        nd = len(shape)
        return pl.BlockSpec((1,) + tuple(shape[1:]), lambda b, i: (b,) + (0,) * (nd - 1),
                            pipeline_mode=pl.Buffered(1))

    return pl.pallas_call(
        _attn_prompt_kernel,
        grid=(bsz, t // qb),
        in_specs=[pl.BlockSpec((1, hq, qb), lambda b, i: (b, 0, i)),
                  pl.BlockSpec((1, ng, qb), lambda b, i: (b, 0, i)),
                  per_b(kc.shape), per_b(vct.shape), _const_spec(ovt.shape), _const_spec(eb.shape),
                  per_b(ks.shape), per_b(vst.shape), per_b(kw.shape), per_b(vwt.shape)],
        out_specs=pl.BlockSpec((1, qb, hq), lambda b, i: (b, i, 0)),
        out_shape=jax.ShapeDtypeStruct((bsz, t, hq), BF16),
        scratch_shapes=[pltpu.VMEM((N_KV, t // SLC_LEN, qb), F32),
                        pltpu.VMEM((N_KV, HEAD_DIM, nq4), F32),
                        pltpu.VMEM((N_KV, HEAD_DIM, nq4), F32),
                        pltpu.VMEM((N_KV, HEAD_DIM + 16, nq4), F32),
                        pltpu.VMEM((N_KV, 1, nq4), F32),
                        pltpu.VMEM((N_KV, 1, nq4), F32)],
        compiler_params=_params("parallel", "arbitrary"),
        name="nsa_prompt",
    )(qt, gt, kc, vct, ovt, eb, ks, vst, kw, vwt)


def _topk_cols(work, n_sel):
    rid = lax.broadcasted_iota(jnp.int32, work.shape, 0).astype(F32)
    for _ in range(n_sel):
        m = jnp.max(work, axis=0, keepdims=True)
        idx = jnp.min(jnp.where(work == m, rid, 1e9), axis=0, keepdims=True)
        work = jnp.where(rid == idx, -jnp.inf, work)
    return jnp.where(work == -jnp.inf, 1.0, 0.0)


def _softmax_lanes(s):
    p = jnp.exp2(s - jnp.max(s, axis=1, keepdims=True))
    return p, jnp.sum(p, axis=1, keepdims=True)


def _heads_to_lanes(o, tq):
    hd = HEAD_DIM
    nkv = N_KV * hd
    lane = lax.broadcasted_iota(jnp.int32, (tq, nkv), 1)
    pieces = []
    for h in range(N_HEADS):
        g = h // GROUP
        blk = o[h * tq:(h + 1) * tq, :]
        blk = jnp.where((lane >= g * hd) & (lane < (g + 1) * hd), blk, 0.0)
        dst = (h * hd) % nkv
        shift = (dst - g * hd) % nkv
        pieces.append(pltpu.roll(blk, shift, axis=1) if shift else blk)
    cols = []
    per = nkv // hd
    for c in range(N_HEADS // per):
        slab = pieces[c * per]
        for k in range(1, per):
            slab = slab + pieces[c * per + k]
        cols.append(slab)
    return jnp.concatenate(cols, axis=1)


def _attn_sample_kernel(pt_ref, *refs, past_len, n_ns, spb):
    del pt_ref
    npg = past_len // PAGE
    page_refs = refs[:npg * spb]
    (q_ref, g_ref, kc_ref, vc_ref, kvn_ref, win_ref, ovt_ref, ex_ref,
     o_ref, kbuf, vbuf, kwbuf, vwbuf) = refs[npg * spb:]
    hd = HEAD_DIM
    nkv = N_KV * hd
    tq = q_ref.shape[1]
    nrow = N_HEADS * tq
    ngt = N_KV * tq
    nwin = win_ref.shape[2]
    nks = kbuf.shape[2]
    nkw = kwbuf.shape[2]
    nc = kc_ref.shape[1]
    seqs = range(spb)

    ztail = jnp.zeros((PAGE - tq, nkv), F32)
    eye = (lax.broadcasted_iota(jnp.int32, (nkv, nkv), 0)
           == lax.broadcasted_iota(jnp.int32, (nkv, nkv), 1)).astype(F32).astype(BF16)
    for s in seqs:
        for k in range(npg):
            pg = page_refs[s * npg + k]
            kbuf[s, :, k * PAGE:(k + 1) * PAGE] = pg[0:nkv, :].astype(BF16)
            vbuf[s, :, k * PAGE:(k + 1) * PAGE] = pg[nkv:2 * nkv, :].astype(BF16)
        kwbuf[s, :, 0:nwin] = win_ref[s, 0:nkv, :].astype(BF16)
        vwbuf[s, :, 0:nwin] = win_ref[s, nkv:2 * nkv, :].astype(BF16)
        kvn = kvn_ref[s]

        def new_t(slot, kvn=kvn):
            rows = jnp.concatenate([kvn[:, slot * nkv:(slot + 1) * nkv], ztail], axis=0).astype(BF16)
            return lax.dot_general(eye, rows, _NT, preferred_element_type=F32).astype(BF16)

        kbuf[s, :, past_len:past_len + PAGE] = new_t(2)
        vbuf[s, :, past_len:past_len + PAGE] = new_t(3)
        kwbuf[s, :, nwin:nwin + PAGE] = new_t(4)
        vwbuf[s, :, nwin:nwin + PAGE] = new_t(5)

    lane = lax.broadcasted_iota(jnp.int32, (tq, nkv), 1)
    qbd = []
    for s in seqs:
        q = q_ref[s]
        qrows = []
        for h in range(N_HEADS):
            g, r = divmod(h, GROUP)
            slab = q[:, g * nkv:(g + 1) * nkv]
            shift = ((g - r) * hd) % nkv
            moved = pltpu.roll(slab, shift, axis=1) if shift else slab
            qrows.append(jnp.where((lane >= g * hd) & (lane < (g + 1) * hd), moved, 0.0))
        qbd.append(jnp.concatenate(qrows, axis=0).astype(BF16))

    tpos = jnp.bitwise_and(lax.broadcasted_iota(jnp.int32, (nrow, 1), 0), tq - 1)
    qpos = past_len + tpos

    s_c = [lax.dot_general(qbd[s], kc_ref[s], _NT, preferred_element_type=F32) for s in seqs]
    s_w = [jnp.dot(qbd[s], kwbuf[s], preferred_element_type=F32) for s in seqs]
    s_s = [jnp.dot(qbd[s], kbuf[s], preferred_element_type=F32) for s in seqs]

    cend = lax.broadcasted_iota(jnp.int32, (nrow, nc), 1) * CMP_STRIDE + (CMP_LEN - 1)
    mc = cend <= qpos
    nsr = -(-n_ns // 8) * 8
    jj = lax.broadcasted_iota(jnp.int32, (nsr, ngt), 0)
    qp2 = past_len + jnp.bitwise_and(lax.broadcasted_iota(jnp.int32, (1, ngt), 1), tq - 1)
    cur = jnp.right_shift(qp2, SLC_SHIFT)
    forced = (jj == 0) | (jj == cur) | (jj == cur - 1)
    valid = (jj * SLC_LEN <= qp2) & (jj < n_ns)
    o_c, imp = [], []
    for s in seqs:
        p_c, l_c = _softmax_lanes(jnp.where(mc, s_c[s], NEG))
        p_c = jnp.where(mc, p_c * (1.0 / l_c), 0.0)
        o_c.append(jnp.dot(p_c.astype(BF16), vc_ref[s], preferred_element_type=F32))
        p4 = p_c.reshape(N_KV, GROUP, tq, nc)
        psum = p4[:, 0]
        for r in range(1, GROUP):
            psum = psum + p4[:, r]
        psum = psum.reshape(ngt, nc)
        it = sum(lax.dot_general(ovt_ref[...], part, _NT, preferred_element_type=F32)
                 for part in _split3(psum))
        imp.append(jnp.where(forced, -jnp.inf, jnp.where(valid, it[:nsr], NEG)))

    jw = lax.broadcasted_iota(jnp.int32, (nrow, nkw), 1)
    kwpos = past_len + tq - (nwin + tq) + jw
    dist = qpos - kwpos
    mw = (dist >= 0) & (dist < WINDOW) & (kwpos >= 0) & (jw < nwin + tq)
    o_w = []
    for s in seqs:
        p_w, l_w = _softmax_lanes(jnp.where(mw, s_w[s], NEG))
        o_w.append(lax.dot_general(p_w.astype(BF16), vwbuf[s], _NT, preferred_element_type=F32) * (1.0 / l_w))

    nblk = ex_ref.shape[0]
    selk = []
    for s in seqs:
        sel_t = _topk_cols(imp[s], min(TOP_N, n_ns) - N_FORCED)
        sel_t = jnp.concatenate([sel_t, jnp.zeros((nblk - nsr, ngt), F32)], axis=0)
        sel_t = jnp.concatenate([sel_t, jnp.zeros((nblk, nblk - ngt), F32)], axis=1)
        sel = sel_t.T[:ngt]
        sk = jnp.dot(sel.astype(BF16), ex_ref[...], preferred_element_type=F32)
        selk.append(jnp.broadcast_to(sk.reshape(N_KV, 1, tq, nks), (N_KV, GROUP, tq, nks)).reshape(nrow, nks))

    kpos = lax.broadcasted_iota(jnp.int32, (nrow, nks), 1)
    causal = kpos <= qpos
    o_s = []
    for s in seqs:
        p_s, l_s = _softmax_lanes(jnp.where((selk[s] > 0.5) & causal, s_s[s], NEG))
        o_s.append(lax.dot_general(p_s.astype(BF16), vbuf[s], _NT, preferred_element_type=F32) * (1.0 / l_s))

    for s in seqs:
        o_ref[s] = (g_ref[s, 0] * _heads_to_lanes(o_c[s], tq) + g_ref[s, 1] * _heads_to_lanes(o_s[s], tq)
                    + g_ref[s, 2] * _heads_to_lanes(o_w[s], tq))


def _attn_sample(pages, page_table, q, gx, kc, vc, kvn, win, ovt, ex, *, past_len, spb):
    bsz, tq, hq = q.shape
    npg = past_len // PAGE
    nkv = N_KV * HEAD_DIM
    nwin = win.shape[2]
    n_ns = -(-(past_len + tq) // SLC_LEN)
    nks = past_len + PAGE
    nkw = nwin + PAGE
    kern = functools.partial(_attn_sample_kernel, past_len=past_len, n_ns=n_ns, spb=spb)

    def page_spec(s, k):
        return pl.BlockSpec((None, 2 * nkv, PAGE), lambda b, pt: (pt[b * spb + s, k], 1, 0))

    def cs(shape):
        nd = len(shape)
        return pl.BlockSpec(shape, lambda b, pt: (0,) * nd, pipeline_mode=pl.Buffered(1))

    def per_b(shape):
        nd = len(shape)
        return pl.BlockSpec((spb,) + tuple(shape[1:]), lambda b, pt: (b,) + (0,) * (nd - 1))

    in_specs = [page_spec(s, k) for s in range(spb) for k in range(npg)]
    in_specs += [per_b(q.shape), per_b(gx.shape), per_b(kc.shape), per_b(vc.shape), per_b(kvn.shape),
                 per_b(win.shape), cs(ovt.shape), cs(ex.shape)]
    grid_spec = pltpu.PrefetchScalarGridSpec(
        num_scalar_prefetch=1,
        grid=(bsz // spb,),
        in_specs=in_specs,
        out_specs=pl.BlockSpec((spb, tq, hq), lambda b, pt: (b, 0, 0)),
        scratch_shapes=[pltpu.VMEM((spb, nkv, nks), BF16), pltpu.VMEM((spb, nkv, nks), BF16),
                        pltpu.VMEM((spb, nkv, nkw), BF16), pltpu.VMEM((spb, nkv, nkw), BF16)],
    )
    return pl.pallas_call(
        kern,
        grid_spec=grid_spec,
        out_shape=jax.ShapeDtypeStruct((bsz, tq, hq), F32),
        compiler_params=_params("arbitrary"),
        name="nsa_sample",
    )(page_table, *([pages] * (npg * spb)), q, gx, kc, vc, kvn, win, ovt, ex)


def _oproj_kernel(o_ref, x_ref, mod_ref, wo_ref, g_ref, b_ref, out_ref, *, alpha):
    sb, tm, d = x_ref.shape
    ob = o_ref[...].reshape(sb * tm, -1).astype(BF16)
    mix = jnp.dot(ob, wo_ref[...], preferred_element_type=F32).reshape(sb, tm, d)
    gate = mod_ref[:, 2:3, :]
    y = alpha * x_ref[...] + (1.0 + gate) * mix
    out_ref[...] = _layer_norm(y, g_ref[...], b_ref[...])


def _oproj(o, x, mod, wo, g, b, *, sb, tm, alpha):
    bsz, t, d = x.shape
    hq = o.shape[-1]
    kern = functools.partial(_oproj_kernel, alpha=alpha)
    return pl.pallas_call(
        kern,
        grid=(bsz // sb, t // tm),
        in_specs=[pl.BlockSpec((sb, tm, hq), lambda i, j: (i, j, 0)),
                  pl.BlockSpec((sb, tm, d), lambda i, j: (i, j, 0)),
                  _mod_spec(mod, sb),
                  _const_spec(wo.shape), _const_spec(g.shape), _const_spec(b.shape)],
        out_specs=pl.BlockSpec((sb, tm, d), lambda i, j: (i, j, 0)),
        out_shape=jax.ShapeDtypeStruct((bsz, t, d), F32),
        compiler_params=_params("parallel", "parallel"),
        name="o_proj",
    )(o, x, mod[0], wo, g, b)


def _overlap(n_c, n_s):
    ci = np.arange(n_c)[:, None] * CMP_STRIDE
    sj = np.arange(n_s)[None, :] * SLC_LEN
    return ((ci <= sj + SLC_LEN - 1) & (ci + CMP_LEN - 1 >= sj)).astype(np.float32)


def kernel(x_prompt, x_sample, cache_kv, state_kv_win, state_pool, state_conv, page_table, c_prompt, c_sample, ada_w, ada_b, ln_g, ln_b, pool_w, pool_ls, ffn_w_up, ffn_b_up, ffn_w_conv, ffn_b_conv, ffn_w_down, w_kv, cmp_pe, cmp_w1, cmp_b1, cmp_w2, cmp_b2, nsa_w_qg, nsa_w_o):
    bp, t, d = x_prompt.shape
    bs, ts, _ = x_sample.shape
    depth = ada_w.shape[0]
    n_a = pool_w.shape[0]
    f2 = ffn_w_up.shape[-1]
    f = f2 // 2
    past_len = page_table.shape[1] * PAGE
    nkv = N_KV * HEAD_DIM
    hq = N_HEADS * HEAD_DIM
    alpha = float((2 * depth) ** 0.25)
    assert t % (16 * PAGE) == 0 and t >= WINDOW + Q_BLOCK
    assert ts == 8 and past_len == 16 * PAGE and state_kv_win.shape[1] == WINDOW
    assert f % 1408 == 0 or f % 256 == 0

    tm_p = 512
    sb_pool = min(bs, 64)
    sb_ffn = min(bs, 16)
    cf = 1408 if f % 1408 == 0 else 256

    ada = _ada(jnp.concatenate([c_sample, c_prompt], axis=0), ada_w, ada_b)
    ada = ada.reshape(depth, bs + bp, 6, d)
    mod_s = [(ada, l, 0) for l in range(depth)]
    mod_p = [(ada, l, bs) for l in range(depth)]
    assert past_len >= 2 * SLC_LEN and -(-(past_len + ts) // SLC_LEN) > TOP_N

    pool_wb = pool_w.astype(BF16)
    wup_b = ffn_w_up.astype(BF16)
    wdn_b = ffn_w_down.astype(BF16)
    w_kv_b = w_kv.astype(BF16)
    w_kvt_b = w_kv.T.astype(BF16)
    wq_b = nsa_w_qg[:, :, :hq].astype(BF16)
    wqt_b = jnp.swapaxes(nsa_w_qg[:, :, :hq], 1, 2).astype(BF16)
    wgt_b = jnp.swapaxes(nsa_w_qg[:, :, hq:], 1, 2).astype(BF16)
    wg = nsa_w_qg[:, :, hq:].reshape(-1, d, N_HEADS, 3)
    wgx_b = jnp.repeat(jnp.moveaxis(wg, 3, 1), HEAD_DIM, axis=3).astype(BF16)
    wo_b = nsa_w_o.astype(BF16)

    def vec(a):
        return a.reshape(1, -1)

    pool0 = jnp.zeros((bp, POOL_HALO, d), F32)
    conv0 = jnp.zeros((bp, CONV_HALO, f2), F32)
    pool_prev_s = jnp.pad(state_pool, ((0, 0), (0, 0), (POOL_HALO - state_pool.shape[2], 0), (0, 0)))
    conv_prev_s = jnp.pad(state_conv, ((0, 0), (0, 0), (CONV_HALO - state_conv.shape[2], 0), (0, 0)))

    xp, xs = x_prompt, x_sample
    pool_p, pool_s, conv_p, conv_s = [], [], [], []
    ctx_p = ctx_s = None
    kv_p = kv_s = None

    for l in range(depth):
        g1, b1, g2, b2 = vec(ln_g[l, 0]), vec(ln_b[l, 0]), vec(ln_g[l, 1]), vec(ln_b[l, 1])
        if l < n_a:
            xp, st = _pool_layer(xp, mod_p[l], pool0, pool_wb[l], vec(pool_ls[l]), g1, b1,
                                 sb=1, tm=tm_p, start_pos=0, alpha=alpha)
            pool_p.append(st[:, 1:])
            xs, st = _pool_layer(xs, mod_s[l], pool_prev_s[l], pool_wb[l], vec(pool_ls[l]), g1, b1,
                                 sb=sb_pool, tm=ts, start_pos=past_len, alpha=alpha)
            pool_s.append(st[:, 1:])
        else:
            jn = l - n_a
            qt, gt = _qproj_prompt(xp, mod_p[l], wqt_b[jn], wgt_b[jn], tm=tm_p)
            op = _attn_prompt(qt, gt, *ctx_p)
            xp = _oproj(op, xp, mod_p[l], wo_b[jn], g1, b1, sb=1, tm=tm_p, alpha=alpha)
            qs, gs = _qproj_sample(xs, mod_s[l], wq_b[jn], wgx_b[jn], sb=sb_pool)
            os_ = _attn_sample(ctx_s[0], page_table, qs, gs, *ctx_s[1:], past_len=past_len,
                               spb=2 if bs % 2 == 0 else 1)
            xs = _oproj(os_, xs, mod_s[l], wo_b[jn], g1, b1, sb=sb_pool, tm=ts, alpha=alpha)

        xp, st = _ffn_layer(xp, mod_p[l], conv0, wup_b[l], vec(ffn_b_up[l]), ffn_w_conv[l], vec(ffn_b_conv[l]),
                            wdn_b[l], g2, b2, sb=1, tm=tm_p, cf=cf, alpha=alpha)
        conv_p.append(st[:, CONV_HALO - (CONV_W - 1):])
        xs, st = _ffn_layer(xs, mod_s[l], conv_prev_s[l], wup_b[l], vec(ffn_b_up[l]), ffn_w_conv[l],
                            vec(ffn_b_conv[l]), wdn_b[l], g2, b2, sb=sb_ffn, tm=ts, cf=cf, alpha=alpha)
        conv_s.append(st[:, CONV_HALO - (CONV_W - 1):])

        if l == n_a - 1:
            kvpt, kvwt, cmp_rows, ks, kw, vst, vwt = _kvproj_prompt(xp, w_kv_b, w_kvt_b, tm=tm_p)
            kv_p = (kvpt, kvwt)
            tiles = t // (16 * PAGE)
            ppb = t // PAGE
            base = (np.arange(bp)[:, None] * ppb + np.arange(tiles)[None, :] * 16).reshape(-1, 1)
            ids = base + np.arange(17)[None, :]
            last = (np.arange(bp)[:, None] * ppb + ppb - 1).repeat(tiles, axis=1).reshape(-1)
            ids[:, 16] = np.minimum(ids[:, 16], last)
            kc, _, vct = _compress(cmp_rows.reshape(bp * ppb, PAGE, 2 * nkv), jnp.asarray(ids, jnp.int32),
                                   cmp_pe, cmp_w1, cmp_b1, cmp_w2, cmp_b2, transposed=False)
            n_c = tiles * 128
            ovt = jnp.asarray(_overlap(n_c, t // SLC_LEN).T, BF16)
            eb = (np.arange(KV_CHUNK)[:, None] // SLC_LEN == np.arange(128)[None, :]).astype(np.float32)
            eb[:, KV_CHUNK // SLC_LEN] = 1.0
            ctx_p = (kc.reshape(bp, n_c, nkv), vct.reshape(bp, tiles, nkv, 128), ovt, jnp.asarray(eb, BF16),
                     ks, vst, kw, vwt)

            kvs = _kvproj_sample(xs, w_kv_b, sb=sb_pool)
            kv_s = kvs
            assert (past_len + ts - CMP_LEN) // CMP_STRIDE + 1 == past_len // CMP_STRIDE - 1
            pages_t = jnp.transpose(cache_kv, (0, 2, 3, 4, 1)).reshape(cache_kv.shape[0], 4 * nkv, PAGE)
            kc_s, vc_s, _ = _compress(pages_t, page_table.astype(jnp.int32), cmp_pe, cmp_w1, cmp_b1, cmp_w2, cmp_b2,
                                      transposed=True)
            n_ns = -(-(past_len + ts) // SLC_LEN)
            ov_s = np.zeros((128, 128), np.float32)
            n_cs = (past_len + ts - CMP_LEN) // CMP_STRIDE + 1
            ov_s[:n_cs, :n_ns] = _overlap(n_cs, n_ns)
            nks = past_len + PAGE
            ex = (np.arange(nks)[None, :] // SLC_LEN == np.arange(128)[:, None]).astype(np.float32)
            win_t = jnp.transpose(state_kv_win, (0, 2, 3, 4, 1)).reshape(bs, 2 * nkv, state_kv_win.shape[1])
            ctx_s = (pages_t, kc_s, vc_s, kvs, win_t, jnp.asarray(ov_s.T, BF16), jnp.asarray(ex, BF16))

    kvpt, kvwt = kv_p
    wlen = min(WINDOW, t)
    wbuf = state_kv_win.shape[1]
    win_s = jnp.concatenate([state_kv_win, kv_s[:, :, 4 * nkv:].reshape(bs, ts, 2, N_KV, HEAD_DIM)], axis=1)[:, -wbuf:]
    kv_prompt = jnp.transpose(kvpt.reshape(bp, 4, N_KV, HEAD_DIM, t), (0, 4, 1, 2, 3))
    win_prompt = jnp.transpose(kvwt[:, :, t - wlen:].reshape(bp, 2, N_KV, HEAD_DIM, wlen), (0, 4, 1, 2, 3))
    return (xp, xs,
            kv_prompt,
            kv_s[:, :, :4 * nkv].reshape(bs, ts, 4, N_KV, HEAD_DIM),
            win_prompt,
            win_s,
            jnp.stack(pool_p), jnp.stack(pool_s), jnp.stack(conv_p), jnp.stack(conv_s))
```

```python
import functools

import numpy as np
import jax
import jax.numpy as jnp
from jax import lax
from jax.experimental import pallas as pl
from jax.experimental.pallas import tpu as pltpu

F32 = jnp.float32
BF16 = jnp.bfloat16

POOL_WINDOWS = (2, 4, 8, 16)
POOL_HALO = 16
N_HEADS = 16
HEAD_DIM = 64
N_KV = 4
GROUP = N_HEADS // N_KV
CMP_LEN = 32
CMP_STRIDE = 16
SLC_LEN = 64
SLC_SHIFT = 6
TOP_N = 16
WINDOW = 512
Q_BLOCK = 128
CONV_W = 3
CONV_HALO = 8
PAGE = 128
LN_EPS = 1e-5
NEG = -1e30
N_FORCED = 3
KV_CHUNK = 512
Q_SCALE = HEAD_DIM ** -0.5 * 1.4426950408889634
EXP_HEADROOM = 100.0
VMEM_LIMIT = 56 * 1024 * 1024
MXU_TILE = 256

_NT = (((1,), (1,)), ((), ()))


def _params(*sem):
    return pltpu.CompilerParams(dimension_semantics=sem, vmem_limit_bytes=VMEM_LIMIT)


def _const_spec(shape):
    nd = len(shape)
    return pl.BlockSpec(shape, lambda *_: (0,) * nd, pipeline_mode=pl.Buffered(1))


def _mod_spec(mod, sb):
    ada, layer, row0 = mod
    assert row0 % sb == 0
    return pl.BlockSpec((None, sb, 6, ada.shape[-1]), lambda i, *_: (layer, row0 // sb + i, 0, 0))


def _layer_norm(y, g, b):
    mu = jnp.mean(y, axis=-1, keepdims=True)
    d = y - mu
    var = jnp.mean(d * d, axis=-1, keepdims=True)
    return d * lax.rsqrt(var + LN_EPS) * g + b


def _split3(x):
    hi = x.astype(BF16)
    r1 = x - hi.astype(F32)
    mid = r1.astype(BF16)
    lo = (r1 - mid.astype(F32)).astype(BF16)
    return hi, mid, lo


def _ada_kernel(c_ref, w_ref, b_ref, o_ref):
    c = c_ref[...]
    s = c * jax.nn.sigmoid(c)
    o_ref[0] = jnp.dot(s.astype(BF16), w_ref[0].astype(BF16), preferred_element_type=F32) + b_ref[0]


def _ada(c_all, ada_w, ada_b):
    depth, d, n = ada_w.shape
    r = c_all.shape[0]
    tn = n // 4
    return pl.pallas_call(
        _ada_kernel,
        grid=(depth, n // tn),
        in_specs=[pl.BlockSpec((r, d), lambda l, j: (0, 0)),
                  pl.BlockSpec((1, d, tn), lambda l, j: (l, 0, j)),
                  pl.BlockSpec((1, 1, tn), lambda l, j: (l, 0, j))],
        out_specs=pl.BlockSpec((1, r, tn), lambda l, j: (l, 0, j)),
        out_shape=jax.ShapeDtypeStruct((depth, r, n), F32),
        compiler_params=_params("parallel", "parallel"),
        name="ada_params",
    )(c_all, ada_w, ada_b.reshape(depth, 1, n))


def _pool_kernel(x_ref, mod_ref, prev_ref, pw_ref, ls_ref, g_ref, b_ref, o_ref, ps_ref, ext_ref,
                 *, start_pos, nj, alpha):
    j = pl.program_id(1)
    sb, tm, d = x_ref.shape
    pg = d // len(POOL_WINDOWS)
    h = POOL_HALO
    x = x_ref[...]
    shift = mod_ref[:, 0:1, :]
    scale = mod_ref[:, 1:2, :]
    gate = mod_ref[:, 2:3, :]
    u = x * (1.0 + scale) + shift

    @pl.when(j == 0)
    def _():
        ext_ref[:, 0:h, :] = prev_ref[...]

    ext_ref[:, h:h + tm, :] = u
    pos = start_pos + j * tm + lax.broadcasted_iota(jnp.int32, (1, tm, pg), 1)
    outs = []
    for gi, w in enumerate(POOL_WINDOWS):
        c0 = gi * pg
        acc = ext_ref[:, h:h + tm, c0:c0 + pg]
        for k in range(1, w):
            acc = acc + ext_ref[:, h - k:h - k + tm, c0:c0 + pg]
        cnt = jnp.minimum(pos + 1, w).astype(F32)
        pooled = acc / cnt - u[:, :, c0:c0 + pg]
        outs.append(jnp.dot(pooled.reshape(sb * tm, pg).astype(BF16), pw_ref[gi],
                            preferred_element_type=F32))
    mixed = jnp.concatenate(outs, axis=-1) * ls_ref[...]
    y = alpha * x + (1.0 + gate) * mixed.reshape(sb, tm, d)
    o_ref[...] = _layer_norm(y, g_ref[...], b_ref[...])

    if nj > 1:
        @pl.when(j < nj - 1)
        def _():
            ext_ref[:, 0:h, :] = ext_ref[:, tm:tm + h, :]

    @pl.when(j == nj - 1)
    def _():
        ps_ref[...] = ext_ref[:, tm:tm + h, :]


def _pool_layer(x, mod, prev, pw, ls, g, b, *, sb, tm, start_pos, alpha):
    bsz, t, d = x.shape
    nj = t // tm
    h = POOL_HALO
    kern = functools.partial(_pool_kernel, start_pos=start_pos, nj=nj, alpha=alpha)
    return pl.pallas_call(
        kern,
        grid=(bsz // sb, nj),
        in_specs=[pl.BlockSpec((sb, tm, d), lambda i, j: (i, j, 0)),
                  _mod_spec(mod, sb),
                  pl.BlockSpec((sb, h, d), lambda i, j: (i, 0, 0)),
                  _const_spec(pw.shape), _const_spec(ls.shape), _const_spec(g.shape), _const_spec(b.shape)],
        out_specs=[pl.BlockSpec((sb, tm, d), lambda i, j: (i, j, 0)),
                   pl.BlockSpec((sb, h, d), lambda i, j: (i, 0, 0))],
        out_shape=[jax.ShapeDtypeStruct((bsz, t, d), F32), jax.ShapeDtypeStruct((bsz, h, d), F32)],
        scratch_shapes=[pltpu.VMEM((sb, h + tm, d), F32)],
        compiler_params=_params("parallel", "arbitrary"),
        name="pool_layer",
    )(x, mod[0], prev, pw, ls, g, b)


def _ffn_kernel(x_ref, mod_ref, cprev_ref, wup_ref, bup_ref, wc_ref, bc_ref, wdn_ref, g_ref, b_ref,
                o_ref, cs_ref, hext_ref, *, cfs, alpha):
    j = pl.program_id(1)
    sb, tm, d = x_ref.shape
    f = wdn_ref.shape[0]
    hh = CONV_HALO
    x = x_ref[...]
    shift = mod_ref[:, 3:4, :]
    scale = mod_ref[:, 4:5, :]
    gate = mod_ref[:, 5:6, :]
    ub = (x * (1.0 + scale) + shift).reshape(sb * tm, d).astype(BF16)

    @pl.when(j == 0)
    def _():
        cs_ref[...] = cprev_ref[...]

    starts = [sum(cfs[:k]) for k in range(len(cfs))]
    acc = [jnp.zeros((sb * tm, d), F32)]

    def up_proj(k):
        return [jnp.dot(ub, wup_ref[:, p * f + starts[k]:p * f + starts[k] + cfs[k]], preferred_element_type=F32)
                for p in range(2)]

    def conv_gate_down(k, hps):
        cw, f0 = cfs[k], starts[k]
        parts = []
        for p in range(2):
            c0 = p * f + f0
            slot = 2 * (k % 2) + p
            hext_ref[slot, :, 0:hh, 0:cw] = cs_ref[:, :, c0:c0 + cw]
            hext_ref[slot, :, hh:hh + tm, 0:cw] = (hps[p] + bup_ref[:, c0:c0 + cw]).reshape(sb, tm, cw)
            hc = (bc_ref[:, c0:c0 + cw]
                  + hext_ref[slot, :, hh - 2:hh - 2 + tm, 0:cw] * wc_ref[0:1, c0:c0 + cw]
                  + hext_ref[slot, :, hh - 1:hh - 1 + tm, 0:cw] * wc_ref[1:2, c0:c0 + cw]
                  + hext_ref[slot, :, hh:hh + tm, 0:cw] * wc_ref[2:3, c0:c0 + cw])
            cs_ref[:, :, c0:c0 + cw] = hext_ref[slot, :, tm:tm + hh, 0:cw]
            parts.append(hc)
        a, v = parts
        gated = (a * jax.nn.sigmoid(a)) * v
        acc[0] = acc[0] + jnp.dot(gated.reshape(sb * tm, cw).astype(BF16), wdn_ref[f0:f0 + cw, :],
                                  preferred_element_type=F32)

    _pipelined(len(cfs), up_proj, conv_gate_down, ahead=1)
    acc = acc[0]
    y = alpha * x + (1.0 + gate) * acc.reshape(sb, tm, d)
    o_ref[...] = _layer_norm(y, g_ref[...], b_ref[...])


def _ffn_chunks(f):
    if f % MXU_TILE:
        return (f,)
    n = f // MXU_TILE
    k = min(2, n)
    return tuple((n // k + (1 if i < n % k else 0)) * MXU_TILE for i in range(k))


def _ffn_layer(x, mod, cprev, wup, bup, wc, bc, wdn, g, b, *, sb, tm, alpha):
    bsz, t, d = x.shape
    f2 = wup.shape[1]
    hh = CONV_HALO
    cfs = _ffn_chunks(f2 // 2)
    cf = max(cfs)
    kern = functools.partial(_ffn_kernel, cfs=cfs, alpha=alpha)
    return pl.pallas_call(
        kern,
        grid=(bsz // sb, t // tm),
        in_specs=[pl.BlockSpec((sb, tm, d), lambda i, j: (i, j, 0)),
                  _mod_spec(mod, sb),
                  pl.BlockSpec((sb, hh, f2), lambda i, j: (i, 0, 0)),
                  _const_spec(wup.shape), _const_spec(bup.shape), _const_spec(wc.shape),
                  _const_spec(bc.shape), _const_spec(wdn.shape), _const_spec(g.shape), _const_spec(b.shape)],
        out_specs=[pl.BlockSpec((sb, tm, d), lambda i, j: (i, j, 0)),
                   pl.BlockSpec((sb, hh, f2), lambda i, j: (i, 0, 0))],
        out_shape=[jax.ShapeDtypeStruct((bsz, t, d), F32), jax.ShapeDtypeStruct((bsz, hh, f2), F32)],
        scratch_shapes=[pltpu.VMEM((4, sb, hh + tm, cf), F32)],
        compiler_params=_params("parallel", "arbitrary"),
        name="conv_ffn_layer",
    )(x, mod[0], cprev, wup, bup, wc, bc, wdn, g, b)


def _kvproj_prompt_kernel(x_ref, w_ref, wt_ref, kvpt_ref, kvwt_ref, cmp_ref, ks_ref, kw_ref, vst_ref, vwt_ref):
    tm = x_ref.shape[1]
    nkv = N_KV * HEAD_DIM
    xb = x_ref[0].astype(BF16)
    kv = jnp.dot(xb, w_ref[...], preferred_element_type=F32)
    cmp_ref[0] = kv[:, :2 * nkv].astype(BF16)
    ks_ref[0] = kv[:, 2 * nkv:3 * nkv].astype(BF16)
    kw_ref[0] = kv[:, 4 * nkv:5 * nkv].astype(BF16)
    kvt = lax.dot_general(wt_ref[...], xb, _NT, preferred_element_type=F32)
    kvpt_ref[0] = kvt[:4 * nkv]
    kvwt_ref[0] = kvt[4 * nkv:]
    for c in range(tm // Q_BLOCK):
        cols = slice(c * Q_BLOCK, (c + 1) * Q_BLOCK)
        vst_ref[0, c] = kvt[3 * nkv:4 * nkv, cols].astype(BF16)
        vwt_ref[0, c] = kvt[5 * nkv:6 * nkv, cols].astype(BF16)


def _kvproj_prompt(x, w_kv_b, w_kvt_b, *, tm):
    bsz, t, d = x.shape
    nkv = N_KV * HEAD_DIM
    nb = tm // Q_BLOCK
    return pl.pallas_call(
        _kvproj_prompt_kernel,
        grid=(bsz, t // tm),
        in_specs=[pl.BlockSpec((1, tm, d), lambda i, j: (i, j, 0)),
                  _const_spec(w_kv_b.shape), _const_spec(w_kvt_b.shape)],
        out_specs=[pl.BlockSpec((1, 4 * nkv, tm), lambda i, j: (i, 0, j)),
                   pl.BlockSpec((1, 2 * nkv, tm), lambda i, j: (i, 0, j)),
                   pl.BlockSpec((1, tm, 2 * nkv), lambda i, j: (i, j, 0)),
                   pl.BlockSpec((1, tm, nkv), lambda i, j: (i, j, 0)),
                   pl.BlockSpec((1, tm, nkv), lambda i, j: (i, j, 0)),
                   pl.BlockSpec((1, nb, nkv, Q_BLOCK), lambda i, j: (i, j, 0, 0)),
                   pl.BlockSpec((1, nb, nkv, Q_BLOCK), lambda i, j: (i, j, 0, 0))],
        out_shape=[jax.ShapeDtypeStruct((bsz, 4 * nkv, t), F32),
                   jax.ShapeDtypeStruct((bsz, 2 * nkv, t), F32),
                   jax.ShapeDtypeStruct((bsz, t, 2 * nkv), BF16),
                   jax.ShapeDtypeStruct((bsz, t, nkv), BF16),
                   jax.ShapeDtypeStruct((bsz, t, nkv), BF16),
                   jax.ShapeDtypeStruct((bsz, t // Q_BLOCK, nkv, Q_BLOCK), BF16),
                   jax.ShapeDtypeStruct((bsz, t // Q_BLOCK, nkv, Q_BLOCK), BF16)],
        compiler_params=_params("parallel", "parallel"),
        name="kv_proj_prompt",
    )(x, w_kv_b, w_kvt_b)


def _kvproj_sample_kernel(x_ref, w_ref, kv_ref):
    sb, tm, d = x_ref.shape
    xb = x_ref[...].reshape(sb * tm, d).astype(BF16)
    kv_ref[...] = jnp.dot(xb, w_ref[...], preferred_element_type=F32).reshape(sb, tm, -1)


def _kvproj_sample(x, w_kv_b, *, sb):
    bsz, t, d = x.shape
    n = w_kv_b.shape[1]
    return pl.pallas_call(
        _kvproj_sample_kernel,
        grid=(bsz // sb,),
        in_specs=[pl.BlockSpec((sb, t, d), lambda i: (i, 0, 0)), _const_spec(w_kv_b.shape)],
        out_specs=pl.BlockSpec((sb, t, n), lambda i: (i, 0, 0)),
        out_shape=jax.ShapeDtypeStruct((bsz, t, n), F32),
        compiler_params=_params("parallel"),
        name="kv_proj_sample",
    )(x, w_kv_b)


def _compress_kernel(pt_ref, *refs, transposed):
    del pt_ref
    npg = 16
    page_refs = refs[:npg]
    rest = refs[npg:]
    if not transposed:
        look_ref, swap_ref, rest = rest[0], rest[1], rest[2:]
    (pet_ref, peb_ref, w1_ref, b1_ref, w2_ref, w2t_ref, b2_ref, b2t_ref,
     kc_ref, vc_ref, vct_ref, xs_ref, p_ref) = rest
    nch = npg * PAGE // CMP_STRIDE
    cpp = PAGE // CMP_STRIDE
    gs = nch + 2 * cpp
    hd = HEAD_DIM
    hidden = b1_ref.shape[-1]
    pr = lax.broadcasted_iota(jnp.int32, (PAGE, PAGE), 0)
    pc = lax.broadcasted_iota(jnp.int32, (PAGE, PAGE), 1)
    perm = (jnp.bitwise_and(pc, CMP_STRIDE - 1) * cpp + jnp.right_shift(pc, 4) == pr).astype(F32).astype(BF16)

    def permuted(ref):
        if transposed:
            pg = ref[...].astype(BF16)
            sw = jnp.concatenate([pg[(b ^ 1) * hd:((b ^ 1) + 1) * hd] for b in range(pg.shape[0] // hd)], axis=0)
            return (lax.dot_general(perm, pg, _NT, preferred_element_type=F32),
                    lax.dot_general(perm, sw, _NT, preferred_element_type=F32))
        tn = jnp.dot(perm, ref[...], preferred_element_type=F32)
        return tn, jnp.dot(tn.astype(BF16), swap_ref[...], preferred_element_type=F32)

    low = lax.broadcasted_iota(jnp.int32, (2 * cpp, 128), 1) < hd

    def scatter(pair_n, pair_s, row0):
        for sl in range(4):
            s = sl // 2
            ga = 2 * (sl % 2)
            cols = slice(sl * 128, (sl + 1) * 128)
            for m in range(CMP_STRIDE // 2):
                r0 = slice(2 * m * cpp, (2 * m + 1) * cpp)
                r1 = slice((2 * m + 1) * cpp, (2 * m + 2) * cpp)
                n0 = jnp.concatenate([t[r0, cols] for t in pair_n], axis=0)
                n1 = jnp.concatenate([t[r1, cols] for t in pair_n], axis=0)
                s0 = jnp.concatenate([t[r0, cols] for t in pair_s], axis=0)
                s1 = jnp.concatenate([t[r1, cols] for t in pair_s], axis=0)
                xs_ref[s, ga * gs + row0:ga * gs + row0 + 2 * cpp, m * 128:(m + 1) * 128] = (
                    jnp.where(low, n0, s1).astype(BF16))
                xs_ref[s, (ga + 1) * gs + row0:(ga + 1) * gs + row0 + 2 * cpp, m * 128:(m + 1) * 128] = (
                    jnp.where(low, s0, n1).astype(BF16))

    for k in range(0, npg, 2):
        ta, tb = permuted(page_refs[k]), permuted(page_refs[k + 1])
        scatter((ta[0], tb[0]), (ta[1], tb[1]), k * cpp)
    if transposed:
        for s in range(2):
            for g in range(N_KV):
                xs_ref[s, g * gs + nch:(g + 1) * gs, :] = jnp.zeros((2 * cpp, xs_ref.shape[-1]), BF16)
    else:
        tl = permuted(look_ref)
        scatter((tl[0], tl[0]), (tl[1], tl[1]), nch)

    nx = N_KV * gs
    for s in range(2):
        tails = []
        for pe_ref in (pet_ref, peb_ref):
            pe = pe_ref[s]
            hi = pe.astype(BF16).astype(F32)
            tails += [hi, pe - hi]
        tails.append(jnp.zeros((12, tails[0].shape[1]), F32))
        xs_ref[s, nx:nx + 16, :] = jnp.concatenate(tails, axis=0).astype(BF16)

    for s in range(2):
        p_ref[...] = jnp.dot(xs_ref[s], w1_ref[s], preferred_element_type=F32)
        cvec = (p_ref[nx:nx + 1, 0:hidden] + p_ref[nx + 1:nx + 2, 0:hidden]
                + p_ref[nx + 2:nx + 3, hidden:] + p_ref[nx + 3:nx + 4, hidden:] + b1_ref[s])
        pre = jnp.concatenate([p_ref[g * gs:g * gs + nch, 0:hidden] + p_ref[g * gs + 1:g * gs + nch + 1, hidden:]
                               for g in range(N_KV)], axis=0) + cvec
        hid = jax.nn.gelu(pre).astype(BF16)
        out = jnp.dot(hid, w2_ref[s], preferred_element_type=F32) + b2_ref[s]
        nat = jnp.concatenate([out[g * nch:(g + 1) * nch, :] for g in range(N_KV)], axis=1)
        if s == 0:
            kc_ref[0] = nat.astype(BF16)
        else:
            vc_ref[0] = nat.astype(BF16)
            outs_t = [lax.dot_general(w2t_ref[s], hid[g * nch:(g + 1) * nch, :], _NT,
                                      preferred_element_type=F32) + b2t_ref[s] for g in range(N_KV)]
            vct_ref[0] = jnp.concatenate(outs_t, axis=0).astype(BF16)


def _compress(pages, ptab, pe, w1, b1, w2, b2, *, transposed):
    nt = ptab.shape[0]
    npg = 16
    nkv = N_KV * HEAD_DIM
    half = CMP_STRIDE * HEAD_DIM
    hidden = w1.shape[-1]
    pet = pe[:, :CMP_STRIDE].reshape(2, 1, half)
    peb = pe[:, CMP_STRIDE:].reshape(2, 1, half)
    w1c = jnp.concatenate([w1[:, :half], w1[:, half:]], axis=2).astype(BF16)
    w2b = w2.astype(BF16)
    w2t = jnp.swapaxes(w2, 1, 2).astype(BF16)
    b1r = b1.reshape(2, 1, hidden)
    b2r = b2.reshape(2, 1, HEAD_DIM)
    b2t = b2.reshape(2, HEAD_DIM, 1)
    nch = npg * PAGE // CMP_STRIDE

    def page_spec(k):
        if transposed:
            return pl.BlockSpec((None, 2 * nkv, PAGE), lambda i, pt: (pt[i, k], 0, 0))
        return pl.BlockSpec((None, PAGE, 2 * nkv), lambda i, pt: (pt[i, k], 0, 0))

    def cs(shape):
        nd = len(shape)
        return pl.BlockSpec(shape, lambda i, pt: (0,) * nd, pipeline_mode=pl.Buffered(1))

    in_specs = [page_spec(k) for k in range(npg)]
    operands = [pages] * npg
    consts = (pet, peb, w1c, b1r, w2b, w2t, b2r, b2t)
    if not transposed:
        in_specs.append(page_spec(npg))
        operands.append(pages)
        f = np.arange(2 * nkv)
        swap = (f[:, None] == (f[None, :] ^ HEAD_DIM)).astype(np.float32)
        consts = (jnp.asarray(swap, BF16),) + consts
    in_specs += [cs(a.shape) for a in consts]
    gs = nch + 2 * (PAGE // CMP_STRIDE)
    grid_spec = pltpu.PrefetchScalarGridSpec(
        num_scalar_prefetch=1,
        grid=(nt,),
        in_specs=in_specs,
        out_specs=[pl.BlockSpec((1, nch, nkv), lambda i, pt: (i, 0, 0)),
                   pl.BlockSpec((1, nch, nkv), lambda i, pt: (i, 0, 0)),
                   pl.BlockSpec((1, nkv, nch), lambda i, pt: (i, 0, 0))],
        scratch_shapes=[pltpu.VMEM((2, N_KV * gs + 16, half), BF16),
                        pltpu.VMEM((N_KV * gs + 16, 2 * hidden), F32)],
    )
    return pl.pallas_call(
        functools.partial(_compress_kernel, transposed=transposed),
        grid_spec=grid_spec,
        out_shape=[jax.ShapeDtypeStruct((nt, nch, nkv), BF16),
                   jax.ShapeDtypeStruct((nt, nch, nkv), BF16),
                   jax.ShapeDtypeStruct((nt, nkv, nch), BF16)],
        compiler_params=_params("arbitrary"),
        name="compress",
    )(ptab, *operands, *consts)


def _qproj_prompt_kernel(x_ref, mod_ref, wqt_ref, wgt_ref, qt_ref, gt_ref):
    x = x_ref[0]
    shift = mod_ref[0, 0:1, :]
    scale = mod_ref[0, 1:2, :]
    ub = (x * (1.0 + scale) + shift).astype(BF16)
    qt = lax.dot_general(wqt_ref[...], ub, _NT, preferred_element_type=F32) * Q_SCALE
    qt_ref[0] = qt.astype(BF16)
    gt_ref[0] = jax.nn.sigmoid(lax.dot_general(wgt_ref[...], ub, _NT, preferred_element_type=F32))


def _qproj_prompt(x, mod, wqt, wgt, *, tm):
    bsz, t, d = x.shape
    hq = wqt.shape[0]
    ng = wgt.shape[0]
    return pl.pallas_call(
        _qproj_prompt_kernel,
        grid=(bsz, t // tm),
        in_specs=[pl.BlockSpec((1, tm, d), lambda i, j: (i, j, 0)),
                  _mod_spec(mod, 1),
                  _const_spec(wqt.shape), _const_spec(wgt.shape)],
        out_specs=[pl.BlockSpec((1, hq, tm), lambda i, j: (i, 0, j)),
                   pl.BlockSpec((1, ng, tm), lambda i, j: (i, 0, j))],
        out_shape=[jax.ShapeDtypeStruct((bsz, hq, t), BF16), jax.ShapeDtypeStruct((bsz, ng, t), F32)],
        compiler_params=_params("parallel", "parallel"),
        name="q_proj_prompt",
    )(x, mod[0], wqt, wgt)


def _qproj_sample_kernel(x_ref, mod_ref, wq_ref, wg_ref, q_ref, g_ref):
    sb, tm, d = x_ref.shape
    x = x_ref[...]
    shift = mod_ref[:, 0:1, :]
    scale = mod_ref[:, 1:2, :]
    ub = (x * (1.0 + scale) + shift).reshape(sb * tm, d).astype(BF16)
    q = jnp.dot(ub, wq_ref[...], preferred_element_type=F32) * Q_SCALE
    q_ref[...] = q.reshape(sb, tm, -1)
    for i in range(3):
        gl = jnp.dot(ub, wg_ref[i], preferred_element_type=F32)
        g_ref[:, i] = jax.nn.sigmoid(gl).reshape(sb, tm, -1)


def _qproj_sample(x, mod, wq, wgx, *, sb):
    bsz, t, d = x.shape
    hq = wq.shape[1]
    return pl.pallas_call(
        _qproj_sample_kernel,
        grid=(bsz // sb,),
        in_specs=[pl.BlockSpec((sb, t, d), lambda i: (i, 0, 0)),
                  _mod_spec(mod, sb),
                  _const_spec(wq.shape), _const_spec(wgx.shape)],
        out_specs=[pl.BlockSpec((sb, t, hq), lambda i: (i, 0, 0)),
                   pl.BlockSpec((sb, 3, t, hq), lambda i: (i, 0, 0, 0))],
        out_shape=[jax.ShapeDtypeStruct((bsz, t, hq), F32), jax.ShapeDtypeStruct((bsz, 3, t, hq), F32)],
        compiler_params=_params("parallel"),
        name="q_proj_sample",
    )(x, mod[0], wq, wgx)


def _topk_rows(work, n_sel):
    rid = lax.broadcasted_iota(jnp.int32, work.shape, 1).astype(F32)

    def body(_, wk):
        m = jnp.max(wk, axis=1, keepdims=True)
        idx = jnp.min(jnp.where(wk == m, rid, 1e9), axis=1, keepdims=True)
        return jnp.where(rid == idx, -jnp.inf, wk)

    return lax.fori_loop(0, n_sel, body, work) == -jnp.inf


def _pipelined(n, issue, finish, ahead=2):
    pending = [issue(k) for k in range(min(ahead, n))]
    for k in range(n):
        if k + ahead < n:
            pending.append(issue(k + ahead))
        finish(k, pending.pop(0))


def _attn_prompt_kernel(qt_ref, gt_ref, kc_ref, vct_ref, ovt_ref, eb_ref, ks_ref, vst_ref, kw_ref, vwt_ref,
                        o_ref, selb_ref, oc_ref, ow_ref, acc_ref, m_ref, bad_ref):
    i = pl.program_id(1)
    qb = Q_BLOCK
    hd = HEAD_DIM
    s0 = i * qb
    nc = kc_ref.shape[1]
    ns = ovt_ref.shape[0]
    nq4 = GROUP * qb
    nblk = KV_CHUNK // SLC_LEN
    nvb = KV_CHUNK // qb
    zeros = jnp.zeros((hd, nq4), BF16)
    ones = jnp.ones((16, 1), BF16)
    qpos1 = s0 + lax.broadcasted_iota(jnp.int32, (1, qb), 1)
    win_lo = jnp.maximum(i - WINDOW // qb, 0)
    n_wb = WINDOW // qb + 1
    nwk = n_wb * qb

    def tile4(a):
        return jnp.concatenate([a] * GROUP, axis=1)

    def pair(g):
        return slice((g // 2) * 128, (g // 2 + 1) * 128)

    def grow(g):
        return slice(g * hd, (g + 1) * hd)

    def qpad(g):
        qg = jnp.concatenate([qt_ref[0, (g * GROUP + r) * hd:(g * GROUP + r + 1) * hd, :] for r in range(GROUP)],
                             axis=1)
        return jnp.concatenate([qg, zeros] if g % 2 == 0 else [zeros, qg], axis=0)

    def with_ones(vt):
        return jnp.concatenate([vt, jnp.broadcast_to(ones, (16, vt.shape[1]))], axis=0)

    cend = lax.broadcasted_iota(jnp.int32, (nc, qb), 0) * CMP_STRIDE + (CMP_LEN - 1)
    bias_c = tile4(jnp.where(cend <= qpos1, 0.0, NEG))
    seen_c = tile4((qpos1 >= CMP_LEN - 1).astype(F32))
    jj = lax.broadcasted_iota(jnp.int32, (ns, qb), 0)
    cur = jnp.right_shift(qpos1, SLC_SHIFT)
    forced = (jj == 0) | (jj == cur) | (jj == cur - 1)
    valid = jj * SLC_LEN <= qpos1
    ovt = ovt_ref[...]
    w0 = pl.multiple_of(win_lo * qb, qb)
    dist = qpos1 - (w0 + lax.broadcasted_iota(jnp.int32, (nwk, qb), 0))
    bias_w = tile4(jnp.where((dist >= 0) & (dist < WINDOW), 0.0, NEG))
    few = s0 + qb <= TOP_N * SLC_LEN

    def issue_cw(k):
        g = k % N_KV
        if k < N_KV:
            return jnp.dot(kc_ref[0, :, pair(g)], qpad(g), preferred_element_type=F32) + bias_c
        return jnp.dot(kw_ref[0, pl.ds(w0, nwk), pair(g)], qpad(g), preferred_element_type=F32) + bias_w

    def branches_and_selection(exact):
        imps = [None] * N_KV

        def finish_cw(k, st):
            g = k % N_KV
            cmax = jnp.max(st, axis=0, keepdims=True)
            if k < N_KV:
                vct = jnp.concatenate([vct_ref[0, t, grow(g), :] for t in range(vct_ref.shape[1])], axis=1)
                if exact:
                    p = jnp.exp2(st - cmax)
                    p = p * (seen_c / jnp.sum(p, axis=0, keepdims=True))
                    oc_ref[g] = jnp.dot(vct, p.astype(BF16), preferred_element_type=F32)
                    psum = p[:, 0:qb]
                    for r in range(1, GROUP):
                        psum = psum + p[:, r * qb:(r + 1) * qb]
                    imp = sum(jnp.dot(ovt, part, preferred_element_type=F32) for part in _split3(psum))
                else:
                    y = jnp.dot(jnp.concatenate([with_ones(vct), ovt], axis=0), jnp.exp2(st).astype(BF16),
                                preferred_element_type=F32)
                    inv = jnp.where(seen_c > 0.0, 1.0 / y[hd:hd + 1], 0.0)
                    oc_ref[g] = y[:hd] * inv
                    imp = y[hd + 16:, 0:qb] * inv[:, 0:qb]
                    for r in range(1, GROUP):
                        imp = imp + y[hd + 16:, r * qb:(r + 1) * qb] * inv[:, r * qb:(r + 1) * qb]
                    low = jnp.where(seen_c > 0.0, -EXP_HEADROOM - cmax, -jnp.inf)
                    bad_ref[g] = jnp.maximum(bad_ref[g], jnp.maximum(cmax - EXP_HEADROOM, low))
                imps[g] = jnp.where(forced, -jnp.inf, jnp.where(valid, imp, NEG))
            else:
                vwt = jnp.concatenate([vwt_ref[0, win_lo + t, grow(g), :] for t in range(n_wb)], axis=1)
                pw = jnp.exp2(st - cmax) if exact else jnp.exp2(st)
                aw = jnp.dot(with_ones(vwt), pw.astype(BF16), preferred_element_type=F32)
                ow_ref[g] = aw[:hd] * (1.0 / aw[hd:hd + 1])
                if not exact:
                    bad_ref[g] = jnp.maximum(bad_ref[g], jnp.maximum(cmax - EXP_HEADROOM, -EXP_HEADROOM - cmax))

        _pipelined(2 * N_KV, issue_cw, finish_cw, ahead=1 if exact else 2)

        n_pick = jnp.where(few, 0, min(TOP_N, ns) - N_FORCED)
        sel = _topk_rows(jnp.stack(imps), n_pick) | (few & valid)
        selb_ref[...] = jnp.where(sel, 0.0, NEG)

    zpad = jnp.zeros((128 - 16, nq4), BF16)


    def scores(c, g, k0, cb):
        kaug = jnp.concatenate([ks_ref[0, pl.ds(k0, KV_CHUNK), pair(g)], eb_ref[...]], axis=1)
        sb = tile4(selb_ref[g, pl.ds(pl.multiple_of(c * nblk, nblk), nblk), :])
        sbp = jnp.concatenate([sb, -m_ref[g], jnp.zeros((15 - nblk, nq4), F32)], axis=0).astype(BF16)
        qaug = jnp.concatenate([qpad(g), sbp, zpad], axis=0)
        sc = jnp.dot(kaug, qaug, preferred_element_type=F32)
        return sc if cb is None else sc + cb

    def vt_ext(c, g):
        return with_ones(jnp.concatenate([vst_ref[0, c * nvb + t, grow(g), :] for t in range(nvb)], axis=1))

    def next_ref(c, r_old, cmax):
        floor = jnp.where(c == 0, -jnp.inf, 0.0)
        return (r_old + jnp.maximum(cmax, floor)).astype(BF16).astype(F32)

    def chunk(c, causal, exact):
        k0 = pl.multiple_of(c * KV_CHUNK, KV_CHUNK)
        cb = None
        if causal:
            kpos = k0 + lax.broadcasted_iota(jnp.int32, (KV_CHUNK, qb), 0)
            cb = tile4(jnp.where(kpos <= qpos1, 0.0, NEG))

        def finish(g, sc):
            cmax = jnp.max(sc, axis=0, keepdims=True)
            r_old = m_ref[g]
            r_new = next_ref(c, r_old, cmax)
            if exact:
                delta = r_new - r_old
                pv = jnp.dot(vt_ext(c, g), jnp.exp2(sc - delta).astype(BF16), preferred_element_type=F32)
                keep = jnp.exp2(-jnp.maximum(delta, 0.0))
                acc_ref[g] = keep * acc_ref[g] + pv
            else:
                pv = jnp.dot(vt_ext(c, g), jnp.exp2(sc).astype(BF16), preferred_element_type=F32)
                acc_ref[g] = (acc_ref[g] + pv) * jnp.exp2(r_old - r_new)
                low = jnp.where(c == 0, -EXP_HEADROOM - cmax, -jnp.inf)
                bad_ref[g] = jnp.maximum(bad_ref[g], jnp.maximum(cmax - EXP_HEADROOM, low))
            m_ref[g] = r_new

        _pipelined(N_KV, lambda g: scores(c, g, k0, cb), finish, ahead=1 if exact else 2)

    n_full = s0 // KV_CHUNK

    def sweep(exact):
        m_ref[...] = jnp.zeros(m_ref.shape, F32)
        acc_ref[...] = jnp.zeros(acc_ref.shape, F32)

        def body(c, carry):
            chunk(c, False, exact)
            return carry

        lax.fori_loop(0, n_full, body, 0)
        chunk(n_full, True, exact)

    bad_ref[...] = jnp.full(bad_ref.shape, -jnp.inf, F32)
    branches_and_selection(False)
    sweep(False)

    @pl.when(jnp.max(bad_ref[...]) > 0.0)
    def _():
        branches_and_selection(True)
        sweep(True)

    heads_out = []
    for g in range(N_KV):
        acc = acc_ref[g]
        o_s = acc[:hd] * (1.0 / acc[hd:hd + 1])
        o_c = oc_ref[g]
        o_w = ow_ref[g]
        for r in range(GROUP):
            h = g * GROUP + r
            ls = slice(r * qb, (r + 1) * qb)
            heads_out.append(gt_ref[0, 3 * h:3 * h + 1, :] * o_c[:, ls]
                             + gt_ref[0, 3 * h + 1:3 * h + 2, :] * o_s[:, ls]
                             + gt_ref[0, 3 * h + 2:3 * h + 3, :] * o_w[:, ls])
    out_t = jnp.concatenate(heads_out, axis=0)
    o_ref[0] = out_t.T.astype(BF16)


def _attn_prompt(qt, gt, kc, vct, ovt, eb, ks, vst, kw, vwt):
    bsz, hq, t = qt.shape
    ng = gt.shape[1]
    qb = Q_BLOCK
    nq4 = GROUP * qb

    def per_b(shape):
        nd = len(shape)
        return pl.BlockSpec((1,) + tuple(shape[1:]), lambda b, i: (b,) + (0,) * (nd - 1),
                            pipeline_mode=pl.Buffered(1))

    return pl.pallas_call(
        _attn_prompt_kernel,
        grid=(bsz, t // qb),
        in_specs=[pl.BlockSpec((1, hq, qb), lambda b, i: (b, 0, i)),
                  pl.BlockSpec((1, ng, qb), lambda b, i: (b, 0, i)),
                  per_b(kc.shape), per_b(vct.shape), _const_spec(ovt.shape), _const_spec(eb.shape),
                  per_b(ks.shape), per_b(vst.shape), per_b(kw.shape), per_b(vwt.shape)],
        out_specs=pl.BlockSpec((1, qb, hq), lambda b, i: (b, i, 0)),
        out_shape=jax.ShapeDtypeStruct((bsz, t, hq), BF16),
        scratch_shapes=[pltpu.VMEM((N_KV, t // SLC_LEN, qb), F32),
                        pltpu.VMEM((N_KV, HEAD_DIM, nq4), F32),
                        pltpu.VMEM((N_KV, HEAD_DIM, nq4), F32),
                        pltpu.VMEM((N_KV, HEAD_DIM + 16, nq4), F32),
                        pltpu.VMEM((N_KV, 1, nq4), F32),
                        pltpu.VMEM((N_KV, 1, nq4), F32)],
        compiler_params=_params("parallel", "arbitrary"),
        name="nsa_prompt",
    )(qt, gt, kc, vct, ovt, eb, ks, vst, kw, vwt)


def _topk_cols(work, n_sel):
    rid = lax.broadcasted_iota(jnp.int32, work.shape, 0).astype(F32)
    for _ in range(n_sel):
        m = jnp.max(work, axis=0, keepdims=True)
        idx = jnp.min(jnp.where(work == m, rid, 1e9), axis=0, keepdims=True)
        work = jnp.where(rid == idx, -jnp.inf, work)
    return jnp.where(work == -jnp.inf, 1.0, 0.0)


def _softmax_lanes(s):
    p = jnp.exp2(s - jnp.max(s, axis=1, keepdims=True))
    return p, jnp.sum(p, axis=1, keepdims=True)


def _heads_to_lanes(o, tq):
    hd = HEAD_DIM
    nkv = N_KV * hd
    lane = lax.broadcasted_iota(jnp.int32, (tq, nkv), 1)
    pieces = []
    for h in range(N_HEADS):
        g = h // GROUP
        blk = o[h * tq:(h + 1) * tq, :]
        blk = jnp.where((lane >= g * hd) & (lane < (g + 1) * hd), blk, 0.0)
        dst = (h * hd) % nkv
        shift = (dst - g * hd) % nkv
        pieces.append(pltpu.roll(blk, shift, axis=1) if shift else blk)
    cols = []
    per = nkv // hd
    for c in range(N_HEADS // per):
        slab = pieces[c * per]
        for k in range(1, per):
            slab = slab + pieces[c * per + k]
        cols.append(slab)
    return jnp.concatenate(cols, axis=1)


def _attn_sample_kernel(pt_ref, *refs, past_len, n_ns, spb):
    del pt_ref
    npg = past_len // PAGE
    page_refs = refs[:npg * spb]
    (q_ref, g_ref, kc_ref, vc_ref, kvn_ref, win_ref, ovt_ref, ex_ref,
     o_ref, kbuf, vbuf, kwbuf, vwbuf) = refs[npg * spb:]
    hd = HEAD_DIM
    nkv = N_KV * hd
    tq = q_ref.shape[1]
    nrow = N_HEADS * tq
    ngt = N_KV * tq
    nwin = win_ref.shape[2]
    nks = kbuf.shape[2]
    nkw = kwbuf.shape[2]
    nc = kc_ref.shape[1]
    seqs = range(spb)

    ztail = jnp.zeros((PAGE - tq, nkv), F32)
    eye = (lax.broadcasted_iota(jnp.int32, (nkv, nkv), 0)
           == lax.broadcasted_iota(jnp.int32, (nkv, nkv), 1)).astype(F32).astype(BF16)
    for s in seqs:
        for k in range(npg):
            pg = page_refs[s * npg + k]
            kbuf[s, :, k * PAGE:(k + 1) * PAGE] = pg[0:nkv, :].astype(BF16)
            vbuf[s, :, k * PAGE:(k + 1) * PAGE] = pg[nkv:2 * nkv, :].astype(BF16)
        kwbuf[s, :, 0:nwin] = win_ref[s, 0:nkv, :].astype(BF16)
        vwbuf[s, :, 0:nwin] = win_ref[s, nkv:2 * nkv, :].astype(BF16)
        kvn = kvn_ref[s]

        def new_t(slot, kvn=kvn):
            rows = jnp.concatenate([kvn[:, slot * nkv:(slot + 1) * nkv], ztail], axis=0).astype(BF16)
            return lax.dot_general(eye, rows, _NT, preferred_element_type=F32).astype(BF16)

        kbuf[s, :, past_len:past_len + PAGE] = new_t(2)
        vbuf[s, :, past_len:past_len + PAGE] = new_t(3)
        kwbuf[s, :, nwin:nwin + PAGE] = new_t(4)
        vwbuf[s, :, nwin:nwin + PAGE] = new_t(5)

    lane = lax.broadcasted_iota(jnp.int32, (tq, nkv), 1)
    qbd = []
    for s in seqs:
        q = q_ref[s]
        qrows = []
        for h in range(N_HEADS):
            g, r = divmod(h, GROUP)
            slab = q[:, g * nkv:(g + 1) * nkv]
            shift = ((g - r) * hd) % nkv
            moved = pltpu.roll(slab, shift, axis=1) if shift else slab
            qrows.append(jnp.where((lane >= g * hd) & (lane < (g + 1) * hd), moved, 0.0))
        qbd.append(jnp.concatenate(qrows, axis=0).astype(BF16))

    tpos = jnp.bitwise_and(lax.broadcasted_iota(jnp.int32, (nrow, 1), 0), tq - 1)
    qpos = past_len + tpos

    s_c = [lax.dot_general(qbd[s], kc_ref[s], _NT, preferred_element_type=F32) for s in seqs]
    s_w = [jnp.dot(qbd[s], kwbuf[s], preferred_element_type=F32) for s in seqs]
    s_s = [jnp.dot(qbd[s], kbuf[s], preferred_element_type=F32) for s in seqs]

    cend = lax.broadcasted_iota(jnp.int32, (nrow, nc), 1) * CMP_STRIDE + (CMP_LEN - 1)
    mc = cend <= qpos
    nsr = -(-n_ns // 8) * 8
    jj = lax.broadcasted_iota(jnp.int32, (nsr, ngt), 0)
    qp2 = past_len + jnp.bitwise_and(lax.broadcasted_iota(jnp.int32, (1, ngt), 1), tq - 1)
    cur = jnp.right_shift(qp2, SLC_SHIFT)
    forced = (jj == 0) | (jj == cur) | (jj == cur - 1)
    valid = (jj * SLC_LEN <= qp2) & (jj < n_ns)
    o_c, imp = [], []
    for s in seqs:
        p_c, l_c = _softmax_lanes(jnp.where(mc, s_c[s], NEG))
        p_c = jnp.where(mc, p_c * (1.0 / l_c), 0.0)
        o_c.append(jnp.dot(p_c.astype(BF16), vc_ref[s], preferred_element_type=F32))
        p4 = p_c.reshape(N_KV, GROUP, tq, nc)
        psum = p4[:, 0]
        for r in range(1, GROUP):
            psum = psum + p4[:, r]
        psum = psum.reshape(ngt, nc)
        it = sum(lax.dot_general(ovt_ref[...], part, _NT, preferred_element_type=F32)
                 for part in _split3(psum))
        imp.append(jnp.where(forced, -jnp.inf, jnp.where(valid, it[:nsr], NEG)))

    jw = lax.broadcasted_iota(jnp.int32, (nrow, nkw), 1)
    kwpos = past_len + tq - (nwin + tq) + jw
    dist = qpos - kwpos
    mw = (dist >= 0) & (dist < WINDOW) & (kwpos >= 0) & (jw < nwin + tq)
    o_w = []
    for s in seqs:
        p_w, l_w = _softmax_lanes(jnp.where(mw, s_w[s], NEG))
        o_w.append(lax.dot_general(p_w.astype(BF16), vwbuf[s], _NT, preferred_element_type=F32) * (1.0 / l_w))

    nblk = ex_ref.shape[0]
    selk = []
    for s in seqs:
        sel_t = _topk_cols(imp[s], min(TOP_N, n_ns) - N_FORCED)
        sel_t = jnp.concatenate([sel_t, jnp.zeros((nblk - nsr, ngt), F32)], axis=0)
        sel_t = jnp.concatenate([sel_t, jnp.zeros((nblk, nblk - ngt), F32)], axis=1)
        sel = sel_t.T[:ngt]
        sk = jnp.dot(sel.astype(BF16), ex_ref[...], preferred_element_type=F32)
        selk.append(jnp.broadcast_to(sk.reshape(N_KV, 1, tq, nks), (N_KV, GROUP, tq, nks)).reshape(nrow, nks))

    kpos = lax.broadcasted_iota(jnp.int32, (nrow, nks), 1)
    causal = kpos <= qpos
    o_s = []
    for s in seqs:
        p_s, l_s = _softmax_lanes(jnp.where((selk[s] > 0.5) & causal, s_s[s], NEG))
        o_s.append(lax.dot_general(p_s.astype(BF16), vbuf[s], _NT, preferred_element_type=F32) * (1.0 / l_s))

    for s in seqs:
        o_ref[s] = (g_ref[s, 0] * _heads_to_lanes(o_c[s], tq) + g_ref[s, 1] * _heads_to_lanes(o_s[s], tq)
                    + g_ref[s, 2] * _heads_to_lanes(o_w[s], tq))


def _attn_sample(pages, page_table, q, gx, kc, vc, kvn, win, ovt, ex, *, past_len, spb):
    bsz, tq, hq = q.shape
    npg = past_len // PAGE
    nkv = N_KV * HEAD_DIM
    nwin = win.shape[2]
    n_ns = -(-(past_len + tq) // SLC_LEN)
    nks = past_len + PAGE
    nkw = nwin + PAGE
    kern = functools.partial(_attn_sample_kernel, past_len=past_len, n_ns=n_ns, spb=spb)

    def page_spec(s, k):
        return pl.BlockSpec((None, 2 * nkv, PAGE), lambda b, pt: (pt[b * spb + s, k], 1, 0))

    def cs(shape):
        nd = len(shape)
        return pl.BlockSpec(shape, lambda b, pt: (0,) * nd, pipeline_mode=pl.Buffered(1))

    def per_b(shape):
        nd = len(shape)
        return pl.BlockSpec((spb,) + tuple(shape[1:]), lambda b, pt: (b,) + (0,) * (nd - 1))

    in_specs = [page_spec(s, k) for s in range(spb) for k in range(npg)]
    in_specs += [per_b(q.shape), per_b(gx.shape), per_b(kc.shape), per_b(vc.shape), per_b(kvn.shape),
                 per_b(win.shape), cs(ovt.shape), cs(ex.shape)]
    grid_spec = pltpu.PrefetchScalarGridSpec(
        num_scalar_prefetch=1,
        grid=(bsz // spb,),
        in_specs=in_specs,
        out_specs=pl.BlockSpec((spb, tq, hq), lambda b, pt: (b, 0, 0)),
        scratch_shapes=[pltpu.VMEM((spb, nkv, nks), BF16), pltpu.VMEM((spb, nkv, nks), BF16),
                        pltpu.VMEM((spb, nkv, nkw), BF16), pltpu.VMEM((spb, nkv, nkw), BF16)],
    )
    return pl.pallas_call(
        kern,
        grid_spec=grid_spec,
        out_shape=jax.ShapeDtypeStruct((bsz, tq, hq), F32),
        compiler_params=_params("arbitrary"),
        name="nsa_sample",
    )(page_table, *([pages] * (npg * spb)), q, gx, kc, vc, kvn, win, ovt, ex)


def _oproj_kernel(o_ref, x_ref, mod_ref, wo_ref, g_ref, b_ref, out_ref, *, alpha):
    sb, tm, d = x_ref.shape
    ob = o_ref[...].reshape(sb * tm, -1).astype(BF16)
    mix = jnp.dot(ob, wo_ref[...], preferred_element_type=F32).reshape(sb, tm, d)
    gate = mod_ref[:, 2:3, :]
    y = alpha * x_ref[...] + (1.0 + gate) * mix
    out_ref[...] = _layer_norm(y, g_ref[...], b_ref[...])


def _oproj(o, x, mod, wo, g, b, *, sb, tm, alpha):
    bsz, t, d = x.shape
    hq = o.shape[-1]
    kern = functools.partial(_oproj_kernel, alpha=alpha)
    return pl.pallas_call(
        kern,
        grid=(bsz // sb, t // tm),
        in_specs=[pl.BlockSpec((sb, tm, hq), lambda i, j: (i, j, 0)),
                  pl.BlockSpec((sb, tm, d), lambda i, j: (i, j, 0)),
                  _mod_spec(mod, sb),
                  _const_spec(wo.shape), _const_spec(g.shape), _const_spec(b.shape)],
        out_specs=pl.BlockSpec((sb, tm, d), lambda i, j: (i, j, 0)),
        out_shape=jax.ShapeDtypeStruct((bsz, t, d), F32),
        compiler_params=_params("parallel", "parallel"),
        name="o_proj",
    )(o, x, mod[0], wo, g, b)


def _overlap(n_c, n_s):
    ci = np.arange(n_c)[:, None] * CMP_STRIDE
    sj = np.arange(n_s)[None, :] * SLC_LEN
    return ((ci <= sj + SLC_LEN - 1) & (ci + CMP_LEN - 1 >= sj)).astype(np.float32)


def kernel(x_prompt, x_sample, cache_kv, state_kv_win, state_pool, state_conv, page_table, c_prompt, c_sample, ada_w, ada_b, ln_g, ln_b, pool_w, pool_ls, ffn_w_up, ffn_b_up, ffn_w_conv, ffn_b_conv, ffn_w_down, w_kv, cmp_pe, cmp_w1, cmp_b1, cmp_w2, cmp_b2, nsa_w_qg, nsa_w_o):
    bp, t, d = x_prompt.shape
    bs, ts, _ = x_sample.shape
    depth = ada_w.shape[0]
    n_a = pool_w.shape[0]
    f2 = ffn_w_up.shape[-1]
    f = f2 // 2
    past_len = page_table.shape[1] * PAGE
    nkv = N_KV * HEAD_DIM
    hq = N_HEADS * HEAD_DIM
    alpha = float((2 * depth) ** 0.25)
    assert t % (16 * PAGE) == 0 and t >= WINDOW + Q_BLOCK
    assert ts == 8 and past_len == 16 * PAGE and state_kv_win.shape[1] == WINDOW
    assert f % 128 == 0

    tm_p = 512
    sb_pool = min(bs, 64)
    sb_ffn = min(bs, 16)

    ada = _ada(jnp.concatenate([c_sample, c_prompt], axis=0), ada_w, ada_b)
    ada = ada.reshape(depth, bs + bp, 6, d)
    mod_s = [(ada, l, 0) for l in range(depth)]
    mod_p = [(ada, l, bs) for l in range(depth)]
    assert past_len >= 2 * SLC_LEN and -(-(past_len + ts) // SLC_LEN) > TOP_N

    pool_wb = pool_w.astype(BF16)
    wup_b = ffn_w_up.astype(BF16)
    wdn_b = ffn_w_down.astype(BF16)
    w_kv_b = w_kv.astype(BF16)
    w_kvt_b = w_kv.T.astype(BF16)
    wq_b = nsa_w_qg[:, :, :hq].astype(BF16)
    wqt_b = jnp.swapaxes(nsa_w_qg[:, :, :hq], 1, 2).astype(BF16)
    wgt_b = jnp.swapaxes(nsa_w_qg[:, :, hq:], 1, 2).astype(BF16)
    wg = nsa_w_qg[:, :, hq:].reshape(-1, d, N_HEADS, 3)
    wgx_b = jnp.repeat(jnp.moveaxis(wg, 3, 1), HEAD_DIM, axis=3).astype(BF16)
    wo_b = nsa_w_o.astype(BF16)

    def vec(a):
        return a.reshape(1, -1)

    pool0 = jnp.zeros((bp, POOL_HALO, d), F32)
    conv0 = jnp.zeros((bp, CONV_HALO, f2), F32)
    pool_prev_s = jnp.pad(state_pool, ((0, 0), (0, 0), (POOL_HALO - state_pool.shape[2], 0), (0, 0)))
    conv_prev_s = jnp.pad(state_conv, ((0, 0), (0, 0), (CONV_HALO - state_conv.shape[2], 0), (0, 0)))

    xp, xs = x_prompt, x_sample
    pool_p, pool_s, conv_p, conv_s = [], [], [], []
    ctx_p = ctx_s = None
    kv_p = kv_s = None

    for l in range(depth):
        g1, b1, g2, b2 = vec(ln_g[l, 0]), vec(ln_b[l, 0]), vec(ln_g[l, 1]), vec(ln_b[l, 1])
        if l < n_a:
            xp, st = _pool_layer(xp, mod_p[l], pool0, pool_wb[l], vec(pool_ls[l]), g1, b1,
                                 sb=1, tm=tm_p, start_pos=0, alpha=alpha)
            pool_p.append(st[:, 1:])
            xs, st = _pool_layer(xs, mod_s[l], pool_prev_s[l], pool_wb[l], vec(pool_ls[l]), g1, b1,
                                 sb=sb_pool, tm=ts, start_pos=past_len, alpha=alpha)
            pool_s.append(st[:, 1:])
        else:
            jn = l - n_a
            qt, gt = _qproj_prompt(xp, mod_p[l], wqt_b[jn], wgt_b[jn], tm=tm_p)
            op = _attn_prompt(qt, gt, *ctx_p)
            xp = _oproj(op, xp, mod_p[l], wo_b[jn], g1, b1, sb=1, tm=tm_p, alpha=alpha)
            qs, gs = _qproj_sample(xs, mod_s[l], wq_b[jn], wgx_b[jn], sb=sb_pool)
            os_ = _attn_sample(ctx_s[0], page_table, qs, gs, *ctx_s[1:], past_len=past_len,
                               spb=2 if bs % 2 == 0 else 1)
            xs = _oproj(os_, xs, mod_s[l], wo_b[jn], g1, b1, sb=sb_pool, tm=ts, alpha=alpha)

        xp, st = _ffn_layer(xp, mod_p[l], conv0, wup_b[l], vec(ffn_b_up[l]), ffn_w_conv[l], vec(ffn_b_conv[l]),
                            wdn_b[l], g2, b2, sb=1, tm=tm_p, alpha=alpha)
        conv_p.append(st[:, CONV_HALO - (CONV_W - 1):])
        xs, st = _ffn_layer(xs, mod_s[l], conv_prev_s[l], wup_b[l], vec(ffn_b_up[l]), ffn_w_conv[l],
                            vec(ffn_b_conv[l]), wdn_b[l], g2, b2, sb=sb_ffn, tm=ts, alpha=alpha)
        conv_s.append(st[:, CONV_HALO - (CONV_W - 1):])

        if l == n_a - 1:
            kvpt, kvwt, cmp_rows, ks, kw, vst, vwt = _kvproj_prompt(xp, w_kv_b, w_kvt_b, tm=tm_p)
            kv_p = (kvpt, kvwt)
            tiles = t // (16 * PAGE)
            ppb = t // PAGE
            base = (np.arange(bp)[:, None] * ppb + np.arange(tiles)[None, :] * 16).reshape(-1, 1)
            ids = base + np.arange(17)[None, :]
            last = (np.arange(bp)[:, None] * ppb + ppb - 1).repeat(tiles, axis=1).reshape(-1)
            ids[:, 16] = np.minimum(ids[:, 16], last)
            kc, _, vct = _compress(cmp_rows.reshape(bp * ppb, PAGE, 2 * nkv), jnp.asarray(ids, jnp.int32),
                                   cmp_pe, cmp_w1, cmp_b1, cmp_w2, cmp_b2, transposed=False)
            n_c = tiles * 128
            ovt = jnp.asarray(_overlap(n_c, t // SLC_LEN).T, BF16)
            eb = (np.arange(KV_CHUNK)[:, None] // SLC_LEN == np.arange(128)[None, :]).astype(np.float32)
            eb[:, KV_CHUNK // SLC_LEN] = 1.0
            ctx_p = (kc.reshape(bp, n_c, nkv), vct.reshape(bp, tiles, nkv, 128), ovt, jnp.asarray(eb, BF16),
                     ks, vst, kw, vwt)

            kvs = _kvproj_sample(xs, w_kv_b, sb=sb_pool)
            kv_s = kvs
            assert (past_len + ts - CMP_LEN) // CMP_STRIDE + 1 == past_len // CMP_STRIDE - 1
            pages_t = jnp.transpose(cache_kv, (0, 2, 3, 4, 1)).reshape(cache_kv.shape[0], 4 * nkv, PAGE)
            kc_s, vc_s, _ = _compress(pages_t, page_table.astype(jnp.int32), cmp_pe, cmp_w1, cmp_b1, cmp_w2, cmp_b2,
                                      transposed=True)
            n_ns = -(-(past_len + ts) // SLC_LEN)
            ov_s = np.zeros((128, 128), np.float32)
            n_cs = (past_len + ts - CMP_LEN) // CMP_STRIDE + 1
            ov_s[:n_cs, :n_ns] = _overlap(n_cs, n_ns)
            nks = past_len + PAGE
            ex = (np.arange(nks)[None, :] // SLC_LEN == np.arange(128)[:, None]).astype(np.float32)
            win_t = jnp.transpose(state_kv_win, (0, 2, 3, 4, 1)).reshape(bs, 2 * nkv, state_kv_win.shape[1])
            ctx_s = (pages_t, kc_s, vc_s, kvs, win_t, jnp.asarray(ov_s.T, BF16), jnp.asarray(ex, BF16))

    kvpt, kvwt = kv_p
    wlen = min(WINDOW, t)
    wbuf = state_kv_win.shape[1]
    win_s = jnp.concatenate([state_kv_win, kv_s[:, :, 4 * nkv:].reshape(bs, ts, 2, N_KV, HEAD_DIM)], axis=1)[:, -wbuf:]
    kv_prompt = jnp.transpose(kvpt.reshape(bp, 4, N_KV, HEAD_DIM, t), (0, 4, 1, 2, 3))
    win_prompt = jnp.transpose(kvwt[:, :, t - wlen:].reshape(bp, 2, N_KV, HEAD_DIM, wlen), (0, 4, 1, 2, 3))
    return (xp, xs,
            kv_prompt,
            kv_s[:, :, :4 * nkv].reshape(bs, ts, 4, N_KV, HEAD_DIM),
            win_prompt,
            win_s,
            jnp.stack(pool_p), jnp.stack(pool_s), jnp.stack(conv_p), jnp.stack(conv_s))
```

```python
import functools

import numpy as np
import jax
import jax.numpy as jnp
from jax import lax
from jax.experimental import pallas as pl
from jax.experimental.pallas import tpu as pltpu

F32 = jnp.float32
BF16 = jnp.bfloat16

POOL_WINDOWS = (2, 4, 8, 16)
POOL_HALO = 16
N_HEADS = 16
HEAD_DIM = 64
N_KV = 4
GROUP = N_HEADS // N_KV
CMP_LEN = 32
CMP_STRIDE = 16
SLC_LEN = 64
SLC_SHIFT = 6
TOP_N = 16
WINDOW = 512
Q_BLOCK = 128
CONV_W = 3
CONV_HALO = 8
PAGE = 128
LN_EPS = 1e-5
NEG = -1e30
N_FORCED = 3
KV_CHUNK = 512
Q_SCALE = HEAD_DIM ** -0.5 * 1.4426950408889634
EXP_HEADROOM = 100.0
VMEM_LIMIT = 56 * 1024 * 1024
MXU_TILE = 256

_NT = (((1,), (1,)), ((), ()))


def _params(*sem):
    return pltpu.CompilerParams(dimension_semantics=sem, vmem_limit_bytes=VMEM_LIMIT)


def _const_spec(shape):
    nd = len(shape)
    return pl.BlockSpec(shape, lambda *_: (0,) * nd, pipeline_mode=pl.Buffered(1))


def _mod_spec(mod, sb):
    ada, layer, row0 = mod
    assert row0 % sb == 0
    return pl.BlockSpec((None, sb, 6, ada.shape[-1]), lambda i, *_: (layer, row0 // sb + i, 0, 0))


def _layer_norm(y, g, b):
    mu = jnp.mean(y, axis=-1, keepdims=True)
    d = y - mu
    var = jnp.mean(d * d, axis=-1, keepdims=True)
    return d * lax.rsqrt(var + LN_EPS) * g + b


def _split3(x):
    hi = x.astype(BF16)
    r1 = x - hi.astype(F32)
    mid = r1.astype(BF16)
    lo = (r1 - mid.astype(F32)).astype(BF16)
    return hi, mid, lo


def _ada_kernel(c_ref, w_ref, b_ref, o_ref):
    c = c_ref[...]
    s = c * jax.nn.sigmoid(c)
    o_ref[0] = jnp.dot(s.astype(BF16), w_ref[0].astype(BF16), preferred_element_type=F32) + b_ref[0]


def _ada(c_all, ada_w, ada_b):
    depth, d, n = ada_w.shape
    r = c_all.shape[0]
    tn = n // 4
    return pl.pallas_call(
        _ada_kernel,
        grid=(depth, n // tn),
        in_specs=[pl.BlockSpec((r, d), lambda l, j: (0, 0)),
                  pl.BlockSpec((1, d, tn), lambda l, j: (l, 0, j)),
                  pl.BlockSpec((1, 1, tn), lambda l, j: (l, 0, j))],
        out_specs=pl.BlockSpec((1, r, tn), lambda l, j: (l, 0, j)),
        out_shape=jax.ShapeDtypeStruct((depth, r, n), F32),
        compiler_params=_params("parallel", "parallel"),
        name="ada_params",
    )(c_all, ada_w, ada_b.reshape(depth, 1, n))


def _pool_kernel(x_ref, mod_ref, prev_ref, pw_ref, ls_ref, g_ref, b_ref, o_ref, ps_ref, ext_ref,
                 *, start_pos, nj, alpha):
    j = pl.program_id(1)
    sb, tm, d = x_ref.shape
    pg = d // len(POOL_WINDOWS)
    h = POOL_HALO
    x = x_ref[...]
    shift = mod_ref[:, 0:1, :]
    scale = mod_ref[:, 1:2, :]
    gate = mod_ref[:, 2:3, :]
    u = x * (1.0 + scale) + shift

    @pl.when(j == 0)
    def _():
        ext_ref[:, 0:h, :] = prev_ref[...]

    ext_ref[:, h:h + tm, :] = u
    pos = start_pos + j * tm + lax.broadcasted_iota(jnp.int32, (1, tm, pg), 1)
    outs = []
    for gi, w in enumerate(POOL_WINDOWS):
        c0 = gi * pg
        acc = ext_ref[:, h:h + tm, c0:c0 + pg]
        for k in range(1, w):
            acc = acc + ext_ref[:, h - k:h - k + tm, c0:c0 + pg]
        cnt = jnp.minimum(pos + 1, w).astype(F32)
        pooled = acc / cnt - u[:, :, c0:c0 + pg]
        outs.append(jnp.dot(pooled.reshape(sb * tm, pg).astype(BF16), pw_ref[gi],
                            preferred_element_type=F32))
    mixed = jnp.concatenate(outs, axis=-1) * ls_ref[...]
    y = alpha * x + (1.0 + gate) * mixed.reshape(sb, tm, d)
    o_ref[...] = _layer_norm(y, g_ref[...], b_ref[...])

    if nj > 1:
        @pl.when(j < nj - 1)
        def _():
            ext_ref[:, 0:h, :] = ext_ref[:, tm:tm + h, :]

    @pl.when(j == nj - 1)
    def _():
        ps_ref[...] = ext_ref[:, tm:tm + h, :]


def _pool_layer(x, mod, prev, pw, ls, g, b, *, sb, tm, start_pos, alpha):
    bsz, t, d = x.shape
    nj = t // tm
    h = POOL_HALO
    kern = functools.partial(_pool_kernel, start_pos=start_pos, nj=nj, alpha=alpha)
    return pl.pallas_call(
        kern,
        grid=(bsz // sb, nj),
        in_specs=[pl.BlockSpec((sb, tm, d), lambda i, j: (i, j, 0)),
                  _mod_spec(mod, sb),
                  pl.BlockSpec((sb, h, d), lambda i, j: (i, 0, 0)),
                  _const_spec(pw.shape), _const_spec(ls.shape), _const_spec(g.shape), _const_spec(b.shape)],
        out_specs=[pl.BlockSpec((sb, tm, d), lambda i, j: (i, j, 0)),
                   pl.BlockSpec((sb, h, d), lambda i, j: (i, 0, 0))],
        out_shape=[jax.ShapeDtypeStruct((bsz, t, d), F32), jax.ShapeDtypeStruct((bsz, h, d), F32)],
        scratch_shapes=[pltpu.VMEM((sb, h + tm, d), F32)],
        compiler_params=_params("parallel", "arbitrary"),
        name="pool_layer",
    )(x, mod[0], prev, pw, ls, g, b)


def _ffn_kernel(x_ref, mod_ref, cprev_ref, wup_ref, bup_ref, wc_ref, bc_ref, wdn_ref, g_ref, b_ref,
                o_ref, cs_ref, hext_ref, *, cfs, alpha):
    j = pl.program_id(1)
    sb, tm, d = x_ref.shape
    f = wdn_ref.shape[0]
    hh = CONV_HALO
    x = x_ref[...]
    shift = mod_ref[:, 3:4, :]
    scale = mod_ref[:, 4:5, :]
    gate = mod_ref[:, 5:6, :]
    ub = (x * (1.0 + scale) + shift).reshape(sb * tm, d).astype(BF16)

    @pl.when(j == 0)
    def _():
        cs_ref[...] = cprev_ref[...]

    starts = [sum(cfs[:k]) for k in range(len(cfs))]
    acc = [jnp.zeros((sb * tm, d), F32)]

    def up_proj(k):
        return [jnp.dot(ub, wup_ref[:, p * f + starts[k]:p * f + starts[k] + cfs[k]], preferred_element_type=F32)
                for p in range(2)]

    def conv_gate_down(k, hps):
        cw, f0 = cfs[k], starts[k]
        parts = []
        for p in range(2):
            c0 = p * f + f0
            slot = 2 * (k % 2) + p
            hext_ref[slot, :, 0:hh, 0:cw] = cs_ref[:, :, c0:c0 + cw]
            hext_ref[slot, :, hh:hh + tm, 0:cw] = (hps[p] + bup_ref[:, c0:c0 + cw]).reshape(sb, tm, cw)
            hc = (bc_ref[:, c0:c0 + cw]
                  + hext_ref[slot, :, hh - 2:hh - 2 + tm, 0:cw] * wc_ref[0:1, c0:c0 + cw]
                  + hext_ref[slot, :, hh - 1:hh - 1 + tm, 0:cw] * wc_ref[1:2, c0:c0 + cw]
                  + hext_ref[slot, :, hh:hh + tm, 0:cw] * wc_ref[2:3, c0:c0 + cw])
            cs_ref[:, :, c0:c0 + cw] = hext_ref[slot, :, tm:tm + hh, 0:cw]
            parts.append(hc)
        a, v = parts
        gated = (a * jax.nn.sigmoid(a)) * v
        acc[0] = acc[0] + jnp.dot(gated.reshape(sb * tm, cw).astype(BF16), wdn_ref[f0:f0 + cw, :],
                                  preferred_element_type=F32)

    _pipelined(len(cfs), up_proj, conv_gate_down, ahead=1)
    acc = acc[0]
    y = alpha * x + (1.0 + gate) * acc.reshape(sb, tm, d)
    o_ref[...] = _layer_norm(y, g_ref[...], b_ref[...])


def _ffn_chunks(f):
    if f % MXU_TILE:
        return (f,)
    n = f // MXU_TILE
    k = min(2, n)
    return tuple((n // k + (1 if i < n % k else 0)) * MXU_TILE for i in range(k))


def _ffn_layer(x, mod, cprev, wup, bup, wc, bc, wdn, g, b, *, sb, tm, alpha):
    bsz, t, d = x.shape
    f2 = wup.shape[1]
    hh = CONV_HALO
    cfs = _ffn_chunks(f2 // 2)
    cf = max(cfs)
    kern = functools.partial(_ffn_kernel, cfs=cfs, alpha=alpha)
    return pl.pallas_call(
        kern,
        grid=(bsz // sb, t // tm),
        in_specs=[pl.BlockSpec((sb, tm, d), lambda i, j: (i, j, 0)),
                  _mod_spec(mod, sb),
                  pl.BlockSpec((sb, hh, f2), lambda i, j: (i, 0, 0)),
                  _const_spec(wup.shape), _const_spec(bup.shape), _const_spec(wc.shape),
                  _const_spec(bc.shape), _const_spec(wdn.shape), _const_spec(g.shape), _const_spec(b.shape)],
        out_specs=[pl.BlockSpec((sb, tm, d), lambda i, j: (i, j, 0)),
                   pl.BlockSpec((sb, hh, f2), lambda i, j: (i, 0, 0))],
        out_shape=[jax.ShapeDtypeStruct((bsz, t, d), F32), jax.ShapeDtypeStruct((bsz, hh, f2), F32)],
        scratch_shapes=[pltpu.VMEM((4, sb, hh + tm, cf), F32)],
        compiler_params=_params("parallel", "arbitrary"),
        name="conv_ffn_layer",
    )(x, mod[0], cprev, wup, bup, wc, bc, wdn, g, b)


def _kvproj_prompt_kernel(x_ref, w_ref, wt_ref, kvpt_ref, kvwt_ref, cmp_ref, ks_ref, kw_ref, vst_ref, vwt_ref):
    tm = x_ref.shape[1]
    nkv = N_KV * HEAD_DIM
    xb = x_ref[0].astype(BF16)
    kv = jnp.dot(xb, w_ref[...], preferred_element_type=F32)
    cmp_ref[0] = kv[:, :2 * nkv].astype(BF16)
    ks_ref[0] = kv[:, 2 * nkv:3 * nkv].astype(BF16)
    kw_ref[0] = kv[:, 4 * nkv:5 * nkv].astype(BF16)
    kvt = lax.dot_general(wt_ref[...], xb, _NT, preferred_element_type=F32)
    kvpt_ref[0] = kvt[:4 * nkv]
    kvwt_ref[0] = kvt[4 * nkv:]
    for c in range(tm // Q_BLOCK):
        cols = slice(c * Q_BLOCK, (c + 1) * Q_BLOCK)
        vst_ref[0, c] = kvt[3 * nkv:4 * nkv, cols].astype(BF16)
        vwt_ref[0, c] = kvt[5 * nkv:6 * nkv, cols].astype(BF16)


def _kvproj_prompt(x, w_kv_b, w_kvt_b, *, tm):
    bsz, t, d = x.shape
    nkv = N_KV * HEAD_DIM
    nb = tm // Q_BLOCK
    return pl.pallas_call(
        _kvproj_prompt_kernel,
        grid=(bsz, t // tm),
        in_specs=[pl.BlockSpec((1, tm, d), lambda i, j: (i, j, 0)),
                  _const_spec(w_kv_b.shape), _const_spec(w_kvt_b.shape)],
        out_specs=[pl.BlockSpec((1, 4 * nkv, tm), lambda i, j: (i, 0, j)),
                   pl.BlockSpec((1, 2 * nkv, tm), lambda i, j: (i, 0, j)),
                   pl.BlockSpec((1, tm, 2 * nkv), lambda i, j: (i, j, 0)),
                   pl.BlockSpec((1, tm, nkv), lambda i, j: (i, j, 0)),
                   pl.BlockSpec((1, tm, nkv), lambda i, j: (i, j, 0)),
                   pl.BlockSpec((1, nb, nkv, Q_BLOCK), lambda i, j: (i, j, 0, 0)),
                   pl.BlockSpec((1, nb, nkv, Q_BLOCK), lambda i, j: (i, j, 0, 0))],
        out_shape=[jax.ShapeDtypeStruct((bsz, 4 * nkv, t), F32),
                   jax.ShapeDtypeStruct((bsz, 2 * nkv, t), F32),
                   jax.ShapeDtypeStruct((bsz, t, 2 * nkv), BF16),
                   jax.ShapeDtypeStruct((bsz, t, nkv), BF16),
                   jax.ShapeDtypeStruct((bsz, t, nkv), BF16),
                   jax.ShapeDtypeStruct((bsz, t // Q_BLOCK, nkv, Q_BLOCK), BF16),
                   jax.ShapeDtypeStruct((bsz, t // Q_BLOCK, nkv, Q_BLOCK), BF16)],
        compiler_params=_params("parallel", "parallel"),
        name="kv_proj_prompt",
    )(x, w_kv_b, w_kvt_b)


def _kvproj_sample_kernel(x_ref, w_ref, kv_ref):
    sb, tm, d = x_ref.shape
    xb = x_ref[...].reshape(sb * tm, d).astype(BF16)
    kv_ref[...] = jnp.dot(xb, w_ref[...], preferred_element_type=F32).reshape(sb, tm, -1)


def _kvproj_sample(x, w_kv_b, *, sb):
    bsz, t, d = x.shape
    n = w_kv_b.shape[1]
    return pl.pallas_call(
        _kvproj_sample_kernel,
        grid=(bsz // sb,),
        in_specs=[pl.BlockSpec((sb, t, d), lambda i: (i, 0, 0)), _const_spec(w_kv_b.shape)],
        out_specs=pl.BlockSpec((sb, t, n), lambda i: (i, 0, 0)),
        out_shape=jax.ShapeDtypeStruct((bsz, t, n), F32),
        compiler_params=_params("parallel"),
        name="kv_proj_sample",
    )(x, w_kv_b)


def _compress_kernel(pt_ref, *refs, transposed):
    del pt_ref
    npg = 16
    page_refs = refs[:npg]
    rest = refs[npg:]
    if not transposed:
        look_ref, rest = rest[0], rest[1:]
    (pet_ref, peb_ref, w1_ref, b1_ref, w2_ref, w2t_ref, b2_ref, b2t_ref,
     kc_ref, vc_ref, vct_ref, xs_ref, p_ref) = rest
    nch = npg * PAGE // CMP_STRIDE
    cpp = PAGE // CMP_STRIDE
    gs = nch + 2 * cpp
    hd = HEAD_DIM
    hidden = b1_ref.shape[-1]
    pr = lax.broadcasted_iota(jnp.int32, (PAGE, PAGE), 0)
    pc = lax.broadcasted_iota(jnp.int32, (PAGE, PAGE), 1)
    perm = (jnp.bitwise_and(pc, CMP_STRIDE - 1) * cpp + jnp.right_shift(pc, 4) == pr).astype(F32).astype(BF16)

    def permuted(ref):
        if transposed:
            tn = lax.dot_general(perm, ref[...].astype(BF16), _NT, preferred_element_type=F32)
        else:
            tn = jnp.dot(perm, ref[...], preferred_element_type=F32)
        ts = jnp.concatenate([pltpu.roll(tn[:, c * 128:(c + 1) * 128], hd, axis=1)
                              for c in range(tn.shape[1] // 128)], axis=1)
        return tn, ts

    low = lax.broadcasted_iota(jnp.int32, (2 * cpp, 128), 1) < hd

    def scatter(pair_n, pair_s, row0):
        for sl in range(4):
            s = sl // 2
            ga = 2 * (sl % 2)
            cols = slice(sl * 128, (sl + 1) * 128)
            for m in range(CMP_STRIDE // 2):
                r0 = slice(2 * m * cpp, (2 * m + 1) * cpp)
                r1 = slice((2 * m + 1) * cpp, (2 * m + 2) * cpp)
                n0 = jnp.concatenate([t[r0, cols] for t in pair_n], axis=0)
                n1 = jnp.concatenate([t[r1, cols] for t in pair_n], axis=0)
                s0 = jnp.concatenate([t[r0, cols] for t in pair_s], axis=0)
                s1 = jnp.concatenate([t[r1, cols] for t in pair_s], axis=0)
                xs_ref[s, ga * gs + row0:ga * gs + row0 + 2 * cpp, m * 128:(m + 1) * 128] = (
                    jnp.where(low, n0, s1).astype(BF16))
                xs_ref[s, (ga + 1) * gs + row0:(ga + 1) * gs + row0 + 2 * cpp, m * 128:(m + 1) * 128] = (
                    jnp.where(low, s0, n1).astype(BF16))

    for k in range(0, npg, 2):
        ta, tb = permuted(page_refs[k]), permuted(page_refs[k + 1])
        scatter((ta[0], tb[0]), (ta[1], tb[1]), k * cpp)
    if transposed:
        for s in range(2):
            for g in range(N_KV):
                xs_ref[s, g * gs + nch:(g + 1) * gs, :] = jnp.zeros((2 * cpp, xs_ref.shape[-1]), BF16)
    else:
        tl = permuted(look_ref)
        scatter((tl[0], tl[0]), (tl[1], tl[1]), nch)

    nx = N_KV * gs
    for s in range(2):
        tails = []
        for pe_ref in (pet_ref, peb_ref):
            pe = pe_ref[s]
            hi = pe.astype(BF16).astype(F32)
            tails += [hi, pe - hi]
        tails.append(jnp.zeros((12, tails[0].shape[1]), F32))
        xs_ref[s, nx:nx + 16, :] = jnp.concatenate(tails, axis=0).astype(BF16)

    for s in range(2):
        p_ref[...] = jnp.dot(xs_ref[s], w1_ref[s], preferred_element_type=F32)
        cvec = (p_ref[nx:nx + 1, 0:hidden] + p_ref[nx + 1:nx + 2, 0:hidden]
                + p_ref[nx + 2:nx + 3, hidden:] + p_ref[nx + 3:nx + 4, hidden:] + b1_ref[s])
        pre = jnp.concatenate([p_ref[g * gs:g * gs + nch, 0:hidden] + p_ref[g * gs + 1:g * gs + nch + 1, hidden:]
                               for g in range(N_KV)], axis=0) + cvec
        hid = jax.nn.gelu(pre).astype(BF16)
        out = jnp.dot(hid, w2_ref[s], preferred_element_type=F32) + b2_ref[s]
        nat = jnp.concatenate([out[g * nch:(g + 1) * nch, :] for g in range(N_KV)], axis=1)
        if s == 0:
            kc_ref[0] = nat.astype(BF16)
        else:
            vc_ref[0] = nat.astype(BF16)
            outs_t = [lax.dot_general(w2t_ref[s], hid[g * nch:(g + 1) * nch, :], _NT,
                                      preferred_element_type=F32) + b2t_ref[s] for g in range(N_KV)]
            vct_ref[0] = jnp.concatenate(outs_t, axis=0).astype(BF16)


def _compress(pages, ptab, pe, w1, b1, w2, b2, *, transposed):
    nt = ptab.shape[0]
    npg = 16
    nkv = N_KV * HEAD_DIM
    half = CMP_STRIDE * HEAD_DIM
    hidden = w1.shape[-1]
    pet = pe[:, :CMP_STRIDE].reshape(2, 1, half)
    peb = pe[:, CMP_STRIDE:].reshape(2, 1, half)
    w1c = jnp.concatenate([w1[:, :half], w1[:, half:]], axis=2).astype(BF16)
    w2b = w2.astype(BF16)
    w2t = jnp.swapaxes(w2, 1, 2).astype(BF16)
    b1r = b1.reshape(2, 1, hidden)
    b2r = b2.reshape(2, 1, HEAD_DIM)
    b2t = b2.reshape(2, HEAD_DIM, 1)
    nch = npg * PAGE // CMP_STRIDE

    def page_spec(k):
        if transposed:
            return pl.BlockSpec((None, 2 * nkv, PAGE), lambda i, pt: (pt[i, k], 0, 0))
        return pl.BlockSpec((None, PAGE, 2 * nkv), lambda i, pt: (pt[i, k], 0, 0))

    def cs(shape):
        nd = len(shape)
        return pl.BlockSpec(shape, lambda i, pt: (0,) * nd, pipeline_mode=pl.Buffered(1))

    in_specs = [page_spec(k) for k in range(npg)]
    operands = [pages] * npg
    consts = (pet, peb, w1c, b1r, w2b, w2t, b2r, b2t)
    if not transposed:
        in_specs.append(page_spec(npg))
        operands.append(pages)
    in_specs += [cs(a.shape) for a in consts]
    gs = nch + 2 * (PAGE // CMP_STRIDE)
    grid_spec = pltpu.PrefetchScalarGridSpec(
        num_scalar_prefetch=1,
        grid=(nt,),
        in_specs=in_specs,
        out_specs=[pl.BlockSpec((1, nch, nkv), lambda i, pt: (i, 0, 0)),
                   pl.BlockSpec((1, nch, nkv), lambda i, pt: (i, 0, 0)),
                   pl.BlockSpec((1, nkv, nch), lambda i, pt: (i, 0, 0))],
        scratch_shapes=[pltpu.VMEM((2, N_KV * gs + 16, half), BF16),
                        pltpu.VMEM((N_KV * gs + 16, 2 * hidden), F32)],
    )
    return pl.pallas_call(
        functools.partial(_compress_kernel, transposed=transposed),
        grid_spec=grid_spec,
        out_shape=[jax.ShapeDtypeStruct((nt, nch, nkv), BF16),
                   jax.ShapeDtypeStruct((nt, nch, nkv), BF16),
                   jax.ShapeDtypeStruct((nt, nkv, nch), BF16)],
        compiler_params=_params("arbitrary"),
        name="compress",
    )(ptab, *operands, *consts)


def _qproj_prompt_kernel(x_ref, mod_ref, wqt_ref, wgt_ref, qt_ref, gt_ref):
    x = x_ref[0]
    shift = mod_ref[0, 0:1, :]
    scale = mod_ref[0, 1:2, :]
    ub = (x * (1.0 + scale) + shift).astype(BF16)
    qt = lax.dot_general(wqt_ref[...], ub, _NT, preferred_element_type=F32) * Q_SCALE
    qt_ref[0] = qt.astype(BF16)
    gt_ref[0] = jax.nn.sigmoid(lax.dot_general(wgt_ref[...], ub, _NT, preferred_element_type=F32))


def _qproj_prompt(x, mod, wqt, wgt, *, tm):
    bsz, t, d = x.shape
    hq = wqt.shape[0]
    ng = wgt.shape[0]
    return pl.pallas_call(
        _qproj_prompt_kernel,
        grid=(bsz, t // tm),
        in_specs=[pl.BlockSpec((1, tm, d), lambda i, j: (i, j, 0)),
                  _mod_spec(mod, 1),
                  _const_spec(wqt.shape), _const_spec(wgt.shape)],
        out_specs=[pl.BlockSpec((1, hq, tm), lambda i, j: (i, 0, j)),
                   pl.BlockSpec((1, ng, tm), lambda i, j: (i, 0, j))],
        out_shape=[jax.ShapeDtypeStruct((bsz, hq, t), BF16), jax.ShapeDtypeStruct((bsz, ng, t), F32)],
        compiler_params=_params("parallel", "parallel"),
        name="q_proj_prompt",
    )(x, mod[0], wqt, wgt)


def _qproj_sample_kernel(x_ref, mod_ref, wq_ref, wg_ref, q_ref, g_ref):
    sb, tm, d = x_ref.shape
    x = x_ref[...]
    shift = mod_ref[:, 0:1, :]
    scale = mod_ref[:, 1:2, :]
    ub = (x * (1.0 + scale) + shift).reshape(sb * tm, d).astype(BF16)
    q = jnp.dot(ub, wq_ref[...], preferred_element_type=F32) * Q_SCALE
    q_ref[...] = q.reshape(sb, tm, -1)
    for i in range(3):
        gl = jnp.dot(ub, wg_ref[i], preferred_element_type=F32)
        g_ref[:, i] = jax.nn.sigmoid(gl).reshape(sb, tm, -1)


def _qproj_sample(x, mod, wq, wgx, *, sb):
    bsz, t, d = x.shape
    hq = wq.shape[1]
    return pl.pallas_call(
        _qproj_sample_kernel,
        grid=(bsz // sb,),
        in_specs=[pl.BlockSpec((sb, t, d), lambda i: (i, 0, 0)),
                  _mod_spec(mod, sb),
                  _const_spec(wq.shape), _const_spec(wgx.shape)],
        out_specs=[pl.BlockSpec((sb, t, hq), lambda i: (i, 0, 0)),
                   pl.BlockSpec((sb, 3, t, hq), lambda i: (i, 0, 0, 0))],
        out_shape=[jax.ShapeDtypeStruct((bsz, t, hq), F32), jax.ShapeDtypeStruct((bsz, 3, t, hq), F32)],
        compiler_params=_params("parallel"),
        name="q_proj_sample",
    )(x, mod[0], wq, wgx)


def _topk_rows(work, n_sel):
    rid = lax.broadcasted_iota(jnp.int32, work.shape, 1).astype(F32)

    def body(_, wk):
        m = jnp.max(wk, axis=1, keepdims=True)
        idx = jnp.min(jnp.where(wk == m, rid, 1e9), axis=1, keepdims=True)
        return jnp.where(rid == idx, -jnp.inf, wk)

    return lax.fori_loop(0, n_sel, body, work) == -jnp.inf


def _pipelined(n, issue, finish, ahead=2):
    pending = [issue(k) for k in range(min(ahead, n))]
    for k in range(n):
        if k + ahead < n:
            pending.append(issue(k + ahead))
        finish(k, pending.pop(0))


def _attn_prompt_kernel(qt_ref, gt_ref, kc_ref, vct_ref, ovt_ref, eb_ref, ks_ref, vst_ref, kw_ref, vwt_ref,
                        o_ref, selb_ref, oc_ref, ow_ref, acc_ref, m_ref, bad_ref):
    i = pl.program_id(1)
    qb = Q_BLOCK
    hd = HEAD_DIM
    s0 = i * qb
    nc = kc_ref.shape[1]
    ns = ovt_ref.shape[0]
    nq4 = GROUP * qb
    nblk = KV_CHUNK // SLC_LEN
    nvb = KV_CHUNK // qb
    zeros = jnp.zeros((hd, nq4), BF16)
    ones = jnp.ones((16, 1), BF16)
    qpos1 = s0 + lax.broadcasted_iota(jnp.int32, (1, qb), 1)
    win_lo = jnp.maximum(i - WINDOW // qb, 0)
    n_wb = WINDOW // qb + 1
    nwk = n_wb * qb

    def tile4(a):
        return jnp.concatenate([a] * GROUP, axis=1)

    def pair(g):
        return slice((g // 2) * 128, (g // 2 + 1) * 128)

    def grow(g):
        return slice(g * hd, (g + 1) * hd)

    def qpad(g):
        qg = jnp.concatenate([qt_ref[0, (g * GROUP + r) * hd:(g * GROUP + r + 1) * hd, :] for r in range(GROUP)],
                             axis=1)
        return jnp.concatenate([qg, zeros] if g % 2 == 0 else [zeros, qg], axis=0)

    def with_ones(vt):
        return jnp.concatenate([vt, jnp.broadcast_to(ones, (16, vt.shape[1]))], axis=0)

    cend = lax.broadcasted_iota(jnp.int32, (nc, qb), 0) * CMP_STRIDE + (CMP_LEN - 1)
    bias_c = tile4(jnp.where(cend <= qpos1, 0.0, NEG))
    seen_c = tile4((qpos1 >= CMP_LEN - 1).astype(F32))
    jj = lax.broadcasted_iota(jnp.int32, (ns, qb), 0)
    cur = jnp.right_shift(qpos1, SLC_SHIFT)
    forced = (jj == 0) | (jj == cur) | (jj == cur - 1)
    valid = jj * SLC_LEN <= qpos1
    ovt = ovt_ref[...]
    w0 = pl.multiple_of(win_lo * qb, qb)
    dist = qpos1 - (w0 + lax.broadcasted_iota(jnp.int32, (nwk, qb), 0))
    bias_w = tile4(jnp.where((dist >= 0) & (dist < WINDOW), 0.0, NEG))
    few = s0 + qb <= TOP_N * SLC_LEN

    def issue_cw(k):
        g = k % N_KV
        if k < N_KV:
            return jnp.dot(kc_ref[0, :, pair(g)], qpad(g), preferred_element_type=F32) + bias_c
        return jnp.dot(kw_ref[0, pl.ds(w0, nwk), pair(g)], qpad(g), preferred_element_type=F32) + bias_w

    def branches_and_selection(exact):
        imps = [None] * N_KV

        def finish_cw(k, st):
            g = k % N_KV
            cmax = jnp.max(st, axis=0, keepdims=True)
            if k < N_KV:
                vct = jnp.concatenate([vct_ref[0, t, grow(g), :] for t in range(vct_ref.shape[1])], axis=1)
                if exact:
                    p = jnp.exp2(st - cmax)
                    p = p * (seen_c / jnp.sum(p, axis=0, keepdims=True))
                    oc_ref[g] = jnp.dot(vct, p.astype(BF16), preferred_element_type=F32)
                    psum = p[:, 0:qb]
                    for r in range(1, GROUP):
                        psum = psum + p[:, r * qb:(r + 1) * qb]
                    imp = sum(jnp.dot(ovt, part, preferred_element_type=F32) for part in _split3(psum))
                else:
                    y = jnp.dot(jnp.concatenate([with_ones(vct), ovt], axis=0), jnp.exp2(st).astype(BF16),
                                preferred_element_type=F32)
                    inv = jnp.where(seen_c > 0.0, 1.0 / y[hd:hd + 1], 0.0)
                    oc_ref[g] = y[:hd] * inv
                    imp = y[hd + 16:, 0:qb] * inv[:, 0:qb]
                    for r in range(1, GROUP):
                        imp = imp + y[hd + 16:, r * qb:(r + 1) * qb] * inv[:, r * qb:(r + 1) * qb]
                    low = jnp.where(seen_c > 0.0, -EXP_HEADROOM - cmax, -jnp.inf)
                    bad_ref[g] = jnp.maximum(bad_ref[g], jnp.maximum(cmax - EXP_HEADROOM, low))
                imps[g] = jnp.where(forced, -jnp.inf, jnp.where(valid, imp, NEG))
            else:
                vwt = jnp.concatenate([vwt_ref[0, win_lo + t, grow(g), :] for t in range(n_wb)], axis=1)
                pw = jnp.exp2(st - cmax) if exact else jnp.exp2(st)
                aw = jnp.dot(with_ones(vwt), pw.astype(BF16), preferred_element_type=F32)
                ow_ref[g] = aw[:hd] * (1.0 / aw[hd:hd + 1])
                if not exact:
                    bad_ref[g] = jnp.maximum(bad_ref[g], jnp.maximum(cmax - EXP_HEADROOM, -EXP_HEADROOM - cmax))

        _pipelined(2 * N_KV, issue_cw, finish_cw, ahead=1 if exact else 2)

        n_pick = jnp.where(few, 0, min(TOP_N, ns) - N_FORCED)
        work = jnp.stack(imps)
        half = ns // 2

        def search_lower_half():
            low = jnp.where(_topk_rows(work[:, :half], n_pick), 0.0, NEG)
            return jnp.concatenate([low, jnp.full((N_KV, ns - half, qb), NEG, F32)], axis=1)

        bias = lax.cond(s0 + qb <= half * SLC_LEN, search_lower_half,
                        lambda: jnp.where(_topk_rows(work, n_pick), 0.0, NEG))
        selb_ref[...] = jnp.where(few & valid, 0.0, bias)

    nextra = -(-(nblk + 1) // 16) * 16
    zpad = jnp.zeros((128 - nextra, nq4), BF16)


    def scores(c, g, k0, cb, rows):
        kaug = jnp.concatenate([ks_ref[0, pl.ds(k0, rows), pair(g)], eb_ref[0:rows, :]], axis=1)
        sb = tile4(selb_ref[g, pl.ds(pl.multiple_of(c * nblk, nblk), nblk), :])
        sbp = jnp.concatenate([sb, -m_ref[g], jnp.zeros((nextra - nblk - 1, nq4), F32)], axis=0).astype(BF16)
        qaug = jnp.concatenate([qpad(g), sbp, zpad], axis=0)
        sc = jnp.dot(kaug, qaug, preferred_element_type=F32)
        return sc if cb is None else sc + cb

    def vt_ext(c, g, rows):
        return with_ones(jnp.concatenate([vst_ref[0, c * nvb + t, grow(g), :] for t in range(rows // qb)], axis=1))

    def next_ref(c, r_old, cmax):
        floor = jnp.where(c == 0, -jnp.inf, 0.0)
        return (r_old + jnp.maximum(cmax, floor)).astype(BF16).astype(F32)

    def chunk(c, causal, exact, rows=KV_CHUNK):
        k0 = pl.multiple_of(c * KV_CHUNK, KV_CHUNK)
        cb = None
        if causal:
            kpos = k0 + lax.broadcasted_iota(jnp.int32, (rows, qb), 0)
            cb = tile4(jnp.where(kpos <= qpos1, 0.0, NEG))

        def finish(g, sc):
            cmax = jnp.max(sc, axis=0, keepdims=True)
            r_old = m_ref[g]
            r_new = next_ref(c, r_old, cmax)
            if exact:
                delta = r_new - r_old
                pv = jnp.dot(vt_ext(c, g, rows), jnp.exp2(sc - delta).astype(BF16), preferred_element_type=F32)
                keep = jnp.exp2(-jnp.maximum(delta, 0.0))
                acc_ref[g] = keep * acc_ref[g] + pv
            else:
                pv = jnp.dot(vt_ext(c, g, rows), jnp.exp2(sc).astype(BF16), preferred_element_type=F32)
                acc_ref[g] = (acc_ref[g] + pv) * jnp.exp2(r_old - r_new)
                low = jnp.where(c == 0, -EXP_HEADROOM - cmax, -jnp.inf)
                bad_ref[g] = jnp.maximum(bad_ref[g], jnp.maximum(cmax - EXP_HEADROOM, low))
            m_ref[g] = r_new

        _pipelined(N_KV, lambda g: scores(c, g, k0, cb, rows), finish, ahead=1 if exact else 2)

    n_full = s0 // KV_CHUNK
    n_tail = (s0 - n_full * KV_CHUNK) // qb

    def sweep(exact):
        m_ref[...] = jnp.zeros(m_ref.shape, F32)
        acc_ref[...] = jnp.zeros(acc_ref.shape, F32)

        def body(c, carry):
            chunk(c, False, exact)
            return carry

        lax.fori_loop(0, n_full, body, 0)
        for r in range(KV_CHUNK // qb):
            @pl.when(n_tail == r)
            def _(r=r):
                chunk(n_full, True, exact, rows=(r + 1) * qb)

    bad_ref[...] = jnp.full(bad_ref.shape, -jnp.inf, F32)
    branches_and_selection(False)
    sweep(False)

    @pl.when(jnp.max(bad_ref[...]) > 0.0)
    def _():
        branches_and_selection(True)
        sweep(True)

    heads_out = []
    for g in range(N_KV):
        acc = acc_ref[g]
        o_s = acc[:hd] * (1.0 / acc[hd:hd + 1])
        o_c = oc_ref[g]
        o_w = ow_ref[g]
        for r in range(GROUP):
            h = g * GROUP + r
            ls = slice(r * qb, (r + 1) * qb)
            heads_out.append(gt_ref[0, 3 * h:3 * h + 1, :] * o_c[:, ls]
                             + gt_ref[0, 3 * h + 1:3 * h + 2, :] * o_s[:, ls]
                             + gt_ref[0, 3 * h + 2:3 * h + 3, :] * o_w[:, ls])
    out_t = jnp.concatenate(heads_out, axis=0)
    o_ref[0] = out_t.T.astype(BF16)


def _attn_prompt(qt, gt, kc, vct, ovt, eb, ks, vst, kw, vwt):
    bsz, hq, t = qt.shape
    ng = gt.shape[1]
    qb = Q_BLOCK
    nq4 = GROUP * qb

    def per_b(shape):
        nd = len(shape)
        return pl.BlockSpec((1,) + tuple(shape[1:]), lambda b, i: (b,) + (0,) * (nd - 1),
                            pipeline_mode=pl.Buffered(1))

    return pl.pallas_call(
        _attn_prompt_kernel,
        grid=(bsz, t // qb),
        in_specs=[pl.BlockSpec((1, hq, qb), lambda b, i: (b, 0, i)),
                  pl.BlockSpec((1, ng, qb), lambda b, i: (b, 0, i)),
                  per_b(kc.shape), per_b(vct.shape), _const_spec(ovt.shape), _const_spec(eb.shape),
                  per_b(ks.shape), per_b(vst.shape), per_b(kw.shape), per_b(vwt.shape)],
        out_specs=pl.BlockSpec((1, qb, hq), lambda b, i: (b, i, 0)),
        out_shape=jax.ShapeDtypeStruct((bsz, t, hq), BF16),
        scratch_shapes=[pltpu.VMEM((N_KV, t // SLC_LEN, qb), F32),
                        pltpu.VMEM((N_KV, HEAD_DIM, nq4), F32),
                        pltpu.VMEM((N_KV, HEAD_DIM, nq4), F32),
                        pltpu.VMEM((N_KV, HEAD_DIM + 16, nq4), F32),
                        pltpu.VMEM((N_KV, 1, nq4), F32),
                        pltpu.VMEM((N_KV, 1, nq4), F32)],
        compiler_params=_params("parallel", "arbitrary"),
        name="nsa_prompt",
    )(qt, gt, kc, vct, ovt, eb, ks, vst, kw, vwt)


def _topk_cols(work, n_sel):
    rid = lax.broadcasted_iota(jnp.int32, work.shape, 0).astype(F32)
    for _ in range(n_sel):
        m = jnp.max(work, axis=0, keepdims=True)
        idx = jnp.min(jnp.where(work == m, rid, 1e9), axis=0, keepdims=True)
        work = jnp.where(rid == idx, -jnp.inf, work)
    return jnp.where(work == -jnp.inf, 1.0, 0.0)


def _softmax_lanes(s):
    p = jnp.exp2(s - jnp.max(s, axis=1, keepdims=True))
    return p, jnp.sum(p, axis=1, keepdims=True)


def _heads_to_lanes(o, tq):
    hd = HEAD_DIM
    nkv = N_KV * hd
    lane = lax.broadcasted_iota(jnp.int32, (tq, nkv), 1)
    pieces = []
    for h in range(N_HEADS):
        g = h // GROUP
        blk = o[h * tq:(h + 1) * tq, :]
        blk = jnp.where((lane >= g * hd) & (lane < (g + 1) * hd), blk, 0.0)
        dst = (h * hd) % nkv
        shift = (dst - g * hd) % nkv
        pieces.append(pltpu.roll(blk, shift, axis=1) if shift else blk)
    cols = []
    per = nkv // hd
    for c in range(N_HEADS // per):
        slab = pieces[c * per]
        for k in range(1, per):
            slab = slab + pieces[c * per + k]
        cols.append(slab)
    return jnp.concatenate(cols, axis=1)


def _attn_sample_kernel(pt_ref, *refs, past_len, n_ns, spb):
    del pt_ref
    npg = past_len // PAGE
    page_refs = refs[:npg * spb]
    (q_ref, g_ref, kc_ref, vc_ref, kvn_ref, win_ref, ovt_ref, ex_ref,
     o_ref, kbuf, vbuf, kwbuf, vwbuf) = refs[npg * spb:]
    hd = HEAD_DIM
    nkv = N_KV * hd
    tq = q_ref.shape[1]
    nrow = N_HEADS * tq
    ngt = N_KV * tq
    nwin = win_ref.shape[2]
    nks = kbuf.shape[2]
    nkw = kwbuf.shape[2]
    nc = kc_ref.shape[1]
    seqs = range(spb)

    ztail = jnp.zeros((PAGE - tq, nkv), F32)
    eye = (lax.broadcasted_iota(jnp.int32, (nkv, nkv), 0)
           == lax.broadcasted_iota(jnp.int32, (nkv, nkv), 1)).astype(F32).astype(BF16)
    for s in seqs:
        for k in range(npg):
            pg = page_refs[s * npg + k]
            kbuf[s, :, k * PAGE:(k + 1) * PAGE] = pg[0:nkv, :].astype(BF16)
            vbuf[s, :, k * PAGE:(k + 1) * PAGE] = pg[nkv:2 * nkv, :].astype(BF16)
        kwbuf[s, :, 0:nwin] = win_ref[s, 0:nkv, :].astype(BF16)
        vwbuf[s, :, 0:nwin] = win_ref[s, nkv:2 * nkv, :].astype(BF16)
        kvn = kvn_ref[s]

        def new_t(slot, kvn=kvn):
            rows = jnp.concatenate([kvn[:, slot * nkv:(slot + 1) * nkv], ztail], axis=0).astype(BF16)
            return lax.dot_general(eye, rows, _NT, preferred_element_type=F32).astype(BF16)

        kbuf[s, :, past_len:past_len + PAGE] = new_t(2)
        vbuf[s, :, past_len:past_len + PAGE] = new_t(3)
        kwbuf[s, :, nwin:nwin + PAGE] = new_t(4)
        vwbuf[s, :, nwin:nwin + PAGE] = new_t(5)

    lane = lax.broadcasted_iota(jnp.int32, (tq, nkv), 1)
    qbd = []
    for s in seqs:
        q = q_ref[s]
        qrows = []
        for h in range(N_HEADS):
            g, r = divmod(h, GROUP)
            slab = q[:, g * nkv:(g + 1) * nkv]
            shift = ((g - r) * hd) % nkv
            moved = pltpu.roll(slab, shift, axis=1) if shift else slab
            qrows.append(jnp.where((lane >= g * hd) & (lane < (g + 1) * hd), moved, 0.0))
        qbd.append(jnp.concatenate(qrows, axis=0).astype(BF16))

    tpos = jnp.bitwise_and(lax.broadcasted_iota(jnp.int32, (nrow, 1), 0), tq - 1)
    qpos = past_len + tpos

    s_c = [lax.dot_general(qbd[s], kc_ref[s], _NT, preferred_element_type=F32) for s in seqs]
    s_w = [jnp.dot(qbd[s], kwbuf[s], preferred_element_type=F32) for s in seqs]
    s_s = [jnp.dot(qbd[s], kbuf[s], preferred_element_type=F32) for s in seqs]

    cend = lax.broadcasted_iota(jnp.int32, (nrow, nc), 1) * CMP_STRIDE + (CMP_LEN - 1)
    mc = cend <= qpos
    nsr = -(-n_ns // 8) * 8
    jj = lax.broadcasted_iota(jnp.int32, (nsr, ngt), 0)
    qp2 = past_len + jnp.bitwise_and(lax.broadcasted_iota(jnp.int32, (1, ngt), 1), tq - 1)
    cur = jnp.right_shift(qp2, SLC_SHIFT)
    forced = (jj == 0) | (jj == cur) | (jj == cur - 1)
    valid = (jj * SLC_LEN <= qp2) & (jj < n_ns)
    o_c, imp = [], []
    for s in seqs:
        p_c, l_c = _softmax_lanes(jnp.where(mc, s_c[s], NEG))
        p_c = jnp.where(mc, p_c * (1.0 / l_c), 0.0)
        o_c.append(jnp.dot(p_c.astype(BF16), vc_ref[s], preferred_element_type=F32))
        p4 = p_c.reshape(N_KV, GROUP, tq, nc)
        psum = p4[:, 0]
        for r in range(1, GROUP):
            psum = psum + p4[:, r]
        psum = psum.reshape(ngt, nc)
        it = sum(lax.dot_general(ovt_ref[...], part, _NT, preferred_element_type=F32)
                 for part in _split3(psum))
        imp.append(jnp.where(forced, -jnp.inf, jnp.where(valid, it[:nsr], NEG)))

    jw = lax.broadcasted_iota(jnp.int32, (nrow, nkw), 1)
    kwpos = past_len + tq - (nwin + tq) + jw
    dist = qpos - kwpos
    mw = (dist >= 0) & (dist < WINDOW) & (kwpos >= 0) & (jw < nwin + tq)
    o_w = []
    for s in seqs:
        p_w, l_w = _softmax_lanes(jnp.where(mw, s_w[s], NEG))
        o_w.append(lax.dot_general(p_w.astype(BF16), vwbuf[s], _NT, preferred_element_type=F32) * (1.0 / l_w))

    nblk = ex_ref.shape[0]
    selk = []
    for s in seqs:
        sel_t = _topk_cols(imp[s], min(TOP_N, n_ns) - N_FORCED)
        sel_t = jnp.concatenate([sel_t, jnp.zeros((nblk - nsr, ngt), F32)], axis=0)
        sel_t = jnp.concatenate([sel_t, jnp.zeros((nblk, nblk - ngt), F32)], axis=1)
        sel = sel_t.T[:ngt]
        sk = jnp.dot(sel.astype(BF16), ex_ref[...], preferred_element_type=F32)
        selk.append(jnp.broadcast_to(sk.reshape(N_KV, 1, tq, nks), (N_KV, GROUP, tq, nks)).reshape(nrow, nks))

    kpos = lax.broadcasted_iota(jnp.int32, (nrow, nks), 1)
    causal = kpos <= qpos
    o_s = []
    for s in seqs:
        p_s, l_s = _softmax_lanes(jnp.where((selk[s] > 0.5) & causal, s_s[s], NEG))
        o_s.append(lax.dot_general(p_s.astype(BF16), vbuf[s], _NT, preferred_element_type=F32) * (1.0 / l_s))

    for s in seqs:
        o_ref[s] = (g_ref[s, 0] * _heads_to_lanes(o_c[s], tq) + g_ref[s, 1] * _heads_to_lanes(o_s[s], tq)
                    + g_ref[s, 2] * _heads_to_lanes(o_w[s], tq))


def _attn_sample(pages, page_table, q, gx, kc, vc, kvn, win, ovt, ex, *, past_len, spb):
    bsz, tq, hq = q.shape
    npg = past_len // PAGE
    nkv = N_KV * HEAD_DIM
    nwin = win.shape[2]
    n_ns = -(-(past_len + tq) // SLC_LEN)
    nks = past_len + PAGE
    nkw = nwin + PAGE
    kern = functools.partial(_attn_sample_kernel, past_len=past_len, n_ns=n_ns, spb=spb)

    def page_spec(s, k):
        return pl.BlockSpec((None, 2 * nkv, PAGE), lambda b, pt: (pt[b * spb + s, k], 1, 0))

    def cs(shape):
        nd = len(shape)
        return pl.BlockSpec(shape, lambda b, pt: (0,) * nd, pipeline_mode=pl.Buffered(1))

    def per_b(shape):
        nd = len(shape)
        return pl.BlockSpec((spb,) + tuple(shape[1:]), lambda b, pt: (b,) + (0,) * (nd - 1))

    in_specs = [page_spec(s, k) for s in range(spb) for k in range(npg)]
    in_specs += [per_b(q.shape), per_b(gx.shape), per_b(kc.shape), per_b(vc.shape), per_b(kvn.shape),
                 per_b(win.shape), cs(ovt.shape), cs(ex.shape)]
    grid_spec = pltpu.PrefetchScalarGridSpec(
        num_scalar_prefetch=1,
        grid=(bsz // spb,),
        in_specs=in_specs,
        out_specs=pl.BlockSpec((spb, tq, hq), lambda b, pt: (b, 0, 0)),
        scratch_shapes=[pltpu.VMEM((spb, nkv, nks), BF16), pltpu.VMEM((spb, nkv, nks), BF16),
                        pltpu.VMEM((spb, nkv, nkw), BF16), pltpu.VMEM((spb, nkv, nkw), BF16)],
    )
    return pl.pallas_call(
        kern,
        grid_spec=grid_spec,
        out_shape=jax.ShapeDtypeStruct((bsz, tq, hq), F32),
        compiler_params=_params("arbitrary"),
        name="nsa_sample",
    )(page_table, *([pages] * (npg * spb)), q, gx, kc, vc, kvn, win, ovt, ex)


def _oproj_kernel(o_ref, x_ref, mod_ref, wo_ref, g_ref, b_ref, out_ref, *, alpha):
    sb, tm, d = x_ref.shape
    ob = o_ref[...].reshape(sb * tm, -1).astype(BF16)
    mix = jnp.dot(ob, wo_ref[...], preferred_element_type=F32).reshape(sb, tm, d)
    gate = mod_ref[:, 2:3, :]
    y = alpha * x_ref[...] + (1.0 + gate) * mix
    out_ref[...] = _layer_norm(y, g_ref[...], b_ref[...])


def _oproj(o, x, mod, wo, g, b, *, sb, tm, alpha):
    bsz, t, d = x.shape
    hq = o.shape[-1]
    kern = functools.partial(_oproj_kernel, alpha=alpha)
    return pl.pallas_call(
        kern,
        grid=(bsz // sb, t // tm),
        in_specs=[pl.BlockSpec((sb, tm, hq), lambda i, j: (i, j, 0)),
                  pl.BlockSpec((sb, tm, d), lambda i, j: (i, j, 0)),
                  _mod_spec(mod, sb),
                  _const_spec(wo.shape), _const_spec(g.shape), _const_spec(b.shape)],
        out_specs=pl.BlockSpec((sb, tm, d), lambda i, j: (i, j, 0)),
        out_shape=jax.ShapeDtypeStruct((bsz, t, d), F32),
        compiler_params=_params("parallel", "parallel"),
        name="o_proj",
    )(o, x, mod[0], wo, g, b)


def _overlap(n_c, n_s):
    ci = np.arange(n_c)[:, None] * CMP_STRIDE
    sj = np.arange(n_s)[None, :] * SLC_LEN
    return ((ci <= sj + SLC_LEN - 1) & (ci + CMP_LEN - 1 >= sj)).astype(np.float32)


def kernel(x_prompt, x_sample, cache_kv, state_kv_win, state_pool, state_conv, page_table, c_prompt, c_sample, ada_w, ada_b, ln_g, ln_b, pool_w, pool_ls, ffn_w_up, ffn_b_up, ffn_w_conv, ffn_b_conv, ffn_w_down, w_kv, cmp_pe, cmp_w1, cmp_b1, cmp_w2, cmp_b2, nsa_w_qg, nsa_w_o):
    bp, t, d = x_prompt.shape
    bs, ts, _ = x_sample.shape
    depth = ada_w.shape[0]
    n_a = pool_w.shape[0]
    f2 = ffn_w_up.shape[-1]
    f = f2 // 2
    past_len = page_table.shape[1] * PAGE
    nkv = N_KV * HEAD_DIM
    hq = N_HEADS * HEAD_DIM
    alpha = float((2 * depth) ** 0.25)
    assert t % (16 * PAGE) == 0 and t >= WINDOW + Q_BLOCK
    assert ts == 8 and past_len == 16 * PAGE and state_kv_win.shape[1] == WINDOW
    assert f % 128 == 0

    tm_p = 512
    sb_pool = min(bs, 64)
    sb_ffn = min(bs, 16)

    ada = _ada(jnp.concatenate([c_sample, c_prompt], axis=0), ada_w, ada_b)
    ada = ada.reshape(depth, bs + bp, 6, d)
    mod_s = [(ada, l, 0) for l in range(depth)]
    mod_p = [(ada, l, bs) for l in range(depth)]
    assert past_len >= 2 * SLC_LEN and -(-(past_len + ts) // SLC_LEN) > TOP_N

    pool_wb = pool_w.astype(BF16)
    wup_b = ffn_w_up.astype(BF16)
    wdn_b = ffn_w_down.astype(BF16)
    w_kv_b = w_kv.astype(BF16)
    w_kvt_b = w_kv.T.astype(BF16)
    wq_b = nsa_w_qg[:, :, :hq].astype(BF16)
    wqt_b = jnp.swapaxes(nsa_w_qg[:, :, :hq], 1, 2).astype(BF16)
    wgt_b = jnp.swapaxes(nsa_w_qg[:, :, hq:], 1, 2).astype(BF16)
    wg = nsa_w_qg[:, :, hq:].reshape(-1, d, N_HEADS, 3)
    wgx_b = jnp.repeat(jnp.moveaxis(wg, 3, 1), HEAD_DIM, axis=3).astype(BF16)
    wo_b = nsa_w_o.astype(BF16)

    def vec(a):
        return a.reshape(1, -1)

    pool0 = jnp.zeros((bp, POOL_HALO, d), F32)
    conv0 = jnp.zeros((bp, CONV_HALO, f2), F32)
    pool_prev_s = jnp.pad(state_pool, ((0, 0), (0, 0), (POOL_HALO - state_pool.shape[2], 0), (0, 0)))
    conv_prev_s = jnp.pad(state_conv, ((0, 0), (0, 0), (CONV_HALO - state_conv.shape[2], 0), (0, 0)))

    xp, xs = x_prompt, x_sample
    pool_p, pool_s, conv_p, conv_s = [], [], [], []
    ctx_p = ctx_s = None
    kv_p = kv_s = None

    for l in range(depth):
        g1, b1, g2, b2 = vec(ln_g[l, 0]), vec(ln_b[l, 0]), vec(ln_g[l, 1]), vec(ln_b[l, 1])
        if l < n_a:
            xp, st = _pool_layer(xp, mod_p[l], pool0, pool_wb[l], vec(pool_ls[l]), g1, b1,
                                 sb=1, tm=tm_p, start_pos=0, alpha=alpha)
            pool_p.append(st[:, 1:])
            xs, st = _pool_layer(xs, mod_s[l], pool_prev_s[l], pool_wb[l], vec(pool_ls[l]), g1, b1,
                                 sb=sb_pool, tm=ts, start_pos=past_len, alpha=alpha)
            pool_s.append(st[:, 1:])
        else:
            jn = l - n_a
            qt, gt = _qproj_prompt(xp, mod_p[l], wqt_b[jn], wgt_b[jn], tm=tm_p)
            op = _attn_prompt(qt, gt, *ctx_p)
            xp = _oproj(op, xp, mod_p[l], wo_b[jn], g1, b1, sb=1, tm=tm_p, alpha=alpha)
            qs, gs = _qproj_sample(xs, mod_s[l], wq_b[jn], wgx_b[jn], sb=sb_pool)
            os_ = _attn_sample(ctx_s[0], page_table, qs, gs, *ctx_s[1:], past_len=past_len,
                               spb=2 if bs % 2 == 0 else 1)
            xs = _oproj(os_, xs, mod_s[l], wo_b[jn], g1, b1, sb=sb_pool, tm=ts, alpha=alpha)

        xp, st = _ffn_layer(xp, mod_p[l], conv0, wup_b[l], vec(ffn_b_up[l]), ffn_w_conv[l], vec(ffn_b_conv[l]),
                            wdn_b[l], g2, b2, sb=1, tm=tm_p, alpha=alpha)
        conv_p.append(st[:, CONV_HALO - (CONV_W - 1):])
        xs, st = _ffn_layer(xs, mod_s[l], conv_prev_s[l], wup_b[l], vec(ffn_b_up[l]), ffn_w_conv[l],
                            vec(ffn_b_conv[l]), wdn_b[l], g2, b2, sb=sb_ffn, tm=ts, alpha=alpha)
        conv_s.append(st[:, CONV_HALO - (CONV_W - 1):])

        if l == n_a - 1:
            kvpt, kvwt, cmp_rows, ks, kw, vst, vwt = _kvproj_prompt(xp, w_kv_b, w_kvt_b, tm=tm_p)
            kv_p = (kvpt, kvwt)
            tiles = t // (16 * PAGE)
            ppb = t // PAGE
            base = (np.arange(bp)[:, None] * ppb + np.arange(tiles)[None, :] * 16).reshape(-1, 1)
            ids = base + np.arange(17)[None, :]
            last = (np.arange(bp)[:, None] * ppb + ppb - 1).repeat(tiles, axis=1).reshape(-1)
            ids[:, 16] = np.minimum(ids[:, 16], last)
            kc, _, vct = _compress(cmp_rows.reshape(bp * ppb, PAGE, 2 * nkv), jnp.asarray(ids, jnp.int32),
                                   cmp_pe, cmp_w1, cmp_b1, cmp_w2, cmp_b2, transposed=False)
            n_c = tiles * 128
            ovt = jnp.asarray(_overlap(n_c, t // SLC_LEN).T, BF16)
            eb = (np.arange(KV_CHUNK)[:, None] // SLC_LEN == np.arange(128)[None, :]).astype(np.float32)
            eb[:, KV_CHUNK // SLC_LEN] = 1.0
            ctx_p = (kc.reshape(bp, n_c, nkv), vct.reshape(bp, tiles, nkv, 128), ovt, jnp.asarray(eb, BF16),
                     ks, vst, kw, vwt)

            kvs = _kvproj_sample(xs, w_kv_b, sb=sb_pool)
            kv_s = kvs
            assert (past_len + ts - CMP_LEN) // CMP_STRIDE + 1 == past_len // CMP_STRIDE - 1
            pages_t = jnp.transpose(cache_kv, (0, 2, 3, 4, 1)).reshape(cache_kv.shape[0], 4 * nkv, PAGE)
            kc_s, vc_s, _ = _compress(pages_t, page_table.astype(jnp.int32), cmp_pe, cmp_w1, cmp_b1, cmp_w2, cmp_b2,
                                      transposed=True)
            n_ns = -(-(past_len + ts) // SLC_LEN)
            ov_s = np.zeros((128, 128), np.float32)
            n_cs = (past_len + ts - CMP_LEN) // CMP_STRIDE + 1
            ov_s[:n_cs, :n_ns] = _overlap(n_cs, n_ns)
            nks = past_len + PAGE
            ex = (np.arange(nks)[None, :] // SLC_LEN == np.arange(128)[:, None]).astype(np.float32)
            win_t = jnp.transpose(state_kv_win, (0, 2, 3, 4, 1)).reshape(bs, 2 * nkv, state_kv_win.shape[1])
            ctx_s = (pages_t, kc_s, vc_s, kvs, win_t, jnp.asarray(ov_s.T, BF16), jnp.asarray(ex, BF16))

    kvpt, kvwt = kv_p
    wlen = min(WINDOW, t)
    wbuf = state_kv_win.shape[1]
    win_s = jnp.concatenate([state_kv_win, kv_s[:, :, 4 * nkv:].reshape(bs, ts, 2, N_KV, HEAD_DIM)], axis=1)[:, -wbuf:]
    kv_prompt = jnp.transpose(kvpt.reshape(bp, 4, N_KV, HEAD_DIM, t), (0, 4, 1, 2, 3))
    win_prompt = jnp.transpose(kvwt[:, :, t - wlen:].reshape(bp, 2, N_KV, HEAD_DIM, wlen), (0, 4, 1, 2, 3))
    return (xp, xs,
            kv_prompt,
            kv_s[:, :, :4 * nkv].reshape(bs, ts, 4, N_KV, HEAD_DIM),
            win_prompt,
            win_s,
            jnp.stack(pool_p), jnp.stack(pool_s), jnp.stack(conv_p), jnp.stack(conv_s))
```

```python
import functools

import numpy as np
import jax
import jax.numpy as jnp
from jax import lax
from jax.experimental import pallas as pl
from jax.experimental.pallas import tpu as pltpu

F32 = jnp.float32
BF16 = jnp.bfloat16

POOL_WINDOWS = (2, 4, 8, 16)
POOL_HALO = 16
N_HEADS = 16
HEAD_DIM = 64
N_KV = 4
GROUP = N_HEADS // N_KV
CMP_LEN = 32
CMP_STRIDE = 16
SLC_LEN = 64
SLC_SHIFT = 6
TOP_N = 16
WINDOW = 512
Q_BLOCK = 128
CONV_W = 3
CONV_HALO = 8
PAGE = 128
LN_EPS = 1e-5
NEG = -1e30
N_FORCED = 3
KV_CHUNK = 512
Q_SCALE = HEAD_DIM ** -0.5 * 1.4426950408889634
EXP_HEADROOM = 100.0
VMEM_LIMIT = 56 * 1024 * 1024
MXU_TILE = 256

_NT = (((1,), (1,)), ((), ()))


def _params(*sem):
    return pltpu.CompilerParams(dimension_semantics=sem, vmem_limit_bytes=VMEM_LIMIT)


def _const_spec(shape):
    nd = len(shape)
    return pl.BlockSpec(shape, lambda *_: (0,) * nd, pipeline_mode=pl.Buffered(1))


def _mod_spec(mod, sb):
    ada, layer, row0 = mod
    assert row0 % sb == 0
    return pl.BlockSpec((None, sb, 6, ada.shape[-1]), lambda i, *_: (layer, row0 // sb + i, 0, 0))


def _layer_norm(y, g, b):
    mu = jnp.mean(y, axis=-1, keepdims=True)
    d = y - mu
    var = jnp.mean(d * d, axis=-1, keepdims=True)
    return d * lax.rsqrt(var + LN_EPS) * g + b


def _split3(x):
    hi = x.astype(BF16)
    r1 = x - hi.astype(F32)
    mid = r1.astype(BF16)
    lo = (r1 - mid.astype(F32)).astype(BF16)
    return hi, mid, lo


def _ada_kernel(c_ref, w_ref, b_ref, o_ref):
    c = c_ref[...]
    s = c * jax.nn.sigmoid(c)
    o_ref[0] = jnp.dot(s.astype(BF16), w_ref[0].astype(BF16), preferred_element_type=F32) + b_ref[0]


def _ada(c_all, ada_w, ada_b):
    depth, d, n = ada_w.shape
    r = c_all.shape[0]
    tn = n // 4
    return pl.pallas_call(
        _ada_kernel,
        grid=(depth, n // tn),
        in_specs=[pl.BlockSpec((r, d), lambda l, j: (0, 0)),
                  pl.BlockSpec((1, d, tn), lambda l, j: (l, 0, j)),
                  pl.BlockSpec((1, 1, tn), lambda l, j: (l, 0, j))],
        out_specs=pl.BlockSpec((1, r, tn), lambda l, j: (l, 0, j)),
        out_shape=jax.ShapeDtypeStruct((depth, r, n), F32),
        compiler_params=_params("parallel", "parallel"),
        name="ada_params",
    )(c_all, ada_w, ada_b.reshape(depth, 1, n))


def _pool_kernel(x_ref, mod_ref, prev_ref, pw_ref, ls_ref, g_ref, b_ref, o_ref, ps_ref, ext_ref,
                 *, start_pos, nj, alpha):
    j = pl.program_id(1)
    sb, tm, d = x_ref.shape
    pg = d // len(POOL_WINDOWS)
    h = POOL_HALO
    x = x_ref[...]
    shift = mod_ref[:, 0:1, :]
    scale = mod_ref[:, 1:2, :]
    gate = mod_ref[:, 2:3, :]
    u = x * (1.0 + scale) + shift

    @pl.when(j == 0)
    def _():
        ext_ref[:, 0:h, :] = prev_ref[...]

    ext_ref[:, h:h + tm, :] = u
    pos = start_pos + j * tm + lax.broadcasted_iota(jnp.int32, (1, tm, pg), 1)
    outs = []
    for gi, w in enumerate(POOL_WINDOWS):
        c0 = gi * pg
        acc = ext_ref[:, h:h + tm, c0:c0 + pg]
        for k in range(1, w):
            acc = acc + ext_ref[:, h - k:h - k + tm, c0:c0 + pg]
        cnt = jnp.minimum(pos + 1, w).astype(F32)
        pooled = acc / cnt - u[:, :, c0:c0 + pg]
        outs.append(jnp.dot(pooled.reshape(sb * tm, pg).astype(BF16), pw_ref[gi],
                            preferred_element_type=F32))
    mixed = jnp.concatenate(outs, axis=-1) * ls_ref[...]
    y = alpha * x + (1.0 + gate) * mixed.reshape(sb, tm, d)
    o_ref[...] = _layer_norm(y, g_ref[...], b_ref[...])

    if nj > 1:
        @pl.when(j < nj - 1)
        def _():
            ext_ref[:, 0:h, :] = ext_ref[:, tm:tm + h, :]

    @pl.when(j == nj - 1)
    def _():
        ps_ref[...] = ext_ref[:, tm:tm + h, :]


def _pool_layer(x, mod, prev, pw, ls, g, b, *, sb, tm, start_pos, alpha):
    bsz, t, d = x.shape
    nj = t // tm
    h = POOL_HALO
    kern = functools.partial(_pool_kernel, start_pos=start_pos, nj=nj, alpha=alpha)
    return pl.pallas_call(
        kern,
        grid=(bsz // sb, nj),
        in_specs=[pl.BlockSpec((sb, tm, d), lambda i, j: (i, j, 0)),
                  _mod_spec(mod, sb),
                  pl.BlockSpec((sb, h, d), lambda i, j: (i, 0, 0)),
                  _const_spec(pw.shape), _const_spec(ls.shape), _const_spec(g.shape), _const_spec(b.shape)],
        out_specs=[pl.BlockSpec((sb, tm, d), lambda i, j: (i, j, 0)),
                   pl.BlockSpec((sb, h, d), lambda i, j: (i, 0, 0))],
        out_shape=[jax.ShapeDtypeStruct((bsz, t, d), F32), jax.ShapeDtypeStruct((bsz, h, d), F32)],
        scratch_shapes=[pltpu.VMEM((sb, h + tm, d), F32)],
        compiler_params=_params("parallel", "arbitrary"),
        name="pool_layer",
    )(x, mod[0], prev, pw, ls, g, b)


def _ffn_kernel(x_ref, mod_ref, cprev_ref, wup_ref, bup_ref, wc_ref, bc_ref, wdn_ref, g_ref, b_ref,
                o_ref, cs_ref, hext_ref, *, cfs, alpha):
    j = pl.program_id(1)
    sb, tm, d = x_ref.shape
    f = wdn_ref.shape[0]
    hh = CONV_HALO
    x = x_ref[...]
    shift = mod_ref[:, 3:4, :]
    scale = mod_ref[:, 4:5, :]
    gate = mod_ref[:, 5:6, :]
    ub = (x * (1.0 + scale) + shift).reshape(sb * tm, d).astype(BF16)

    @pl.when(j == 0)
    def _():
        cs_ref[...] = cprev_ref[...]

    starts = [sum(cfs[:k]) for k in range(len(cfs))]
    acc = [jnp.zeros((sb * tm, d), F32)]

    def up_proj(k):
        return [jnp.dot(ub, wup_ref[:, p * f + starts[k]:p * f + starts[k] + cfs[k]], preferred_element_type=F32)
                for p in range(2)]

    def conv_gate_down(k, hps):
        cw, f0 = cfs[k], starts[k]
        parts = []
        for p in range(2):
            c0 = p * f + f0
            slot = 2 * (k % 2) + p
            hext_ref[slot, :, 0:hh, 0:cw] = cs_ref[:, :, c0:c0 + cw]
            hext_ref[slot, :, hh:hh + tm, 0:cw] = (hps[p] + bup_ref[:, c0:c0 + cw]).reshape(sb, tm, cw)
            hc = (bc_ref[:, c0:c0 + cw]
                  + hext_ref[slot, :, hh - 2:hh - 2 + tm, 0:cw] * wc_ref[0:1, c0:c0 + cw]
                  + hext_ref[slot, :, hh - 1:hh - 1 + tm, 0:cw] * wc_ref[1:2, c0:c0 + cw]
                  + hext_ref[slot, :, hh:hh + tm, 0:cw] * wc_ref[2:3, c0:c0 + cw])
            cs_ref[:, :, c0:c0 + cw] = hext_ref[slot, :, tm:tm + hh, 0:cw]
            parts.append(hc)
        a, v = parts
        gated = (a * jax.nn.sigmoid(a)) * v
        acc[0] = acc[0] + jnp.dot(gated.reshape(sb * tm, cw).astype(BF16), wdn_ref[f0:f0 + cw, :],
                                  preferred_element_type=F32)

    _pipelined(len(cfs), up_proj, conv_gate_down, ahead=1)
    acc = acc[0]
    y = alpha * x + (1.0 + gate) * acc.reshape(sb, tm, d)
    o_ref[...] = _layer_norm(y, g_ref[...], b_ref[...])


def _ffn_chunks(f):
    if f % MXU_TILE:
        return (f,)
    n = f // MXU_TILE
    k = min(2, n)
    return tuple((n // k + (1 if i < n % k else 0)) * MXU_TILE for i in range(k))


def _ffn_layer(x, mod, cprev, wup, bup, wc, bc, wdn, g, b, *, sb, tm, alpha):
    bsz, t, d = x.shape
    f2 = wup.shape[1]
    hh = CONV_HALO
    cfs = _ffn_chunks(f2 // 2)
    cf = max(cfs)
    kern = functools.partial(_ffn_kernel, cfs=cfs, alpha=alpha)
    return pl.pallas_call(
        kern,
        grid=(bsz // sb, t // tm),
        in_specs=[pl.BlockSpec((sb, tm, d), lambda i, j: (i, j, 0)),
                  _mod_spec(mod, sb),
                  pl.BlockSpec((sb, hh, f2), lambda i, j: (i, 0, 0)),
                  _const_spec(wup.shape), _const_spec(bup.shape), _const_spec(wc.shape),
                  _const_spec(bc.shape), _const_spec(wdn.shape), _const_spec(g.shape), _const_spec(b.shape)],
        out_specs=[pl.BlockSpec((sb, tm, d), lambda i, j: (i, j, 0)),
                   pl.BlockSpec((sb, hh, f2), lambda i, j: (i, 0, 0))],
        out_shape=[jax.ShapeDtypeStruct((bsz, t, d), F32), jax.ShapeDtypeStruct((bsz, hh, f2), F32)],
        scratch_shapes=[pltpu.VMEM((4, sb, hh + tm, cf), F32)],
        compiler_params=_params("parallel", "arbitrary"),
        name="conv_ffn_layer",
    )(x, mod[0], cprev, wup, bup, wc, bc, wdn, g, b)


def _kvproj_prompt_kernel(x_ref, w_ref, wt_ref, kvpt_ref, kvwt_ref, cmp_ref, ks_ref, kw_ref, vst_ref, vwt_ref):
    tm = x_ref.shape[1]
    nkv = N_KV * HEAD_DIM
    xb = x_ref[0].astype(BF16)
    kv = jnp.dot(xb, w_ref[...], preferred_element_type=F32)
    cmp_ref[0] = kv[:, :2 * nkv].astype(BF16)
    ks_ref[0] = kv[:, 2 * nkv:3 * nkv].astype(BF16)
    kw_ref[0] = kv[:, 4 * nkv:5 * nkv].astype(BF16)
    kvt = lax.dot_general(wt_ref[...], xb, _NT, preferred_element_type=F32)
    kvpt_ref[0] = kvt[:4 * nkv]
    kvwt_ref[0] = kvt[4 * nkv:]
    for c in range(tm // Q_BLOCK):
        cols = slice(c * Q_BLOCK, (c + 1) * Q_BLOCK)
        vst_ref[0, c] = kvt[3 * nkv:4 * nkv, cols].astype(BF16)
        vwt_ref[0, c] = kvt[5 * nkv:6 * nkv, cols].astype(BF16)


def _kvproj_prompt(x, w_kv_b, w_kvt_b, *, tm):
    bsz, t, d = x.shape
    nkv = N_KV * HEAD_DIM
    nb = tm // Q_BLOCK
    return pl.pallas_call(
        _kvproj_prompt_kernel,
        grid=(bsz, t // tm),
        in_specs=[pl.BlockSpec((1, tm, d), lambda i, j: (i, j, 0)),
                  _const_spec(w_kv_b.shape), _const_spec(w_kvt_b.shape)],
        out_specs=[pl.BlockSpec((1, 4 * nkv, tm), lambda i, j: (i, 0, j)),
                   pl.BlockSpec((1, 2 * nkv, tm), lambda i, j: (i, 0, j)),
                   pl.BlockSpec((1, tm, 2 * nkv), lambda i, j: (i, j, 0)),
                   pl.BlockSpec((1, tm, nkv), lambda i, j: (i, j, 0)),
                   pl.BlockSpec((1, tm, nkv), lambda i, j: (i, j, 0)),
                   pl.BlockSpec((1, nb, nkv, Q_BLOCK), lambda i, j: (i, j, 0, 0)),
                   pl.BlockSpec((1, nb, nkv, Q_BLOCK), lambda i, j: (i, j, 0, 0))],
        out_shape=[jax.ShapeDtypeStruct((bsz, 4 * nkv, t), F32),
                   jax.ShapeDtypeStruct((bsz, 2 * nkv, t), F32),
                   jax.ShapeDtypeStruct((bsz, t, 2 * nkv), BF16),
                   jax.ShapeDtypeStruct((bsz, t, nkv), BF16),
                   jax.ShapeDtypeStruct((bsz, t, nkv), BF16),
                   jax.ShapeDtypeStruct((bsz, t // Q_BLOCK, nkv, Q_BLOCK), BF16),
                   jax.ShapeDtypeStruct((bsz, t // Q_BLOCK, nkv, Q_BLOCK), BF16)],
        compiler_params=_params("parallel", "parallel"),
        name="kv_proj_prompt",
    )(x, w_kv_b, w_kvt_b)


def _kvproj_sample_kernel(x_ref, w_ref, kv_ref):
    sb, tm, d = x_ref.shape
    xb = x_ref[...].reshape(sb * tm, d).astype(BF16)
    kv_ref[...] = jnp.dot(xb, w_ref[...], preferred_element_type=F32).reshape(sb, tm, -1)


def _kvproj_sample(x, w_kv_b, *, sb):
    bsz, t, d = x.shape
    n = w_kv_b.shape[1]
    return pl.pallas_call(
        _kvproj_sample_kernel,
        grid=(bsz // sb,),
        in_specs=[pl.BlockSpec((sb, t, d), lambda i: (i, 0, 0)), _const_spec(w_kv_b.shape)],
        out_specs=pl.BlockSpec((sb, t, n), lambda i: (i, 0, 0)),
        out_shape=jax.ShapeDtypeStruct((bsz, t, n), F32),
        compiler_params=_params("parallel"),
        name="kv_proj_sample",
    )(x, w_kv_b)


def _compress_kernel(pt_ref, *refs, transposed):
    del pt_ref
    npg = 16
    page_refs = refs[:npg]
    rest = refs[npg:]
    if not transposed:
        look_ref, rest = rest[0], rest[1:]
    (pet_ref, peb_ref, w1_ref, b1_ref, w2_ref, w2t_ref, b2_ref, b2t_ref,
     kc_ref, vc_ref, vct_ref, xs_ref, p_ref) = rest
    nch = npg * PAGE // CMP_STRIDE
    cpp = PAGE // CMP_STRIDE
    gs = nch + 2 * cpp
    hd = HEAD_DIM
    hidden = b1_ref.shape[-1]
    pr = lax.broadcasted_iota(jnp.int32, (PAGE, PAGE), 0)
    pc = lax.broadcasted_iota(jnp.int32, (PAGE, PAGE), 1)
    perm = (jnp.bitwise_and(pc, CMP_STRIDE - 1) * cpp + jnp.right_shift(pc, 4) == pr).astype(F32).astype(BF16)

    def permuted(ref):
        if transposed:
            tn = lax.dot_general(perm, ref[...].astype(BF16), _NT, preferred_element_type=F32)
        else:
            tn = jnp.dot(perm, ref[...], preferred_element_type=F32)
        ts = jnp.concatenate([pltpu.roll(tn[:, c * 128:(c + 1) * 128], hd, axis=1)
                              for c in range(tn.shape[1] // 128)], axis=1)
        return tn, ts

    low = lax.broadcasted_iota(jnp.int32, (2 * cpp, 128), 1) < hd

    def scatter(pair_n, pair_s, row0):
        for sl in range(4):
            s = sl // 2
            ga = 2 * (sl % 2)
            cols = slice(sl * 128, (sl + 1) * 128)
            for m in range(CMP_STRIDE // 2):
                r0 = slice(2 * m * cpp, (2 * m + 1) * cpp)
                r1 = slice((2 * m + 1) * cpp, (2 * m + 2) * cpp)
                n0 = jnp.concatenate([t[r0, cols] for t in pair_n], axis=0)
                n1 = jnp.concatenate([t[r1, cols] for t in pair_n], axis=0)
                s0 = jnp.concatenate([t[r0, cols] for t in pair_s], axis=0)
                s1 = jnp.concatenate([t[r1, cols] for t in pair_s], axis=0)
                xs_ref[s, ga * gs + row0:ga * gs + row0 + 2 * cpp, m * 128:(m + 1) * 128] = (
                    jnp.where(low, n0, s1).astype(BF16))
                xs_ref[s, (ga + 1) * gs + row0:(ga + 1) * gs + row0 + 2 * cpp, m * 128:(m + 1) * 128] = (
                    jnp.where(low, s0, n1).astype(BF16))

    for k in range(0, npg, 2):
        ta, tb = permuted(page_refs[k]), permuted(page_refs[k + 1])
        scatter((ta[0], tb[0]), (ta[1], tb[1]), k * cpp)
    if transposed:
        for s in range(2):
            for g in range(N_KV):
                xs_ref[s, g * gs + nch:(g + 1) * gs, :] = jnp.zeros((2 * cpp, xs_ref.shape[-1]), BF16)
    else:
        tl = permuted(look_ref)
        scatter((tl[0], tl[0]), (tl[1], tl[1]), nch)

    nx = N_KV * gs
    for s in range(2):
        tails = []
        for pe_ref in (pet_ref, peb_ref):
            pe = pe_ref[s]
            hi = pe.astype(BF16).astype(F32)
            tails += [hi, pe - hi]
        tails.append(jnp.zeros((12, tails[0].shape[1]), F32))
        xs_ref[s, nx:nx + 16, :] = jnp.concatenate(tails, axis=0).astype(BF16)

    for s in range(2):
        p_ref[...] = jnp.dot(xs_ref[s], w1_ref[s], preferred_element_type=F32)
        cvec = (p_ref[nx:nx + 1, 0:hidden] + p_ref[nx + 1:nx + 2, 0:hidden]
                + p_ref[nx + 2:nx + 3, hidden:] + p_ref[nx + 3:nx + 4, hidden:] + b1_ref[s])
        pre = jnp.concatenate([p_ref[g * gs:g * gs + nch, 0:hidden] + p_ref[g * gs + 1:g * gs + nch + 1, hidden:]
                               for g in range(N_KV)], axis=0) + cvec
        hid = jax.nn.gelu(pre).astype(BF16)
        out = jnp.dot(hid, w2_ref[s], preferred_element_type=F32) + b2_ref[s]
        nat = jnp.concatenate([out[g * nch:(g + 1) * nch, :] for g in range(N_KV)], axis=1)
        if s == 0:
            kc_ref[0] = nat.astype(BF16)
        else:
            vc_ref[0] = nat.astype(BF16)
            outs_t = [lax.dot_general(w2t_ref[s], hid[g * nch:(g + 1) * nch, :], _NT,
                                      preferred_element_type=F32) + b2t_ref[s] for g in range(N_KV)]
            vct_ref[0] = jnp.concatenate(outs_t, axis=0).astype(BF16)


def _compress(pages, ptab, pe, w1, b1, w2, b2, *, transposed):
    nt = ptab.shape[0]
    npg = 16
    nkv = N_KV * HEAD_DIM
    half = CMP_STRIDE * HEAD_DIM
    hidden = w1.shape[-1]
    pet = pe[:, :CMP_STRIDE].reshape(2, 1, half)
    peb = pe[:, CMP_STRIDE:].reshape(2, 1, half)
    w1c = jnp.concatenate([w1[:, :half], w1[:, half:]], axis=2).astype(BF16)
    w2b = w2.astype(BF16)
    w2t = jnp.swapaxes(w2, 1, 2).astype(BF16)
    b1r = b1.reshape(2, 1, hidden)
    b2r = b2.reshape(2, 1, HEAD_DIM)
    b2t = b2.reshape(2, HEAD_DIM, 1)
    nch = npg * PAGE // CMP_STRIDE

    def page_spec(k):
        if transposed:
            return pl.BlockSpec((None, 2 * nkv, PAGE), lambda i, pt: (pt[i, k], 0, 0))
        return pl.BlockSpec((None, PAGE, 2 * nkv), lambda i, pt: (pt[i, k], 0, 0))

    def cs(shape):
        nd = len(shape)
        return pl.BlockSpec(shape, lambda i, pt: (0,) * nd, pipeline_mode=pl.Buffered(1))

    in_specs = [page_spec(k) for k in range(npg)]
    operands = [pages] * npg
    consts = (pet, peb, w1c, b1r, w2b, w2t, b2r, b2t)
    if not transposed:
        in_specs.append(page_spec(npg))
        operands.append(pages)
    in_specs += [cs(a.shape) for a in consts]
    gs = nch + 2 * (PAGE // CMP_STRIDE)
    grid_spec = pltpu.PrefetchScalarGridSpec(
        num_scalar_prefetch=1,
        grid=(nt,),
        in_specs=in_specs,
        out_specs=[pl.BlockSpec((1, nch, nkv), lambda i, pt: (i, 0, 0)),
                   pl.BlockSpec((1, nch, nkv), lambda i, pt: (i, 0, 0)),
                   pl.BlockSpec((1, nkv, nch), lambda i, pt: (i, 0, 0))],
        scratch_shapes=[pltpu.VMEM((2, N_KV * gs + 16, half), BF16),
                        pltpu.VMEM((N_KV * gs + 16, 2 * hidden), F32)],
    )
    return pl.pallas_call(
        functools.partial(_compress_kernel, transposed=transposed),
        grid_spec=grid_spec,
        out_shape=[jax.ShapeDtypeStruct((nt, nch, nkv), BF16),
                   jax.ShapeDtypeStruct((nt, nch, nkv), BF16),
                   jax.ShapeDtypeStruct((nt, nkv, nch), BF16)],
        compiler_params=_params("arbitrary"),
        name="compress",
    )(ptab, *operands, *consts)


def _qproj_prompt_kernel(x_ref, mod_ref, wqt_ref, wgt_ref, qt_ref, gt_ref):
    x = x_ref[0]
    shift = mod_ref[0, 0:1, :]
    scale = mod_ref[0, 1:2, :]
    ub = (x * (1.0 + scale) + shift).astype(BF16)
    qt = lax.dot_general(wqt_ref[...], ub, _NT, preferred_element_type=F32) * Q_SCALE
    qt_ref[0] = qt.astype(BF16)
    gt_ref[0] = jax.nn.sigmoid(lax.dot_general(wgt_ref[...], ub, _NT, preferred_element_type=F32))


def _qproj_prompt(x, mod, wqt, wgt, *, tm):
    bsz, t, d = x.shape
    hq = wqt.shape[0]
    ng = wgt.shape[0]
    return pl.pallas_call(
        _qproj_prompt_kernel,
        grid=(bsz, t // tm),
        in_specs=[pl.BlockSpec((1, tm, d), lambda i, j: (i, j, 0)),
                  _mod_spec(mod, 1),
                  _const_spec(wqt.shape), _const_spec(wgt.shape)],
        out_specs=[pl.BlockSpec((1, hq, tm), lambda i, j: (i, 0, j)),
                   pl.BlockSpec((1, ng, tm), lambda i, j: (i, 0, j))],
        out_shape=[jax.ShapeDtypeStruct((bsz, hq, t), BF16), jax.ShapeDtypeStruct((bsz, ng, t), F32)],
        compiler_params=_params("parallel", "parallel"),
        name="q_proj_prompt",
    )(x, mod[0], wqt, wgt)


def _qproj_sample_kernel(x_ref, mod_ref, wq_ref, wg_ref, q_ref, g_ref):
    sb, tm, d = x_ref.shape
    x = x_ref[...]
    shift = mod_ref[:, 0:1, :]
    scale = mod_ref[:, 1:2, :]
    ub = (x * (1.0 + scale) + shift).reshape(sb * tm, d).astype(BF16)
    q = jnp.dot(ub, wq_ref[...], preferred_element_type=F32) * Q_SCALE
    q_ref[...] = q.reshape(sb, tm, -1)
    for i in range(3):
        gl = jnp.dot(ub, wg_ref[i], preferred_element_type=F32)
        g_ref[:, i] = jax.nn.sigmoid(gl).reshape(sb, tm, -1)


def _qproj_sample(x, mod, wq, wgx, *, sb):
    bsz, t, d = x.shape
    hq = wq.shape[1]
    return pl.pallas_call(
        _qproj_sample_kernel,
        grid=(bsz // sb,),
        in_specs=[pl.BlockSpec((sb, t, d), lambda i: (i, 0, 0)),
                  _mod_spec(mod, sb),
                  _const_spec(wq.shape), _const_spec(wgx.shape)],
        out_specs=[pl.BlockSpec((sb, t, hq), lambda i: (i, 0, 0)),
                   pl.BlockSpec((sb, 3, t, hq), lambda i: (i, 0, 0, 0))],
        out_shape=[jax.ShapeDtypeStruct((bsz, t, hq), F32), jax.ShapeDtypeStruct((bsz, 3, t, hq), F32)],
        compiler_params=_params("parallel"),
        name="q_proj_sample",
    )(x, mod[0], wq, wgx)


def _topk_rows(work, n_sel):
    rid = lax.broadcasted_iota(jnp.int32, work.shape, 1).astype(F32)

    def body(_, wk):
        m = jnp.max(wk, axis=1, keepdims=True)
        idx = jnp.min(jnp.where(wk == m, rid, 1e9), axis=1, keepdims=True)
        return jnp.where(rid == idx, -jnp.inf, wk)

    return lax.fori_loop(0, n_sel, body, work) == -jnp.inf


def _pipelined(n, issue, finish, ahead=2):
    pending = [issue(k) for k in range(min(ahead, n))]
    for k in range(n):
        if k + ahead < n:
            pending.append(issue(k + ahead))
        finish(k, pending.pop(0))


def _attn_prompt_kernel(qt_ref, gt_ref, kc_ref, vct_ref, ovt_ref, eb_ref, ks_ref, vst_ref, kw_ref, vwt_ref,
                        o_ref, selb_ref, oc_ref, ow_ref, acc_ref, m_ref, bad_ref):
    i = pl.program_id(1)
    qb = Q_BLOCK
    hd = HEAD_DIM
    s0 = i * qb
    nc = kc_ref.shape[1]
    ns = ovt_ref.shape[0]
    nq4 = GROUP * qb
    nblk = KV_CHUNK // SLC_LEN
    nvb = KV_CHUNK // qb
    zeros = jnp.zeros((hd, nq4), BF16)
    ones = jnp.ones((16, 1), BF16)
    qpos1 = s0 + lax.broadcasted_iota(jnp.int32, (1, qb), 1)
    win_lo = jnp.maximum(i - WINDOW // qb, 0)
    n_wb = WINDOW // qb + 1
    nwk = n_wb * qb

    def tile4(a):
        return jnp.concatenate([a] * GROUP, axis=1)

    def pair(g):
        return slice((g // 2) * 128, (g // 2 + 1) * 128)

    def grow(g):
        return slice(g * hd, (g + 1) * hd)

    def qpad(g):
        qg = jnp.concatenate([qt_ref[0, (g * GROUP + r) * hd:(g * GROUP + r + 1) * hd, :] for r in range(GROUP)],
                             axis=1)
        return jnp.concatenate([qg, zeros] if g % 2 == 0 else [zeros, qg], axis=0)

    def with_ones(vt):
        return jnp.concatenate([vt, jnp.broadcast_to(ones, (16, vt.shape[1]))], axis=0)

    cend = lax.broadcasted_iota(jnp.int32, (nc, qb), 0) * CMP_STRIDE + (CMP_LEN - 1)
    bias_c = tile4(jnp.where(cend <= qpos1, 0.0, NEG))
    seen_c = tile4((qpos1 >= CMP_LEN - 1).astype(F32))
    jj = lax.broadcasted_iota(jnp.int32, (ns, qb), 0)
    cur = jnp.right_shift(qpos1, SLC_SHIFT)
    forced = (jj == 0) | (jj == cur) | (jj == cur - 1)
    valid = jj * SLC_LEN <= qpos1
    w0 = pl.multiple_of(win_lo * qb, qb)
    dist = qpos1 - (w0 + lax.broadcasted_iota(jnp.int32, (nwk, qb), 0))
    bias_w = tile4(jnp.where((dist >= 0) & (dist < WINDOW), 0.0, NEG))
    few = s0 + qb <= TOP_N * SLC_LEN

    def branches_and_selection(exact, ncr=nc):
        imps = [None] * N_KV
        ovt = ovt_ref[:, 0:ncr]

        def issue_cw(k):
            g = k % N_KV
            if k < N_KV:
                return jnp.dot(kc_ref[0, 0:ncr, pair(g)], qpad(g), preferred_element_type=F32) + bias_c[0:ncr]
            return jnp.dot(kw_ref[0, pl.ds(w0, nwk), pair(g)], qpad(g), preferred_element_type=F32) + bias_w

        def finish_cw(k, st):
            g = k % N_KV
            cmax = jnp.max(st, axis=0, keepdims=True)
            if k < N_KV:
                vct = jnp.concatenate([vct_ref[0, t, grow(g), :] for t in range(ncr // qb)], axis=1)
                if exact:
                    p = jnp.exp2(st - cmax)
                    p = p * (seen_c / jnp.sum(p, axis=0, keepdims=True))
                    oc_ref[g] = jnp.dot(vct, p.astype(BF16), preferred_element_type=F32)
                    psum = p[:, 0:qb]
                    for r in range(1, GROUP):
                        psum = psum + p[:, r * qb:(r + 1) * qb]
                    imp = sum(jnp.dot(ovt, part, preferred_element_type=F32) for part in _split3(psum))
                else:
                    y = jnp.dot(jnp.concatenate([with_ones(vct), ovt], axis=0), jnp.exp2(st).astype(BF16),
                                preferred_element_type=F32)
                    inv = jnp.where(seen_c > 0.0, 1.0 / y[hd:hd + 1], 0.0)
                    oc_ref[g] = y[:hd] * inv
                    imp = y[hd + 16:, 0:qb] * inv[:, 0:qb]
                    for r in range(1, GROUP):
                        imp = imp + y[hd + 16:, r * qb:(r + 1) * qb] * inv[:, r * qb:(r + 1) * qb]
                    low = jnp.where(seen_c > 0.0, -EXP_HEADROOM - cmax, -jnp.inf)
                    bad_ref[g] = jnp.maximum(bad_ref[g], jnp.maximum(cmax - EXP_HEADROOM, low))
                imps[g] = jnp.where(forced, -jnp.inf, jnp.where(valid, imp, NEG))
            else:
                vwt = jnp.concatenate([vwt_ref[0, win_lo + t, grow(g), :] for t in range(n_wb)], axis=1)
                pw = jnp.exp2(st - cmax) if exact else jnp.exp2(st)
                aw = jnp.dot(with_ones(vwt), pw.astype(BF16), preferred_element_type=F32)
                ow_ref[g] = aw[:hd] * (1.0 / aw[hd:hd + 1])
                if not exact:
                    bad_ref[g] = jnp.maximum(bad_ref[g], jnp.maximum(cmax - EXP_HEADROOM, -EXP_HEADROOM - cmax))

        _pipelined(2 * N_KV, issue_cw, finish_cw, ahead=1 if exact else 2)

        n_pick = jnp.where(few, 0, min(TOP_N, ns) - N_FORCED)
        work = jnp.stack(imps)
        half = ns // 2

        def search_lower_half():
            low = jnp.where(_topk_rows(work[:, :half], n_pick), 0.0, NEG)
            return jnp.concatenate([low, jnp.full((N_KV, ns - half, qb), NEG, F32)], axis=1)

        bias = lax.cond(s0 + qb <= half * SLC_LEN, search_lower_half,
                        lambda: jnp.where(_topk_rows(work, n_pick), 0.0, NEG))
        selb_ref[...] = jnp.where(few & valid, 0.0, bias)

    nextra = -(-(nblk + 1) // 16) * 16
    zpad = jnp.zeros((128 - nextra, nq4), BF16)


    def scores(c, g, k0, cb, rows):
        kaug = jnp.concatenate([ks_ref[0, pl.ds(k0, rows), pair(g)], eb_ref[0:rows, :]], axis=1)
        sb = tile4(selb_ref[g, pl.ds(pl.multiple_of(c * nblk, nblk), nblk), :])
        sbp = jnp.concatenate([sb, -m_ref[g], jnp.zeros((nextra - nblk - 1, nq4), F32)], axis=0).astype(BF16)
        qaug = jnp.concatenate([qpad(g), sbp, zpad], axis=0)
        sc = jnp.dot(kaug, qaug, preferred_element_type=F32)
        return sc if cb is None else sc + cb

    def vt_ext(c, g, rows):
        return with_ones(jnp.concatenate([vst_ref[0, c * nvb + t, grow(g), :] for t in range(rows // qb)], axis=1))

    def next_ref(c, r_old, cmax):
        floor = jnp.where(c == 0, -jnp.inf, 0.0)
        return (r_old + jnp.maximum(cmax, floor)).astype(BF16).astype(F32)

    def chunk(c, causal, exact, rows=KV_CHUNK):
        k0 = pl.multiple_of(c * KV_CHUNK, KV_CHUNK)
        cb = None
        if causal:
            kpos = k0 + lax.broadcasted_iota(jnp.int32, (rows, qb), 0)
            cb = tile4(jnp.where(kpos <= qpos1, 0.0, NEG))

        def finish(g, sc):
            cmax = jnp.max(sc, axis=0, keepdims=True)
            r_old = m_ref[g]
            r_new = next_ref(c, r_old, cmax)
            if exact:
                delta = r_new - r_old
                pv = jnp.dot(vt_ext(c, g, rows), jnp.exp2(sc - delta).astype(BF16), preferred_element_type=F32)
                keep = jnp.exp2(-jnp.maximum(delta, 0.0))
                acc_ref[g] = keep * acc_ref[g] + pv
            else:
                pv = jnp.dot(vt_ext(c, g, rows), jnp.exp2(sc).astype(BF16), preferred_element_type=F32)
                acc_ref[g] = (acc_ref[g] + pv) * jnp.exp2(r_old - r_new)
                low = jnp.where(c == 0, -EXP_HEADROOM - cmax, -jnp.inf)
                bad_ref[g] = jnp.maximum(bad_ref[g], jnp.maximum(cmax - EXP_HEADROOM, low))
            m_ref[g] = r_new

        _pipelined(N_KV, lambda g: scores(c, g, k0, cb, rows), finish, ahead=1 if exact else 2)

    n_full = s0 // KV_CHUNK
    n_tail = (s0 - n_full * KV_CHUNK) // qb

    def sweep(exact):
        m_ref[...] = jnp.zeros(m_ref.shape, F32)
        acc_ref[...] = jnp.zeros(acc_ref.shape, F32)

        def body(c, carry):
            chunk(c, False, exact)
            return carry

        lax.fori_loop(0, n_full, body, 0)
        for r in range(KV_CHUNK // qb):
            @pl.when(n_tail == r)
            def _(r=r):
                chunk(n_full, True, exact, rows=(r + 1) * qb)

    bad_ref[...] = jnp.full(bad_ref.shape, -jnp.inf, F32)
    if (nc // 2) % qb == 0:
        half_visible = (s0 + qb - CMP_LEN) // CMP_STRIDE + 1 <= nc // 2

        @pl.when(half_visible)
        def _():
            branches_and_selection(False, nc // 2)

        @pl.when(jnp.logical_not(half_visible))
        def _():
            branches_and_selection(False)
    else:
        branches_and_selection(False)

    sweep(False)

    @pl.when(jnp.max(bad_ref[...]) > 0.0)
    def _():
        branches_and_selection(True)
        sweep(True)

    heads_out = []
    for g in range(N_KV):
        acc = acc_ref[g]
        o_s = acc[:hd] * (1.0 / acc[hd:hd + 1])
        o_c = oc_ref[g]
        o_w = ow_ref[g]
        for r in range(GROUP):
            h = g * GROUP + r
            ls = slice(r * qb, (r + 1) * qb)
            heads_out.append(gt_ref[0, 3 * h:3 * h + 1, :] * o_c[:, ls]
                             + gt_ref[0, 3 * h + 1:3 * h + 2, :] * o_s[:, ls]
                             + gt_ref[0, 3 * h + 2:3 * h + 3, :] * o_w[:, ls])
    out_t = jnp.concatenate(heads_out, axis=0)
    o_ref[0] = out_t.T.astype(BF16)


def _attn_prompt(qt, gt, kc, vct, ovt, eb, ks, vst, kw, vwt):
    bsz, hq, t = qt.shape
    ng = gt.shape[1]
    qb = Q_BLOCK
    nq4 = GROUP * qb

    def per_b(shape):
        nd = len(shape)
        return pl.BlockSpec((1,) + tuple(shape[1:]), lambda b, i: (b,) + (0,) * (nd - 1),
                            pipeline_mode=pl.Buffered(1))

    return pl.pallas_call(
        _attn_prompt_kernel,
        grid=(bsz, t // qb),
        in_specs=[pl.BlockSpec((1, hq, qb), lambda b, i: (b, 0, i)),
                  pl.BlockSpec((1, ng, qb), lambda b, i: (b, 0, i)),
                  per_b(kc.shape), per_b(vct.shape), _const_spec(ovt.shape), _const_spec(eb.shape),
                  per_b(ks.shape), per_b(vst.shape), per_b(kw.shape), per_b(vwt.shape)],
        out_specs=pl.BlockSpec((1, qb, hq), lambda b, i: (b, i, 0)),
        out_shape=jax.ShapeDtypeStruct((bsz, t, hq), BF16),
        scratch_shapes=[pltpu.VMEM((N_KV, t // SLC_LEN, qb), F32),
                        pltpu.VMEM((N_KV, HEAD_DIM, nq4), F32),
                        pltpu.VMEM((N_KV, HEAD_DIM, nq4), F32),
                        pltpu.VMEM((N_KV, HEAD_DIM + 16, nq4), F32),
                        pltpu.VMEM((N_KV, 1, nq4), F32),
                        pltpu.VMEM((N_KV, 1, nq4), F32)],
        compiler_params=_params("parallel", "arbitrary"),
        name="nsa_prompt",
    )(qt, gt, kc, vct, ovt, eb, ks, vst, kw, vwt)


def _topk_cols(work, n_sel):
    rid = lax.broadcasted_iota(jnp.int32, work.shape, 0).astype(F32)
    for _ in range(n_sel):
        m = jnp.max(work, axis=0, keepdims=True)
        idx = jnp.min(jnp.where(work == m, rid, 1e9), axis=0, keepdims=True)
        work = jnp.where(rid == idx, -jnp.inf, work)
    return jnp.where(work == -jnp.inf, 1.0, 0.0)


def _softmax_lanes(s):
    p = jnp.exp2(s - jnp.max(s, axis=1, keepdims=True))
    return p, jnp.sum(p, axis=1, keepdims=True)


def _heads_to_lanes(o, tq):
    hd = HEAD_DIM
    nkv = N_KV * hd
    lane = lax.broadcasted_iota(jnp.int32, (tq, nkv), 1)
    pieces = []
    for h in range(N_HEADS):
        g = h // GROUP
        blk = o[h * tq:(h + 1) * tq, :]
        blk = jnp.where((lane >= g * hd) & (lane < (g + 1) * hd), blk, 0.0)
        dst = (h * hd) % nkv
        shift = (dst - g * hd) % nkv
        pieces.append(pltpu.roll(blk, shift, axis=1) if shift else blk)
    cols = []
    per = nkv // hd
    for c in range(N_HEADS // per):
        slab = pieces[c * per]
        for k in range(1, per):
            slab = slab + pieces[c * per + k]
        cols.append(slab)
    return jnp.concatenate(cols, axis=1)


def _attn_sample_kernel(pt_ref, *refs, past_len, n_ns, spb):
    del pt_ref
    npg = past_len // PAGE
    page_refs = refs[:npg * spb]
    (q_ref, g_ref, kc_ref, vc_ref, kvn_ref, win_ref, ovt_ref, ex_ref,
     o_ref, kbuf, vbuf, kwbuf, vwbuf) = refs[npg * spb:]
    hd = HEAD_DIM
    nkv = N_KV * hd
    tq = q_ref.shape[1]
    nrow = N_HEADS * tq
    ngt = N_KV * tq
    nwin = win_ref.shape[2]
    nks = kbuf.shape[2]
    nkw = kwbuf.shape[2]
    nc = kc_ref.shape[1]
    seqs = range(spb)

    ztail = jnp.zeros((PAGE - tq, nkv), F32)
    eye = (lax.broadcasted_iota(jnp.int32, (nkv, nkv), 0)
           == lax.broadcasted_iota(jnp.int32, (nkv, nkv), 1)).astype(F32).astype(BF16)
    for s in seqs:
        for k in range(npg):
            pg = page_refs[s * npg + k]
            kbuf[s, :, k * PAGE:(k + 1) * PAGE] = pg[0:nkv, :].astype(BF16)
            vbuf[s, :, k * PAGE:(k + 1) * PAGE] = pg[nkv:2 * nkv, :].astype(BF16)
        kwbuf[s, :, 0:nwin] = win_ref[s, 0:nkv, :].astype(BF16)
        vwbuf[s, :, 0:nwin] = win_ref[s, nkv:2 * nkv, :].astype(BF16)
        kvn = kvn_ref[s]

        def new_t(slot, kvn=kvn):
            rows = jnp.concatenate([kvn[:, slot * nkv:(slot + 1) * nkv], ztail], axis=0).astype(BF16)
            return lax.dot_general(eye, rows, _NT, preferred_element_type=F32).astype(BF16)

        kbuf[s, :, past_len:past_len + PAGE] = new_t(2)
        vbuf[s, :, past_len:past_len + PAGE] = new_t(3)
        kwbuf[s, :, nwin:nwin + PAGE] = new_t(4)
        vwbuf[s, :, nwin:nwin + PAGE] = new_t(5)

    lane = lax.broadcasted_iota(jnp.int32, (tq, nkv), 1)
    qbd = []
    for s in seqs:
        q = q_ref[s]
        qrows = []
        for h in range(N_HEADS):
            g, r = divmod(h, GROUP)
            slab = q[:, g * nkv:(g + 1) * nkv]
            shift = ((g - r) * hd) % nkv
            moved = pltpu.roll(slab, shift, axis=1) if shift else slab
            qrows.append(jnp.where((lane >= g * hd) & (lane < (g + 1) * hd), moved, 0.0))
        qbd.append(jnp.concatenate(qrows, axis=0).astype(BF16))

    tpos = jnp.bitwise_and(lax.broadcasted_iota(jnp.int32, (nrow, 1), 0), tq - 1)
    qpos = past_len + tpos

    s_c = [lax.dot_general(qbd[s], kc_ref[s], _NT, preferred_element_type=F32) for s in seqs]
    s_w = [jnp.dot(qbd[s], kwbuf[s], preferred_element_type=F32) for s in seqs]
    s_s = [jnp.dot(qbd[s], kbuf[s], preferred_element_type=F32) for s in seqs]

    cend = lax.broadcasted_iota(jnp.int32, (nrow, nc), 1) * CMP_STRIDE + (CMP_LEN - 1)
    mc = cend <= qpos
    nsr = -(-n_ns // 8) * 8
    jj = lax.broadcasted_iota(jnp.int32, (nsr, ngt), 0)
    qp2 = past_len + jnp.bitwise_and(lax.broadcasted_iota(jnp.int32, (1, ngt), 1), tq - 1)
    cur = jnp.right_shift(qp2, SLC_SHIFT)
    forced = (jj == 0) | (jj == cur) | (jj == cur - 1)
    valid = (jj * SLC_LEN <= qp2) & (jj < n_ns)
    o_c, imp = [], []
    for s in seqs:
        p_c, l_c = _softmax_lanes(jnp.where(mc, s_c[s], NEG))
        p_c = jnp.where(mc, p_c * (1.0 / l_c), 0.0)
        o_c.append(jnp.dot(p_c.astype(BF16), vc_ref[s], preferred_element_type=F32))
        p4 = p_c.reshape(N_KV, GROUP, tq, nc)
        psum = p4[:, 0]
        for r in range(1, GROUP):
            psum = psum + p4[:, r]
        psum = psum.reshape(ngt, nc)
        it = sum(lax.dot_general(ovt_ref[...], part, _NT, preferred_element_type=F32)
                 for part in _split3(psum))
        imp.append(jnp.where(forced, -jnp.inf, jnp.where(valid, it[:nsr], NEG)))

    jw = lax.broadcasted_iota(jnp.int32, (nrow, nkw), 1)
    kwpos = past_len + tq - (nwin + tq) + jw
    dist = qpos - kwpos
    mw = (dist >= 0) & (dist < WINDOW) & (kwpos >= 0) & (jw < nwin + tq)
    o_w = []
    for s in seqs:
        p_w, l_w = _softmax_lanes(jnp.where(mw, s_w[s], NEG))
        o_w.append(lax.dot_general(p_w.astype(BF16), vwbuf[s], _NT, preferred_element_type=F32) * (1.0 / l_w))

    nblk = ex_ref.shape[0]
    selk = []
    for s in seqs:
        sel_t = _topk_cols(imp[s], min(TOP_N, n_ns) - N_FORCED)
        sel_t = jnp.concatenate([sel_t, jnp.zeros((nblk - nsr, ngt), F32)], axis=0)
        sel_t = jnp.concatenate([sel_t, jnp.zeros((nblk, nblk - ngt), F32)], axis=1)
        sel = sel_t.T[:ngt]
        sk = jnp.dot(sel.astype(BF16), ex_ref[...], preferred_element_type=F32)
        selk.append(jnp.broadcast_to(sk.reshape(N_KV, 1, tq, nks), (N_KV, GROUP, tq, nks)).reshape(nrow, nks))

    kpos = lax.broadcasted_iota(jnp.int32, (nrow, nks), 1)
    causal = kpos <= qpos
    o_s = []
    for s in seqs:
        p_s, l_s = _softmax_lanes(jnp.where((selk[s] > 0.5) & causal, s_s[s], NEG))
        o_s.append(lax.dot_general(p_s.astype(BF16), vbuf[s], _NT, preferred_element_type=F32) * (1.0 / l_s))

    for s in seqs:
        o_ref[s] = (g_ref[s, 0] * _heads_to_lanes(o_c[s], tq) + g_ref[s, 1] * _heads_to_lanes(o_s[s], tq)
                    + g_ref[s, 2] * _heads_to_lanes(o_w[s], tq))


def _attn_sample(pages, page_table, q, gx, kc, vc, kvn, win, ovt, ex, *, past_len, spb):
    bsz, tq, hq = q.shape
    npg = past_len // PAGE
    nkv = N_KV * HEAD_DIM
    nwin = win.shape[2]
    n_ns = -(-(past_len + tq) // SLC_LEN)
    nks = past_len + PAGE
    nkw = nwin + PAGE
    kern = functools.partial(_attn_sample_kernel, past_len=past_len, n_ns=n_ns, spb=spb)

    def page_spec(s, k):
        return pl.BlockSpec((None, 2 * nkv, PAGE), lambda b, pt: (pt[b * spb + s, k], 1, 0))

    def cs(shape):
        nd = len(shape)
        return pl.BlockSpec(shape, lambda b, pt: (0,) * nd, pipeline_mode=pl.Buffered(1))

    def per_b(shape):
        nd = len(shape)
        return pl.BlockSpec((spb,) + tuple(shape[1:]), lambda b, pt: (b,) + (0,) * (nd - 1))

    in_specs = [page_spec(s, k) for s in range(spb) for k in range(npg)]
    in_specs += [per_b(q.shape), per_b(gx.shape), per_b(kc.shape), per_b(vc.shape), per_b(kvn.shape),
                 per_b(win.shape), cs(ovt.shape), cs(ex.shape)]
    grid_spec = pltpu.PrefetchScalarGridSpec(
        num_scalar_prefetch=1,
        grid=(bsz // spb,),
        in_specs=in_specs,
        out_specs=pl.BlockSpec((spb, tq, hq), lambda b, pt: (b, 0, 0)),
        scratch_shapes=[pltpu.VMEM((spb, nkv, nks), BF16), pltpu.VMEM((spb, nkv, nks), BF16),
                        pltpu.VMEM((spb, nkv, nkw), BF16), pltpu.VMEM((spb, nkv, nkw), BF16)],
    )
    return pl.pallas_call(
        kern,
        grid_spec=grid_spec,
        out_shape=jax.ShapeDtypeStruct((bsz, tq, hq), F32),
        compiler_params=_params("arbitrary"),
        name="nsa_sample",
    )(page_table, *([pages] * (npg * spb)), q, gx, kc, vc, kvn, win, ovt, ex)


def _oproj_kernel(o_ref, x_ref, mod_ref, wo_ref, g_ref, b_ref, out_ref, *, alpha):
    sb, tm, d = x_ref.shape
    ob = o_ref[...].reshape(sb * tm, -1).astype(BF16)
    mix = jnp.dot(ob, wo_ref[...], preferred_element_type=F32).reshape(sb, tm, d)
    gate = mod_ref[:, 2:3, :]
    y = alpha * x_ref[...] + (1.0 + gate) * mix
    out_ref[...] = _layer_norm(y, g_ref[...], b_ref[...])


def _oproj(o, x, mod, wo, g, b, *, sb, tm, alpha):
    bsz, t, d = x.shape
    hq = o.shape[-1]
    kern = functools.partial(_oproj_kernel, alpha=alpha)
    return pl.pallas_call(
        kern,
        grid=(bsz // sb, t // tm),
        in_specs=[pl.BlockSpec((sb, tm, hq), lambda i, j: (i, j, 0)),
                  pl.BlockSpec((sb, tm, d), lambda i, j: (i, j, 0)),
                  _mod_spec(mod, sb),
                  _const_spec(wo.shape), _const_spec(g.shape), _const_spec(b.shape)],
        out_specs=pl.BlockSpec((sb, tm, d), lambda i, j: (i, j, 0)),
        out_shape=jax.ShapeDtypeStruct((bsz, t, d), F32),
        compiler_params=_params("parallel", "parallel"),
        name="o_proj",
    )(o, x, mod[0], wo, g, b)


def _overlap(n_c, n_s):
    ci = np.arange(n_c)[:, None] * CMP_STRIDE
    sj = np.arange(n_s)[None, :] * SLC_LEN
    return ((ci <= sj + SLC_LEN - 1) & (ci + CMP_LEN - 1 >= sj)).astype(np.float32)


def kernel(x_prompt, x_sample, cache_kv, state_kv_win, state_pool, state_conv, page_table, c_prompt, c_sample, ada_w, ada_b, ln_g, ln_b, pool_w, pool_ls, ffn_w_up, ffn_b_up, ffn_w_conv, ffn_b_conv, ffn_w_down, w_kv, cmp_pe, cmp_w1, cmp_b1, cmp_w2, cmp_b2, nsa_w_qg, nsa_w_o):
    bp, t, d = x_prompt.shape
    bs, ts, _ = x_sample.shape
    depth = ada_w.shape[0]
    n_a = pool_w.shape[0]
    f2 = ffn_w_up.shape[-1]
    f = f2 // 2
    past_len = page_table.shape[1] * PAGE
    nkv = N_KV * HEAD_DIM
    hq = N_HEADS * HEAD_DIM
    alpha = float((2 * depth) ** 0.25)
    assert t % (16 * PAGE) == 0 and t >= WINDOW + Q_BLOCK
    assert ts == 8 and past_len == 16 * PAGE and state_kv_win.shape[1] == WINDOW
    assert f % 128 == 0

    tm_p = 512
    sb_pool = min(bs, 64)
    sb_ffn = min(bs, 16)

    ada = _ada(jnp.concatenate([c_sample, c_prompt], axis=0), ada_w, ada_b)
    ada = ada.reshape(depth, bs + bp, 6, d)
    mod_s = [(ada, l, 0) for l in range(depth)]
    mod_p = [(ada, l, bs) for l in range(depth)]
    assert past_len >= 2 * SLC_LEN and -(-(past_len + ts) // SLC_LEN) > TOP_N

    pool_wb = pool_w.astype(BF16)
    wup_b = ffn_w_up.astype(BF16)
    wdn_b = ffn_w_down.astype(BF16)
    w_kv_b = w_kv.astype(BF16)
    w_kvt_b = w_kv.T.astype(BF16)
    wq_b = nsa_w_qg[:, :, :hq].astype(BF16)
    wqt_b = jnp.swapaxes(nsa_w_qg[:, :, :hq], 1, 2).astype(BF16)
    wgt_b = jnp.swapaxes(nsa_w_qg[:, :, hq:], 1, 2).astype(BF16)
    wg = nsa_w_qg[:, :, hq:].reshape(-1, d, N_HEADS, 3)
    wgx_b = jnp.repeat(jnp.moveaxis(wg, 3, 1), HEAD_DIM, axis=3).astype(BF16)
    wo_b = nsa_w_o.astype(BF16)

    def vec(a):
        return a.reshape(1, -1)

    pool0 = jnp.zeros((bp, POOL_HALO, d), F32)
    conv0 = jnp.zeros((bp, CONV_HALO, f2), F32)
    pool_prev_s = jnp.pad(state_pool, ((0, 0), (0, 0), (POOL_HALO - state_pool.shape[2], 0), (0, 0)))
    conv_prev_s = jnp.pad(state_conv, ((0, 0), (0, 0), (CONV_HALO - state_conv.shape[2], 0), (0, 0)))

    xp, xs = x_prompt, x_sample
    pool_p, pool_s, conv_p, conv_s = [], [], [], []
    ctx_p = ctx_s = None
    kv_p = kv_s = None

    for l in range(depth):
        g1, b1, g2, b2 = vec(ln_g[l, 0]), vec(ln_b[l, 0]), vec(ln_g[l, 1]), vec(ln_b[l, 1])
        if l < n_a:
            xp, st = _pool_layer(xp, mod_p[l], pool0, pool_wb[l], vec(pool_ls[l]), g1, b1,
                                 sb=1, tm=tm_p, start_pos=0, alpha=alpha)
            pool_p.append(st[:, 1:])
            xs, st = _pool_layer(xs, mod_s[l], pool_prev_s[l], pool_wb[l], vec(pool_ls[l]), g1, b1,
                                 sb=sb_pool, tm=ts, start_pos=past_len, alpha=alpha)
            pool_s.append(st[:, 1:])
        else:
            jn = l - n_a
            qt, gt = _qproj_prompt(xp, mod_p[l], wqt_b[jn], wgt_b[jn], tm=tm_p)
            op = _attn_prompt(qt, gt, *ctx_p)
            xp = _oproj(op, xp, mod_p[l], wo_b[jn], g1, b1, sb=1, tm=tm_p, alpha=alpha)
            qs, gs = _qproj_sample(xs, mod_s[l], wq_b[jn], wgx_b[jn], sb=sb_pool)
            os_ = _attn_sample(ctx_s[0], page_table, qs, gs, *ctx_s[1:], past_len=past_len,
                               spb=2 if bs % 2 == 0 else 1)
            xs = _oproj(os_, xs, mod_s[l], wo_b[jn], g1, b1, sb=sb_pool, tm=ts, alpha=alpha)

        xp, st = _ffn_layer(xp, mod_p[l], conv0, wup_b[l], vec(ffn_b_up[l]), ffn_w_conv[l], vec(ffn_b_conv[l]),
                            wdn_b[l], g2, b2, sb=1, tm=tm_p, alpha=alpha)
        conv_p.append(st[:, CONV_HALO - (CONV_W - 1):])
        xs, st = _ffn_layer(xs, mod_s[l], conv_prev_s[l], wup_b[l], vec(ffn_b_up[l]), ffn_w_conv[l],
                            vec(ffn_b_conv[l]), wdn_b[l], g2, b2, sb=sb_ffn, tm=ts, alpha=alpha)
        conv_s.append(st[:, CONV_HALO - (CONV_W - 1):])

        if l == n_a - 1:
            kvpt, kvwt, cmp_rows, ks, kw, vst, vwt = _kvproj_prompt(xp, w_kv_b, w_kvt_b, tm=tm_p)
            kv_p = (kvpt, kvwt)
            tiles = t // (16 * PAGE)
            ppb = t // PAGE
            base = (np.arange(bp)[:, None] * ppb + np.arange(tiles)[None, :] * 16).reshape(-1, 1)
            ids = base + np.arange(17)[None, :]
            last = (np.arange(bp)[:, None] * ppb + ppb - 1).repeat(tiles, axis=1).reshape(-1)
            ids[:, 16] = np.minimum(ids[:, 16], last)
            kc, _, vct = _compress(cmp_rows.reshape(bp * ppb, PAGE, 2 * nkv), jnp.asarray(ids, jnp.int32),
                                   cmp_pe, cmp_w1, cmp_b1, cmp_w2, cmp_b2, transposed=False)
            n_c = tiles * 128
            ovt = jnp.asarray(_overlap(n_c, t // SLC_LEN).T, BF16)
            eb = (np.arange(KV_CHUNK)[:, None] // SLC_LEN == np.arange(128)[None, :]).astype(np.float32)
            eb[:, KV_CHUNK // SLC_LEN] = 1.0
            ctx_p = (kc.reshape(bp, n_c, nkv), vct.reshape(bp, tiles, nkv, 128), ovt, jnp.asarray(eb, BF16),
                     ks, vst, kw, vwt)

            kvs = _kvproj_sample(xs, w_kv_b, sb=sb_pool)
            kv_s = kvs
            assert (past_len + ts - CMP_LEN) // CMP_STRIDE + 1 == past_len // CMP_STRIDE - 1
            pages_t = jnp.transpose(cache_kv, (0, 2, 3, 4, 1)).reshape(cache_kv.shape[0], 4 * nkv, PAGE)
            kc_s, vc_s, _ = _compress(pages_t, page_table.astype(jnp.int32), cmp_pe, cmp_w1, cmp_b1, cmp_w2, cmp_b2,
                                      transposed=True)
            n_ns = -(-(past_len + ts) // SLC_LEN)
            ov_s = np.zeros((128, 128), np.float32)
            n_cs = (past_len + ts - CMP_LEN) // CMP_STRIDE + 1
            ov_s[:n_cs, :n_ns] = _overlap(n_cs, n_ns)
            nks = past_len + PAGE
            ex = (np.arange(nks)[None, :] // SLC_LEN == np.arange(128)[:, None]).astype(np.float32)
            win_t = jnp.transpose(state_kv_win, (0, 2, 3, 4, 1)).reshape(bs, 2 * nkv, state_kv_win.shape[1])
            ctx_s = (pages_t, kc_s, vc_s, kvs, win_t, jnp.asarray(ov_s.T, BF16), jnp.asarray(ex, BF16))

    kvpt, kvwt = kv_p
    wlen = min(WINDOW, t)
    wbuf = state_kv_win.shape[1]
    win_s = jnp.concatenate([state_kv_win, kv_s[:, :, 4 * nkv:].reshape(bs, ts, 2, N_KV, HEAD_DIM)], axis=1)[:, -wbuf:]
    kv_prompt = jnp.transpose(kvpt.reshape(bp, 4, N_KV, HEAD_DIM, t), (0, 4, 1, 2, 3))
    win_prompt = jnp.transpose(kvwt[:, :, t - wlen:].reshape(bp, 2, N_KV, HEAD_DIM, wlen), (0, 4, 1, 2, 3))
    return (xp, xs,
            kv_prompt,
            kv_s[:, :, :4 * nkv].reshape(bs, ts, 4, N_KV, HEAD_DIM),
            win_prompt,
            win_s,
            jnp.stack(pool_p), jnp.stack(pool_s), jnp.stack(conv_p), jnp.stack(conv_s))
```
